```python
import jax
import jax.numpy as jnp
from jax import lax
import numpy as np

D_MODEL = 2048
BATCH = 1
SEQ = 8192
DEPTH = 4

D_MIX = D_MODEL
HEAD_DIM = 128
HG_HEADS = D_MIX // (4 * HEAD_DIM)
HG_DK = 128
HG_DV = HEAD_DIM
HG_WIDTH = HG_HEADS * HG_DV
HG_CHUNK = 64
HG_MIN_FORGET = 1e-20
NSA_HEADS = D_MIX // (2 * HEAD_DIM)
NSA_KV_HEADS = 2
NSA_GROUP = NSA_HEADS // NSA_KV_HEADS
NSA_WIDTH = NSA_HEADS * HEAD_DIM
NSA_KV_WIDTH = NSA_KV_HEADS * HEAD_DIM
CMP_LEN = 32
CMP_STRIDE = 16
SEL_BLOCK = 64
N_SELECT = 16
WINDOW = 512
WIN_BLOCK = 128
SEL_Q_BLOCK = 64
ML_HEADS = D_MIX // (4 * HEAD_DIM)
ML_DK = HEAD_DIM
ML_DV = HEAD_DIM
ML_WIDTH = ML_HEADS * ML_DV
ML_CHUNK = 64
ML_CONV = 4
MEM_TOKENS = 256
XA_HEADS = 4
XA_HEAD_DIM = D_MODEL // XA_HEADS
D_FF = 4 * D_MODEL
ROPE_THETA = 10000.0
EPS = 1e-6
NEG_INF = -1e30
FORCE_BONUS = 1e4
IN_SIZES = ((HG_HEADS * HG_DK,) * 2 + (HG_WIDTH,) * 2
            + (NSA_WIDTH,) + (NSA_KV_WIDTH,) * 6 + (3 * NSA_HEADS,)
            + (ML_HEADS * ML_DK,) * 2 + (ML_WIDTH,) * 2 + (ML_HEADS, ML_HEADS))
D_IN = sum(IN_SIZES)

kernel_name = 'hymba_style_hgrn2_nsa_mlstm_trunk'


def rms_norm(x, gain):
    xf = x.astype(jnp.float32)
    y = xf * lax.rsqrt(jnp.mean(xf * xf, axis=-1, keepdims=True) + EPS)
    return (y * gain.astype(jnp.float32)).astype(x.dtype)


def rope(x, pos):
    half = x.shape[-1] // 2
    inv_freq = ROPE_THETA ** (-jnp.arange(half, dtype=jnp.float32) / half)
    ang = pos[:, None] * inv_freq[None, :]
    cos = jnp.cos(ang)[:, None, :]
    sin = jnp.sin(ang)[:, None, :]
    xf = x.astype(jnp.float32)
    x1, x2 = xf[..., :half], xf[..., half:]
    return jnp.concatenate([x1 * cos - x2 * sin, x2 * cos + x1 * sin], axis=-1).astype(x.dtype)


def masked_softmax(s, mask):
    p = jax.nn.softmax(jnp.where(mask, s, NEG_INF), axis=-1)
    return jnp.where(mask, p, 0.0)


def causal_conv(x, w):
    c = x.shape[-1]
    return lax.conv_general_dilated(x, w[:, None, :].astype(x.dtype), window_strides=(1,),
                                    padding=[(w.shape[0] - 1, 0)],
                                    dimension_numbers=('NWC', 'WIO', 'NWC'),
                                    feature_group_count=c)


def split_cols(z, sizes):
    parts, start = [], 0
    for s in sizes:
        parts.append(z[..., start:start + s])
        start += s
    return parts


def hgrn2_mixer(q, f_pre, i_in, g, lb, norm_gain):
    b_sz, t_len, _ = q.shape
    lbf = lb.astype(jnp.float32)
    zf = f_pre.astype(jnp.float32)
    forget = lbf + (1.0 - lbf) * jax.nn.sigmoid(zf)
    log_f = jnp.log(jnp.maximum(forget, HG_MIN_FORGET))
    key = (1.0 - lbf) * jax.nn.sigmoid(-zf)
    n_chunks = t_len // HG_CHUNK

    def chunked(a, d):
        return a.astype(jnp.float32).reshape(b_sz, n_chunks, HG_CHUNK, HG_HEADS, d).transpose(1, 0, 3, 2, 4)

    xs = (chunked(q, HG_DK), chunked(log_f, HG_DK), chunked(key, HG_DK), chunked(i_in, HG_DV))
    causal = jnp.tril(jnp.ones((HG_CHUNK, HG_CHUNK), dtype=bool))[:, :, None]

    def step(state, inp):
        qc, lfc, kc, vc = inp
        b = jnp.cumsum(lfc, axis=2)
        diff = b[:, :, :, None, :] - b[:, :, None, :, :]
        decay = jnp.where(causal, jnp.exp(jnp.where(causal, diff, 0.0)), 0.0)
        scores = jnp.einsum('bhtd,bhsd,bhtsd->bhts', qc, kc, decay)
        out = (jnp.einsum('bhtd,bhde->bhte', qc * jnp.exp(b), state)
               + jnp.einsum('bhts,bhse->bhte', scores, vc))
        b_end = b[:, :, -1:, :]
        new_state = (jnp.exp(b_end[:, :, 0, :])[..., None] * state
                     + jnp.einsum('bhsd,bhse->bhde', kc * jnp.exp(b_end - b), vc))
        return new_state, out

    s0 = jnp.zeros((b_sz, HG_HEADS, HG_DK, HG_DV), jnp.float32)
    _, o = lax.scan(step, s0, xs)
    o = o.transpose(1, 0, 3, 2, 4).reshape(b_sz, t_len, HG_HEADS, HG_DV)
    o = rms_norm(o, norm_gain.reshape(HG_HEADS, HG_DV)).reshape(b_sz, t_len, HG_WIDTH)
    return (o * jax.nn.silu(g.astype(jnp.float32))).astype(g.dtype)


def nsa_mixer(q, k_c, v_c, k_s, v_s, k_w, v_w, gate_pre, q_gain, k_gains, cmp_pos, cmp_w):
    dt = q.dtype
    b_sz, t_len, _ = q.shape
    hk, grp, d = NSA_KV_HEADS, NSA_GROUP, HEAD_DIM
    scale = d ** -0.5
    pos = jnp.arange(t_len, dtype=jnp.float32)
    qh = rope(rms_norm(q.reshape(b_sz, t_len, NSA_HEADS, d), q_gain), pos)
    qh = qh.reshape(b_sz, t_len, hk, grp, d).transpose(0, 2, 3, 1, 4)

    def kv_heads(a):
        return a.reshape(b_sz, t_len, hk, d)

    n_cmp = t_len // CMP_STRIDE - 1
    cmp_idx = np.arange(n_cmp)[:, None] * CMP_STRIDE + np.arange(CMP_LEN)[None, :]

    def compress(a, pe, w):
        blocks = kv_heads(a)[:, cmp_idx] + pe[None, None, :, None, :]
        return jnp.einsum('bclgd,lde->bcge', blocks, w)

    cmp_end = jnp.arange(n_cmp, dtype=jnp.float32) * CMP_STRIDE + (CMP_LEN - 1)
    kc = rope(rms_norm(compress(k_c, cmp_pos[0], cmp_w[0]), k_gains[0]), cmp_end).transpose(0, 2, 1, 3)
    vc = compress(v_c, cmp_pos[1], cmp_w[1]).transpose(0, 2, 1, 3)
    cmp_visible = cmp_end[None, :] <= pos[:, None]
    s_cmp = jnp.einsum('bgjtd,bgcd->bgjtc', qh, kc, preferred_element_type=jnp.float32) * scale
    p_cmp = masked_softmax(s_cmp, cmp_visible)
    o_cmp = jnp.einsum('bgjtc,bgcd->bgjtd', p_cmp.astype(dt), vc)

    n_sel = t_len // SEL_BLOCK
    k_top = min(N_SELECT, n_sel)
    c_start = np.arange(n_cmp) * CMP_STRIDE
    s_start = np.arange(n_sel) * SEL_BLOCK
    overlap = ((c_start[:, None] < s_start[None, :] + SEL_BLOCK)
               & (c_start[:, None] + CMP_LEN > s_start[None, :])).astype(np.float32)
    importance = jnp.einsum('bgjtc,cs->bgts', p_cmp, jnp.asarray(overlap))
    cur_blk = np.arange(t_len)[:, None] // SEL_BLOCK
    blk = np.arange(n_sel)[None, :]
    eligible = blk <= cur_blk
    forced = (blk == 0) | (blk == cur_blk) | (blk == cur_blk - 1)
    sel_score = jnp.where(eligible, importance + jnp.where(forced, FORCE_BONUS, 0.0), NEG_INF)
    top_val, top_idx = lax.top_k(sel_score, k_top)
    top_ok = top_val > 0.5 * NEG_INF

    k_sel = rope(rms_norm(kv_heads(k_s), k_gains[1]), pos).transpose(0, 2, 1, 3).reshape(b_sz, hk, n_sel, SEL_BLOCK, d)
    v_sel = kv_heads(v_s).transpose(0, 2, 1, 3).reshape(b_sz, hk, n_sel, SEL_BLOCK, d)
    n_qb = t_len // SEL_Q_BLOCK
    q_blocks = qh.reshape(b_sz, hk, grp, n_qb, SEL_Q_BLOCK, d).transpose(3, 0, 1, 2, 4, 5)
    idx_blocks = top_idx.reshape(b_sz, hk, n_qb, SEL_Q_BLOCK, k_top).transpose(2, 0, 1, 3, 4)
    ok_blocks = top_ok.reshape(b_sz, hk, n_qb, SEL_Q_BLOCK, k_top).transpose(2, 0, 1, 3, 4)
    t_blocks = jnp.arange(t_len).reshape(n_qb, SEL_Q_BLOCK)
    bi = jnp.arange(b_sz)[:, None, None, None]
    gi = jnp.arange(hk)[None, :, None, None]
    offs = jnp.arange(SEL_BLOCK)

    def select_block(args):
        qb, ib, okb, tb = args
        kg = k_sel[bi, gi, ib].reshape(b_sz, hk, SEL_Q_BLOCK, k_top * SEL_BLOCK, d)
        vg = v_sel[bi, gi, ib].reshape(b_sz, hk, SEL_Q_BLOCK, k_top * SEL_BLOCK, d)
        kpos = ib[..., None] * SEL_BLOCK + offs
        mask = (okb[..., None] & (kpos <= tb[None, None, :, None, None])).reshape(
            b_sz, hk, SEL_Q_BLOCK, k_top * SEL_BLOCK)
        s = jnp.einsum('bgjqd,bgqkd->bgjqk', qb, kg, preferred_element_type=jnp.float32) * scale
        p = masked_softmax(s, mask[:, :, None])
        return jnp.einsum('bgjqk,bgqkd->bgjqd', p.astype(dt), vg)

    o_sel = lax.map(select_block, (q_blocks, idx_blocks, ok_blocks, t_blocks))
    o_sel = o_sel.transpose(1, 2, 3, 0, 4, 5).reshape(b_sz, hk, grp, t_len, d)

    n_wb = t_len // WIN_BLOCK
    n_back = WINDOW // WIN_BLOCK
    band = np.arange(n_wb)[:, None] + np.arange(n_back + 1)[None, :]

    def banded(a):
        blocks = a.transpose(0, 2, 1, 3).reshape(b_sz, hk, n_wb, WIN_BLOCK, d)
        padded = jnp.pad(blocks, ((0, 0), (0, 0), (n_back, 0), (0, 0), (0, 0)))
        return padded[:, :, band].reshape(b_sz, hk, n_wb, (n_back + 1) * WIN_BLOCK, d)

    k_band = banded(rope(rms_norm(kv_heads(k_w), k_gains[2]), pos))
    v_band = banded(kv_heads(v_w))
    q_pos = np.arange(t_len).reshape(n_wb, WIN_BLOCK)
    k_pos = (np.arange(n_wb)[:, None] - n_back) * WIN_BLOCK + np.arange((n_back + 1) * WIN_BLOCK)[None, :]
    rel = q_pos[:, :, None] - k_pos[:, None, :]
    win_mask = (rel >= 0) & (rel < WINDOW) & (k_pos[:, None, :] >= 0)
    q_win = qh.reshape(b_sz, hk, grp, n_wb, WIN_BLOCK, d)
    s_win = jnp.einsum('bgjnqd,bgnkd->bgjnqk', q_win, k_band, preferred_element_type=jnp.float32) * scale
    p_win = masked_softmax(s_win, win_mask)
    o_win = jnp.einsum('bgjnqk,bgnkd->bgjnqd', p_win.astype(dt), v_band).reshape(b_sz, hk, grp, t_len, d)

    gates = jax.nn.sigmoid(gate_pre.astype(jnp.float32)).reshape(b_sz, t_len, hk, grp, 3).transpose(0, 2, 3, 1, 4)
    o = gates[..., 0:1] * o_cmp + gates[..., 1:2] * o_sel + gates[..., 2:3] * o_win
    return o.transpose(0, 3, 1, 2, 4).reshape(b_sz, t_len, NSA_WIDTH).astype(dt)


def mlstm_mixer(q, k, v, o_pre, i_pre, f_pre, conv_w, norm_gain):
    b_sz, t_len, _ = q.shape
    qk = jax.nn.silu(causal_conv(jnp.concatenate([q, k], axis=-1), conv_w))
    q, k = qk[..., :ML_HEADS * ML_DK], qk[..., ML_HEADS * ML_DK:]
    n_chunks = t_len // ML_CHUNK

    def chunked(a, d):
        return a.astype(jnp.float32).reshape(b_sz, n_chunks, ML_CHUNK, ML_HEADS, d).transpose(1, 0, 3, 2, 4)

    def chunked_gate(a):
        return a.astype(jnp.float32).reshape(b_sz, n_chunks, ML_CHUNK, ML_HEADS).transpose(1, 0, 3, 2)

    xs = (chunked(q, ML_DK) * ML_DK ** -0.5, chunked(k, ML_DK), chunked(v, ML_DV),
          chunked_gate(jax.nn.log_sigmoid(f_pre.astype(jnp.float32))),
          chunked_gate(i_pre))
    causal = jnp.tril(jnp.ones((ML_CHUNK, ML_CHUNK), dtype=bool))

    def step(carry, inp):
        c_mem, n_mem, m_prev = carry
        qc, kc, vc, lfc, lic = inp
        b = jnp.cumsum(lfc, axis=-1)
        d_log = jnp.where(causal, b[..., :, None] - b[..., None, :] + lic[..., None, :], NEG_INF)
        inter_log = b + m_prev[..., None]
        m_t = jnp.maximum(inter_log, jnp.max(d_log, axis=-1))
        w_intra = jnp.exp(d_log - m_t[..., None])
        w_inter = jnp.exp(inter_log - m_t)
        s = jnp.einsum('bhtd,bhsd->bhts', qc, kc) * w_intra
        num = (w_inter[..., None] * jnp.einsum('bhtd,bhde->bhte', qc, c_mem)
               + jnp.einsum('bhts,bhse->bhte', s, vc))
        qn = w_inter * jnp.einsum('bhtd,bhd->bht', qc, n_mem) + jnp.sum(s, axis=-1)
        h = num / jnp.maximum(jnp.abs(qn), jnp.exp(-m_t))[..., None]
        b_end = b[..., -1]
        w_state = b_end[..., None] - b + lic
        m_new = jnp.maximum(b_end + m_prev, jnp.max(w_state, axis=-1))
        carry_decay = jnp.exp(b_end + m_prev - m_new)
        k_w = kc * jnp.exp(w_state - m_new[..., None])[..., None]
        c_new = carry_decay[..., None, None] * c_mem + jnp.einsum('bhsd,bhse->bhde', k_w, vc)
        n_new = carry_decay[..., None] * n_mem + jnp.sum(k_w, axis=-2)
        return (c_new, n_new, m_new), h

    init = (jnp.zeros((b_sz, ML_HEADS, ML_DK, ML_DV), jnp.float32),
            jnp.zeros((b_sz, ML_HEADS, ML_DK), jnp.float32),
            jnp.zeros((b_sz, ML_HEADS), jnp.float32))
    _, h = lax.scan(step, init, xs)
    h = h.transpose(1, 0, 3, 2, 4).reshape(b_sz, t_len, ML_HEADS, ML_DV)
    h = rms_norm(h, norm_gain.reshape(ML_HEADS, ML_DV)).reshape(b_sz, t_len, ML_WIDTH)
    return (h * jax.nn.sigmoid(o_pre.astype(jnp.float32))).astype(o_pre.dtype)


def memory_cross_attention(h, mem_n, wq, wk, wv, wo, q_gain, k_gain):
    b_sz, t_len, _ = h.shape
    m_len = mem_n.shape[1]
    q = rms_norm((h @ wq).reshape(b_sz, t_len, XA_HEADS, XA_HEAD_DIM), q_gain)
    k = rms_norm((mem_n @ wk).reshape(b_sz, m_len, XA_HEADS, XA_HEAD_DIM), k_gain)
    v = (mem_n @ wv).reshape(b_sz, m_len, XA_HEADS, XA_HEAD_DIM)
    s = jnp.einsum('bthd,bmhd->bhtm', q, k, preferred_element_type=jnp.float32) * XA_HEAD_DIM ** -0.5
    p = jax.nn.softmax(s, axis=-1)
    o = jnp.einsum('bhtm,bmhd->bthd', p.astype(v.dtype), v).reshape(b_sz, t_len, D_MODEL)
    return o @ wo


def setup_inputs(seed: int = 0) -> dict:
    key = jax.random.key(seed)
    ks = jax.random.split(key, 26)
    f32 = jnp.float32

    def normal(k, shape, scale):
        return jax.random.normal(k, shape, f32) * scale

    def gain(k, shape):
        return 1.0 + 0.02 * jax.random.normal(k, shape, f32)

    b_in = normal(ks[3], (DEPTH, D_IN), 0.02)
    b_in = b_in.at[:, D_IN - ML_HEADS:].add(jnp.linspace(3.0, 6.0, ML_HEADS, dtype=f32))
    return {
        'x': normal(ks[0], (BATCH, SEQ, D_MODEL), 1.0),
        'mem': normal(ks[1], (BATCH, MEM_TOKENS, D_MODEL), 1.0),
        'norm_mix': gain(ks[2], (DEPTH, D_MODEL)),
        'w_in': normal(ks[4], (DEPTH, D_MODEL, D_IN), D_MODEL ** -0.5),
        'b_in': b_in,
        'hgrn_lb_logits': normal(ks[5], (DEPTH, HG_HEADS * HG_DK), 0.1),
        'hgrn_norm': gain(ks[6], (DEPTH, HG_WIDTH)),
        'nsa_q_norm': gain(ks[7], (DEPTH, HEAD_DIM)),
        'nsa_k_norm': gain(ks[8], (DEPTH, 3, HEAD_DIM)),
        'nsa_cmp_pos': normal(ks[9], (DEPTH, 2, CMP_LEN, HEAD_DIM), 0.02),
        'nsa_cmp_w': normal(ks[10], (DEPTH, 2, CMP_LEN, HEAD_DIM, HEAD_DIM), (CMP_LEN * HEAD_DIM) ** -0.5),
        'mlstm_conv': normal(ks[11], (DEPTH, ML_CONV, 2 * ML_HEADS * ML_DK), ML_CONV ** -0.5),
        'mlstm_norm': gain(ks[12], (DEPTH, ML_WIDTH)),
        'w_out': normal(ks[13], (DEPTH, D_MIX, D_MODEL), 0.5 * D_MIX ** -0.5),
        'norm_xattn': gain(ks[14], (DEPTH, D_MODEL)),
        'norm_mem': gain(ks[15], (DEPTH, D_MODEL)),
        'xa_wq': normal(ks[16], (DEPTH, D_MODEL, D_MODEL), D_MODEL ** -0.5),
        'xa_wk': normal(ks[17], (DEPTH, D_MODEL, D_MODEL), D_MODEL ** -0.5),
        'xa_wv': normal(ks[18], (DEPTH, D_MODEL, D_MODEL), D_MODEL ** -0.5),
        'xa_wo': normal(ks[19], (DEPTH, D_MODEL, D_MODEL), 0.5 * D_MODEL ** -0.5),
        'xa_q_norm': gain(ks[20], (DEPTH, XA_HEAD_DIM)),
        'xa_k_norm': gain(ks[21], (DEPTH, XA_HEAD_DIM)),
        'norm_mlp': gain(ks[22], (DEPTH, D_MODEL)),
        'mlp_w1': normal(ks[23], (DEPTH, D_MODEL, D_FF), D_MODEL ** -0.5),
        'mlp_w2': normal(ks[24], (DEPTH, D_FF, D_MODEL), 0.5 * D_FF ** -0.5),
    }


def reference(x, mem, norm_mix, w_in, b_in, hgrn_lb_logits, hgrn_norm, nsa_q_norm, nsa_k_norm,
              nsa_cmp_pos, nsa_cmp_w, mlstm_conv, mlstm_norm, w_out, norm_xattn, norm_mem,
              xa_wq, xa_wk, xa_wv, xa_wo, xa_q_norm, xa_k_norm, norm_mlp, mlp_w1, mlp_w2):
    probs = jax.nn.softmax(hgrn_lb_logits.astype(jnp.float32), axis=0)
    lower_bounds = jnp.cumsum(probs, axis=0) - probs[0:1]
    for layer in range(DEPTH):
        h = rms_norm(x, norm_mix[layer])
        z = h @ w_in[layer] + b_in[layer]
        (hg_q, hg_f, hg_i, hg_g,
         ns_q, ns_kc, ns_vc, ns_ks, ns_vs, ns_kw, ns_vw, ns_gate,
         ml_q, ml_k, ml_v, ml_o, ml_i, ml_f) = split_cols(z, IN_SIZES)
        y_hg = hgrn2_mixer(hg_q, hg_f, hg_i, hg_g, lower_bounds[layer], hgrn_norm[layer])
        y_ns = nsa_mixer(ns_q, ns_kc, ns_vc, ns_ks, ns_vs, ns_kw, ns_vw, ns_gate,
                         nsa_q_norm[layer], nsa_k_norm[layer], nsa_cmp_pos[layer], nsa_cmp_w[layer])
        y_ml = mlstm_mixer(ml_q, ml_k, ml_v, ml_o, ml_i, ml_f, mlstm_conv[layer], mlstm_norm[layer])
        y = jnp.concatenate([y_hg, y_ns, y_ml], axis=-1)
        x = x + y @ w_out[layer]
        x = x + memory_cross_attention(rms_norm(x, norm_xattn[layer]), rms_norm(mem, norm_mem[layer]),
                                       xa_wq[layer], xa_wk[layer], xa_wv[layer], xa_wo[layer],
                                       xa_q_norm[layer], xa_k_norm[layer])
        hm = rms_norm(x, norm_mlp[layer])
        x = x + jnp.square(jax.nn.relu(hm @ mlp_w1[layer])) @ mlp_w2[layer]
    return x
```

```python
import functools

import jax
import jax.numpy as jnp
import numpy as np
from jax import lax
from jax.experimental import pallas as pl
from jax.experimental.pallas import tpu as pltpu

F32 = jnp.float32
MXU_DTYPE = jnp.bfloat16

EPS = 1e-6
NEG_INF = -1e30
HEAD_DIM = 128
ROPE_THETA = 10000.0

V7X_VMEM_LIMIT_BYTES = 56 * 1024 * 1024


def _params(*semantics):
    return pltpu.CompilerParams(dimension_semantics=semantics, vmem_limit_bytes=V7X_VMEM_LIMIT_BYTES)


def _rms(x, gain):
    return x * lax.rsqrt(jnp.mean(x * x, axis=-1, keepdims=True) + EPS) * gain


def _dot(a, b):
    return jnp.dot(a.astype(MXU_DTYPE), b.astype(MXU_DTYPE), preferred_element_type=F32)


def _dot_nt(a, b):
    return lax.dot_general(a.astype(MXU_DTYPE), b.astype(MXU_DTYPE), (((1,), (1,)), ((), ())),
                           preferred_element_type=F32)


def _dot_tn(a, b):
    return lax.dot_general(a.astype(MXU_DTYPE), b.astype(MXU_DTYPE), (((0,), (0,)), ((), ())),
                           preferred_element_type=F32)


def _norm_matmul_kernel(x_ref, g_ref, w_ref, b_ref, o_ref, h_scr):
    @pl.when(pl.program_id(1) == 0)
    def _():
        h_scr[...] = _rms(x_ref[...], g_ref[...]).astype(h_scr.dtype)

    o_ref[...] = (jnp.dot(h_scr[...], w_ref[...], preferred_element_type=F32) + b_ref[...]).astype(o_ref.dtype)


def norm_matmul(x, gain, w, bias, *, tm, tn, out_dtype=F32):
    m, d = x.shape
    n = w.shape[1]
    return pl.pallas_call(
        _norm_matmul_kernel,
        grid=(m // tm, n // tn),
        in_specs=[
            pl.BlockSpec((tm, d), lambda i, j: (i, 0)),
            pl.BlockSpec((1, d), lambda i, j: (0, 0)),
            pl.BlockSpec((d, tn), lambda i, j: (0, j)),
            pl.BlockSpec((1, tn), lambda i, j: (0, j)),
        ],
        out_specs=pl.BlockSpec((tm, tn), lambda i, j: (i, j)),
        out_shape=jax.ShapeDtypeStruct((m, n), out_dtype),
        scratch_shapes=[pltpu.VMEM((tm, d), MXU_DTYPE)],
        compiler_params=_params("parallel", "arbitrary"),
        name="norm_matmul",
    )(x, gain.reshape(1, d), w, bias.reshape(1, n))


def _matmul_res_kernel(*refs, n_terms):
    ys, ws = refs[:n_terms], refs[n_terms:2 * n_terms]
    res_ref, o_ref = refs[2 * n_terms], refs[2 * n_terms + 1]
    acc = res_ref[...]
    for y_ref, w_ref in zip(ys, ws):
        acc = acc + jnp.dot(y_ref[...].astype(MXU_DTYPE), w_ref[...], preferred_element_type=F32)
    o_ref[...] = acc


def matmul_residual(terms, w, res, *, tm):
    m, n = res.shape
    kb = terms[0][3]
    y_specs = [pl.BlockSpec((tm, kb), functools.partial(lambda i, c: (i, c), c=cb)) for _, cb, _, _ in terms]
    w_specs = [pl.BlockSpec((kb, n), functools.partial(lambda i, r: (r, 0), r=rb)) for _, _, rb, _ in terms]
    return pl.pallas_call(
        functools.partial(_matmul_res_kernel, n_terms=len(terms)),
        grid=(m // tm,),
        in_specs=y_specs + w_specs + [pl.BlockSpec((tm, n), lambda i: (i, 0))],
        out_specs=pl.BlockSpec((tm, n), lambda i: (i, 0)),
        out_shape=jax.ShapeDtypeStruct((m, n), F32),
        compiler_params=_params("parallel"),
        name="matmul_residual",
    )(*[t[0] for t in terms], *([w] * len(terms)), res)


def _mlp_kernel(x_ref, g_ref, w1_ref, w2_ref, o_ref, h_scr):
    @pl.when(pl.program_id(1) == 0)
    def _():
        x = x_ref[...]
        h_scr[...] = _rms(x, g_ref[...]).astype(h_scr.dtype)
        o_ref[...] = x

    u = jnp.dot(h_scr[...], w1_ref[...], preferred_element_type=F32)
    a = jnp.square(jnp.maximum(u, 0.0)).astype(MXU_DTYPE)
    o_ref[...] += jnp.dot(a, w2_ref[...], preferred_element_type=F32)


def mlp(x, gain, w1, w2, *, tm, tf):
    m, d = x.shape
    ff = w1.shape[1]
    return pl.pallas_call(
        _mlp_kernel,
        grid=(m // tm, ff // tf),
        in_specs=[
            pl.BlockSpec((tm, d), lambda i, f: (i, 0)),
            pl.BlockSpec((1, d), lambda i, f: (0, 0)),
            pl.BlockSpec((d, tf), lambda i, f: (0, f)),
            pl.BlockSpec((tf, d), lambda i, f: (f, 0)),
        ],
        out_specs=pl.BlockSpec((tm, d), lambda i, f: (i, 0)),
        out_shape=jax.ShapeDtypeStruct((m, d), F32),
        scratch_shapes=[pltpu.VMEM((tm, d), MXU_DTYPE)],
        compiler_params=_params("parallel", "arbitrary"),
        name="mlp",
    )(x, gain.reshape(1, d), w1, w2)


def _xa_kv_kernel(mem_ref, g_ref, wk_ref, wv_ref, kg_ref, k_ref, v_ref):
    mn = _rms(mem_ref[...], g_ref[...]).astype(MXU_DTYPE)
    k = jnp.dot(mn, wk_ref[...], preferred_element_type=F32)
    k_ref[...] = _rms(k, kg_ref[...]).astype(k_ref.dtype)
    v_ref[...] = jnp.dot(mn, wv_ref[...], preferred_element_type=F32).astype(v_ref.dtype)


def xa_kv(mem, gain, wk, wv, k_gain, *, heads):
    ml, d = mem.shape
    hd = d // heads
    return pl.pallas_call(
        _xa_kv_kernel,
        grid=(heads,),
        in_specs=[
            pl.BlockSpec((ml, d), lambda h: (0, 0)),
            pl.BlockSpec((1, d), lambda h: (0, 0)),
            pl.BlockSpec((d, hd), lambda h: (0, h)),
            pl.BlockSpec((d, hd), lambda h: (0, h)),
            pl.BlockSpec((1, hd), lambda h: (0, 0)),
        ],
        out_specs=[pl.BlockSpec((ml, hd), lambda h: (0, h))] * 2,
        out_shape=[jax.ShapeDtypeStruct((ml, d), MXU_DTYPE)] * 2,
        compiler_params=_params("parallel"),
        name="xa_kv",
    )(mem, gain.reshape(1, d), wk, wv, k_gain.reshape(1, hd))


def _xa_attn_kernel(q_ref, qg_ref, k_ref, v_ref, o_ref, *, heads, hd):
    scale = hd ** -0.5
    for h in range(heads):
        cs = slice(h * hd, (h + 1) * hd)
        q = _rms(q_ref[:, cs], qg_ref[...]) * scale
        s = _dot_nt(q, k_ref[:, cs])
        p = jnp.exp(s - jnp.max(s, axis=-1, keepdims=True))
        p = p / jnp.sum(p, axis=-1, keepdims=True)
        o_ref[:, cs] = _dot(p, v_ref[:, cs]).astype(o_ref.dtype)


def xa_attn(q, q_gain, k, v, *, heads, tm):
    m, d = q.shape
    ml = k.shape[0]
    hd = d // heads
    return pl.pallas_call(
        functools.partial(_xa_attn_kernel, heads=heads, hd=hd),
        grid=(m // tm,),
        in_specs=[
            pl.BlockSpec((tm, d), lambda i: (i, 0)),
            pl.BlockSpec((1, hd), lambda i: (0, 0)),
            pl.BlockSpec((ml, d), lambda i: (0, 0)),
            pl.BlockSpec((ml, d), lambda i: (0, 0)),
        ],
        out_specs=pl.BlockSpec((tm, d), lambda i: (i, 0)),
        out_shape=jax.ShapeDtypeStruct((m, d), MXU_DTYPE),
        compiler_params=_params("parallel"),
        name="xa_attn",
    )(q, q_gain.reshape(1, hd), k, v)


def _seg_cumsum_rows(x, seg):
    pos = lax.broadcasted_iota(jnp.int32, x.shape, 0) % seg
    k = 1
    while k < seg:
        x = x + jnp.where(pos >= k, pltpu.roll(x, k, axis=0), 0.0)
        k *= 2
    return x


def _seg_cumsum_lanes(x, seg):
    pos = lax.broadcasted_iota(jnp.int32, x.shape, 1) % seg
    k = 1
    while k < seg:
        x = x + jnp.where(pos >= k, pltpu.roll(x, k, axis=1), 0.0)
        k *= 2
    return x


def _sigmoid(x):
    return 1.0 / (1.0 + jnp.exp(-x))


def _log_sigmoid(x):
    return jnp.minimum(x, 0.0) - jnp.log(1.0 + jnp.exp(-jnp.abs(x)))


HG_SUB = 16
HG_MIN_FORGET = 1e-20


def _hgrn2_kernel(q_ref, f_ref, i_ref, g_ref, lbl_ref, ng_ref, y_ref, st_ref, *, layer, heads, tb):
    dk = HEAD_DIM

    @pl.when(pl.program_id(0) == 0)
    def _():
        st_ref[...] = jnp.zeros_like(st_ref)

    logits = lbl_ref[...]
    e = jnp.exp(logits - jnp.max(logits, axis=0, keepdims=True))
    probs = e / jnp.sum(e, axis=0, keepdims=True)
    lb_all = jnp.zeros_like(probs[0:1])
    for l in range(1, layer + 1):
        lb_all = lb_all + probs[l:l + 1]
    row = lax.broadcasted_iota(jnp.int32, (HG_SUB, dk), 0)

    def body(c, carry):
        r = pl.ds(pl.multiple_of(c * HG_SUB, HG_SUB), HG_SUB)
        for h in range(heads):
            cs = slice(h * dk, (h + 1) * dk)
            lb = lb_all[:, cs]
            q, zf, v, g = q_ref[r, cs], f_ref[r, cs], i_ref[r, cs], g_ref[r, cs]
            forget = lb + (1.0 - lb) * _sigmoid(zf)
            log_f = jnp.log(jnp.maximum(forget, HG_MIN_FORGET))
            key = (1.0 - lb) * _sigmoid(-zf)
            b = _seg_cumsum_rows(log_f, HG_SUB)
            b_end = b[HG_SUB - 1:HG_SUB]
            st = st_ref[h]
            o = _dot_nt(q * jnp.exp(b), st)
            for s in range(HG_SUB):
                decay = jnp.exp(jnp.where(row >= s, b - b[s:s + 1], NEG_INF))
                a = jnp.sum(q * decay * key[s:s + 1], axis=-1, keepdims=True)
                o = o + a * v[s:s + 1]
            st_ref[h] = jnp.exp(b_end) * st + _dot_tn(v, key * jnp.exp(b_end - b))
            y = _rms(o, ng_ref[:, cs]) * (g * _sigmoid(g))
            y_ref[r, cs] = y.astype(y_ref.dtype)
        return carry

    lax.fori_loop(0, tb // HG_SUB, body, 0)


def hgrn2(z, lb_logits, norm_gain, *, layer, heads, tb):
    t = z.shape[0]
    w = heads * HEAD_DIM
    depth = lb_logits.shape[0]
    return pl.pallas_call(
        functools.partial(_hgrn2_kernel, layer=layer, heads=heads, tb=tb),
        grid=(t // tb,),
        in_specs=[pl.BlockSpec((tb, w), functools.partial(lambda i, c: (i, c), c=c)) for c in range(4)] + [
            pl.BlockSpec((depth, w), lambda i: (0, 0)),
            pl.BlockSpec((1, w), lambda i: (0, 0)),
        ],
        out_specs=pl.BlockSpec((tb, w), lambda i: (i, 0)),
        out_shape=jax.ShapeDtypeStruct((t, w), MXU_DTYPE),
        scratch_shapes=[pltpu.VMEM((heads, HEAD_DIM, HEAD_DIM), F32)],
        compiler_params=_params("arbitrary"),
        name="hgrn2",
    )(z, z, z, z, lb_logits, norm_gain.reshape(1, w))


ML_CHUNK = 64
ML_CONV = 4
ML_TAIL = 8
ZS_ML_BLOCK = 2
ZS_ML_I = 0
ZS_ML_F = 4


def _mlstm_kernel(q_ref, k_ref, v_ref, o_ref, zs_ref, gt_ref, cw_ref, ng_ref, y_ref,
                  conv_scr, c_scr, n_scr, m_scr, *, heads, tb):
    d = HEAD_DIM
    w = heads * d

    @pl.when(pl.program_id(0) == 0)
    def _():
        conv_scr[0:ML_TAIL, :] = jnp.zeros((ML_TAIL, 2 * w), F32)
        c_scr[...] = jnp.zeros_like(c_scr)
        n_scr[...] = jnp.zeros_like(n_scr)
        m_scr[...] = jnp.zeros_like(m_scr)

    conv_scr[ML_TAIL:ML_TAIL + tb, 0:w] = q_ref[...]
    conv_scr[ML_TAIL:ML_TAIL + tb, w:2 * w] = k_ref[...]
    acc = jnp.zeros((tb, 2 * w), F32)
    for j in range(ML_CONV):
        off = ML_TAIL - (ML_CONV - 1) + j
        acc = acc + conv_scr[off:off + tb, :] * cw_ref[j:j + 1, :]
    conv_scr[0:ML_TAIL, :] = conv_scr[tb:tb + ML_TAIL, :]
    qk = acc * _sigmoid(acc)

    zs = zs_ref[...]
    b_col_all = _seg_cumsum_rows(_log_sigmoid(zs), ML_CHUNK)
    gt = gt_ref[...]
    b_row_all = _seg_cumsum_lanes(_log_sigmoid(gt), ML_CHUNK)
    tri = (lax.broadcasted_iota(jnp.int32, (ML_CHUNK, ML_CHUNK), 0)
           >= lax.broadcasted_iota(jnp.int32, (ML_CHUNK, ML_CHUNK), 1))

    for c in range(tb // ML_CHUNK):
        rs = slice(c * ML_CHUNK, (c + 1) * ML_CHUNK)
        for h in range(heads):
            cs = slice(h * d, (h + 1) * d)
            qc = qk[rs, cs] * (d ** -0.5)
            kc = qk[rs, w + h * d:w + (h + 1) * d]
            vc = v_ref[rs, cs]
            b_col = b_col_all[rs, ZS_ML_F + h:ZS_ML_F + h + 1]
            b_row = b_row_all[heads + h:heads + h + 1, rs]
            li_row = gt[h:h + 1, rs]
            c_mem, n_mem, m_prev = c_scr[h], n_scr[h], m_scr[h]

            d_log = jnp.where(tri, b_col - b_row + li_row, NEG_INF)
            inter_log = b_col + m_prev
            m_t = jnp.maximum(inter_log, jnp.max(d_log, axis=-1, keepdims=True))
            w_intra = jnp.exp(d_log - m_t)
            w_inter = jnp.exp(inter_log - m_t)
            s = _dot_nt(qc, kc) * w_intra
            num = w_inter * _dot(qc, c_mem) + _dot(s, vc)
            qn = w_inter * jnp.sum(qc * n_mem, axis=-1, keepdims=True) + jnp.sum(s, axis=-1, keepdims=True)
            hcell = num / jnp.maximum(jnp.abs(qn), jnp.exp(-m_t))

            b_end = b_col[ML_CHUNK - 1:ML_CHUNK]
            w_state_col = b_end - b_col + zs[rs, ZS_ML_I + h:ZS_ML_I + h + 1]
            m_new = jnp.maximum(b_end + m_prev, jnp.max(w_state_col, axis=0, keepdims=True))
            carry_decay = jnp.exp(b_end + m_prev - m_new)
            k_w = kc * jnp.exp(w_state_col - m_new)
            c_scr[h] = carry_decay * c_mem + _dot_tn(k_w, vc)
            n_scr[h] = carry_decay * n_mem + jnp.sum(k_w, axis=0, keepdims=True)
            m_scr[h] = m_new

            y = _rms(hcell, ng_ref[:, cs]) * _sigmoid(o_ref[rs, cs])
            y_ref[rs, cs] = y.astype(y_ref.dtype)


def mlstm(z, zs, gt, conv_w, norm_gain, *, heads, tb, col0):
    t = z.shape[0]
    w = heads * HEAD_DIM
    cb = col0 // w
    return pl.pallas_call(
        functools.partial(_mlstm_kernel, heads=heads, tb=tb),
        grid=(t // tb,),
        in_specs=[pl.BlockSpec((tb, w), functools.partial(lambda i, c: (i, c), c=cb + c)) for c in range(4)] + [
            pl.BlockSpec((tb, 128), lambda i: (i, ZS_ML_BLOCK)),
            pl.BlockSpec((8, tb), lambda i: (0, i)),
            pl.BlockSpec((ML_CONV, 2 * w), lambda i: (0, 0)),
            pl.BlockSpec((1, w), lambda i: (0, 0)),
        ],
        out_specs=pl.BlockSpec((tb, w), lambda i: (i, 0)),
        out_shape=jax.ShapeDtypeStruct((t, w), MXU_DTYPE),
        scratch_shapes=[
            pltpu.VMEM((tb + ML_TAIL, 2 * w), F32),
            pltpu.VMEM((heads, HEAD_DIM, HEAD_DIM), F32),
            pltpu.VMEM((heads, 1, HEAD_DIM), F32),
            pltpu.VMEM((heads, 1, 1), F32),
        ],
        compiler_params=_params("arbitrary"),
        name="mlstm",
    )(z, z, z, z, zs, gt, conv_w, norm_gain.reshape(1, w))


NSA_GROUP = 4
CMP_LEN = 32
CMP_STRIDE = 16
SEL_BLOCK = 64
N_SELECT = 16
WINDOW = 512
FORCE_BONUS = 1e4
SEL_LANES = 128


def _rope(x, cos_t, sin_t):
    return x * cos_t + pltpu.roll(x, HEAD_DIM // 2, axis=1) * sin_t


def _nsa_prep_kernel(q_ref, ks_ref, vs_ref, kw_ref, vw_ref, cos_ref, sin_ref, et_ref, qg_ref, kg_ref,
                     qh_ref, ksa_ref, vso_ref, kwr_ref, vwo_ref, *, heads, kv_heads):
    d = HEAD_DIM
    cos_t, sin_t = cos_ref[...], sin_ref[...]
    for h in range(heads):
        cs = slice(h * d, (h + 1) * d)
        q = _rope(_rms(q_ref[:, cs], qg_ref[...]), cos_t, sin_t) * (d ** -0.5)
        qh_ref[:, cs] = q.astype(qh_ref.dtype)
    for g in range(kv_heads):
        cs = slice(g * d, (g + 1) * d)
        ksa_ref[g, :, 0:d] = _rope(_rms(ks_ref[:, cs], kg_ref[1:2]), cos_t, sin_t).astype(ksa_ref.dtype)
        ksa_ref[g, :, d:2 * d] = et_ref[...]
        kwr_ref[g] = _rope(_rms(kw_ref[:, cs], kg_ref[2:3]), cos_t, sin_t).astype(kwr_ref.dtype)
        vso_ref[g] = vs_ref[:, cs].astype(vso_ref.dtype)
        vwo_ref[g] = vw_ref[:, cs].astype(vwo_ref.dtype)


def nsa_prep(z, cos_t, sin_t, et, q_gain, k_gains, *, heads, kv_heads, tq, col_q):
    t = z.shape[0]
    d = HEAD_DIM
    kvw = kv_heads * d
    qb = col_q // (heads * d)
    kb = (col_q + heads * d) // kvw

    def zcol(width, c):
        return pl.BlockSpec((tq, width), functools.partial(lambda i, c: (i, c), c=c))

    kv_out = pl.BlockSpec((kv_heads, tq, d), lambda i: (0, i, 0))
    kv_shape = jax.ShapeDtypeStruct((kv_heads, t, d), MXU_DTYPE)
    return pl.pallas_call(
        functools.partial(_nsa_prep_kernel, heads=heads, kv_heads=kv_heads),
        grid=(t // tq,),
        in_specs=[zcol(heads * d, qb), zcol(kvw, kb + 2), zcol(kvw, kb + 3), zcol(kvw, kb + 4), zcol(kvw, kb + 5),
                  pl.BlockSpec((tq, d), lambda i: (i, 0)), pl.BlockSpec((tq, d), lambda i: (i, 0)),
                  pl.BlockSpec((tq, SEL_LANES), lambda i: (i, 0)),
                  pl.BlockSpec((1, d), lambda i: (0, 0)), pl.BlockSpec((3, d), lambda i: (0, 0))],
        out_specs=[pl.BlockSpec((tq, heads * d), lambda i: (i, 0)),
                   pl.BlockSpec((kv_heads, tq, 2 * d), lambda i: (0, i, 0)), kv_out, kv_out, kv_out],
        out_shape=[jax.ShapeDtypeStruct((t, heads * d), MXU_DTYPE),
                   jax.ShapeDtypeStruct((kv_heads, t, 2 * d), MXU_DTYPE), kv_shape, kv_shape, kv_shape],
        compiler_params=_params("parallel"),
        name="nsa_prep",
    )(z, z, z, z, z, cos_t, sin_t, et, q_gain.reshape(1, d), k_gains)


def _nsa_compress_kernel(zk_ref, zv_ref, pe_ref, w_ref, kg_ref, cos_ref, sin_ref, kc_ref, vc_ref, *, ncp):
    d = HEAD_DIM
    half = CMP_LEN // 2
    row = lax.broadcasted_iota(jnp.int32, (ncp, d), 0)

    def compress(z_ref, which):
        lo = jnp.zeros((ncp, d), F32)
        hi = jnp.zeros((ncp, d), F32)
        for l in range(half):
            xl = z_ref[pl.ds(l, ncp, stride=CMP_STRIDE), :]
            lo = lo + _dot(xl + pe_ref[which, l:l + 1, :], w_ref[which, l])
            hi = hi + _dot(xl + pe_ref[which, half + l:half + l + 1, :], w_ref[which, half + l])
        out = lo + pltpu.roll(hi, ncp - 1, axis=0)
        return jnp.where(row < ncp - 1, out, 0.0)

    kc_ref[0] = _rope(_rms(compress(zk_ref, 0), kg_ref[0:1]), cos_ref[...], sin_ref[...]).astype(kc_ref.dtype)
    vc_ref[0] = compress(zv_ref, 1).astype(vc_ref.dtype)


def nsa_compress(z, cmp_pos, cmp_w, k_gains, cos_c, sin_c, *, kv_heads, col_kc):
    t = z.shape[0]
    d = HEAD_DIM
    ncp = t // CMP_STRIDE
    kb = col_kc // d
    out_spec = pl.BlockSpec((1, ncp, d), lambda g: (g, 0, 0))
    out_shape = jax.ShapeDtypeStruct((kv_heads, ncp, d), MXU_DTYPE)
    return pl.pallas_call(
        functools.partial(_nsa_compress_kernel, ncp=ncp),
        grid=(kv_heads,),
        in_specs=[pl.BlockSpec((t, d), lambda g: (0, kb + g)),
                  pl.BlockSpec((t, d), lambda g: (0, kb + kv_heads + g)),
                  pl.BlockSpec((2, CMP_LEN, d), lambda g: (0, 0, 0)),
                  pl.BlockSpec((2, CMP_LEN, d, d), lambda g: (0, 0, 0, 0)),
                  pl.BlockSpec((3, d), lambda g: (0, 0)),
                  pl.BlockSpec((ncp, d), lambda g: (0, 0)), pl.BlockSpec((ncp, d), lambda g: (0, 0))],
        out_specs=[out_spec, out_spec],
        out_shape=[out_shape, out_shape],
        compiler_params=_params("parallel"),
        name="nsa_compress",
    )(z, z, cmp_pos, cmp_w, k_gains, cos_c, sin_c)


def _stack_heads(q_ref, grp):
    return jnp.concatenate([q_ref[:, j * HEAD_DIM:(j + 1) * HEAD_DIM] for j in range(grp)], axis=0)


def _nsa_cmp_kernel(q_ref, kc_ref, vc_ref, ov_ref, ocmp_ref, mb_ref, *, tq, grp, ncp):
    d = HEAD_DIM
    qi = pl.program_id(1)
    rows = grp * tq
    s = _dot_nt(_stack_heads(q_ref, grp), kc_ref[0])
    tok = qi * tq + lax.broadcasted_iota(jnp.int32, (rows, ncp), 0) % tq
    cmp_end = lax.broadcasted_iota(jnp.int32, (rows, ncp), 1) * CMP_STRIDE + (CMP_LEN - 1)
    vis = cmp_end <= tok
    sm = jnp.where(vis, s, NEG_INF)
    p = jnp.where(vis, jnp.exp(sm - jnp.max(sm, axis=-1, keepdims=True)), 0.0)
    l = jnp.sum(p, axis=-1, keepdims=True)
    p = p / jnp.where(l > 0.0, l, 1.0)
    o = _dot(p, vc_ref[0])
    psum = p[0:tq]
    for j in range(grp):
        ocmp_ref[:, j * d:(j + 1) * d] = o[j * tq:(j + 1) * tq]
        if j:
            psum = psum + p[j * tq:(j + 1) * tq]
    hi = psum.astype(MXU_DTYPE)
    lo = (psum - hi.astype(F32)).astype(MXU_DTYPE)
    imp = (jnp.dot(hi, ov_ref[...], preferred_element_type=F32)
           + jnp.dot(lo, ov_ref[...], preferred_element_type=F32))
    blk = lax.broadcasted_iota(jnp.int32, (tq, SEL_LANES), 1)
    cur = (qi * tq + lax.broadcasted_iota(jnp.int32, (tq, SEL_LANES), 0)) // SEL_BLOCK
    forced = (blk == 0) | (blk == cur) | (blk == cur - 1)
    score = jnp.where(blk <= cur, imp + jnp.where(forced, FORCE_BONUS, 0.0), NEG_INF)
    blk_f = blk.astype(F32)
    bias = jnp.full((tq, SEL_LANES), NEG_INF, F32)
    for _ in range(N_SELECT):
        mx = jnp.max(score, axis=-1, keepdims=True)
        first = jnp.min(jnp.where(score == mx, blk_f, float(SEL_LANES)), axis=-1, keepdims=True)
        pick = blk_f == first
        bias = jnp.where(pick, jnp.where(mx > 0.5 * NEG_INF, 0.0, NEG_INF), bias)
        score = jnp.where(pick, -jnp.inf, score)
    mb_ref[0] = bias.astype(mb_ref.dtype)


def nsa_cmp_select(qh, kc, vc, ov, *, kv_heads, tq):
    t = qh.shape[0]
    d = HEAD_DIM
    ncp = kc.shape[1]
    gw = NSA_GROUP * d
    return pl.pallas_call(
        functools.partial(_nsa_cmp_kernel, tq=tq, grp=NSA_GROUP, ncp=ncp),
        grid=(kv_heads, t // tq),
        in_specs=[pl.BlockSpec((tq, gw), lambda g, i: (i, g)),
                  pl.BlockSpec((1, ncp, d), lambda g, i: (g, 0, 0)),
                  pl.BlockSpec((1, ncp, d), lambda g, i: (g, 0, 0)),
                  pl.BlockSpec((ncp, SEL_LANES), lambda g, i: (0, 0))],
        out_specs=[pl.BlockSpec((tq, gw), lambda g, i: (i, g)),
                   pl.BlockSpec((1, tq, SEL_LANES), lambda g, i: (g, i, 0))],
        out_shape=[jax.ShapeDtypeStruct((t, kv_heads * gw), F32),
                   jax.ShapeDtypeStruct((kv_heads, t, SEL_LANES), MXU_DTYPE)],
        compiler_params=_params("parallel", "parallel"),
        name="nsa_cmp_select",
    )(qh, kc, vc, ov)


def _nsa_sel_kernel(qi_tab, kj_tab, q_ref, mb_ref, ka_ref, v_ref, o_ref, qa_scr, m_scr, l_scr, acc_scr,
                    *, tq, tk, grp):
    d = HEAD_DIM
    step = pl.program_id(1)
    qi, kj = qi_tab[step], kj_tab[step]
    last = (qi * tq + tq - 1) // tk
    rows = grp * tq

    @pl.when(kj == 0)
    def _():
        for j in range(grp):
            qa_scr[j * tq:(j + 1) * tq, 0:d] = q_ref[:, j * d:(j + 1) * d]
            qa_scr[j * tq:(j + 1) * tq, d:2 * d] = mb_ref[0]
        m_scr[...] = jnp.full_like(m_scr, NEG_INF)
        l_scr[...] = jnp.zeros_like(l_scr)
        acc_scr[...] = jnp.zeros_like(acc_scr)

    def update(diagonal):
        s = lax.dot_general(qa_scr[...], ka_ref[0], (((1,), (1,)), ((), ())), preferred_element_type=F32)
        if diagonal:
            tok = qi * tq + lax.broadcasted_iota(jnp.int32, (rows, tk), 0) % tq
            kpos = kj * tk + lax.broadcasted_iota(jnp.int32, (rows, tk), 1)
            s = jnp.where(kpos <= tok, s, NEG_INF)
        m_prev = m_scr[...]
        m_new = jnp.maximum(m_prev, jnp.max(s, axis=-1, keepdims=True))
        alpha = jnp.exp(m_prev - m_new)
        p = jnp.exp(s - m_new)
        l_scr[...] = alpha * l_scr[...] + jnp.sum(p, axis=-1, keepdims=True)
        acc_scr[...] = alpha * acc_scr[...] + _dot(p, v_ref[0])
        m_scr[...] = m_new

    @pl.when(kj < last)
    def _():
        update(False)

    @pl.when(kj == last)
    def _():
        update(True)
        o = acc_scr[...] / l_scr[...]
        for j in range(grp):
            o_ref[:, j * d:(j + 1) * d] = o[j * tq:(j + 1) * tq]


def nsa_sel_attn(qh, mb, ks_aug, vs, *, kv_heads, tq, tk):
    t = qh.shape[0]
    d = HEAD_DIM
    gw = NSA_GROUP * d
    pairs = [(qi, kj) for qi in range(t // tq) for kj in range((qi * tq + tq - 1) // tk + 1)]
    qi_tab = jnp.asarray(np.array([p[0] for p in pairs], np.int32))
    kj_tab = jnp.asarray(np.array([p[1] for p in pairs], np.int32))
    rows = NSA_GROUP * tq
    grid_spec = pltpu.PrefetchScalarGridSpec(
        num_scalar_prefetch=2,
        grid=(kv_heads, len(pairs)),
        in_specs=[pl.BlockSpec((tq, gw), lambda g, s, qt, kt: (qt[s], g)),
                  pl.BlockSpec((1, tq, SEL_LANES), lambda g, s, qt, kt: (g, qt[s], 0)),
                  pl.BlockSpec((1, tk, 2 * d), lambda g, s, qt, kt: (g, kt[s], 0)),
                  pl.BlockSpec((1, tk, d), lambda g, s, qt, kt: (g, kt[s], 0))],
        out_specs=pl.BlockSpec((tq, gw), lambda g, s, qt, kt: (qt[s], g)),
        scratch_shapes=[pltpu.VMEM((rows, 2 * d), MXU_DTYPE), pltpu.VMEM((rows, 1), F32),
                        pltpu.VMEM((rows, 1), F32), pltpu.VMEM((rows, d), F32)],
    )
    return pl.pallas_call(
        functools.partial(_nsa_sel_kernel, tq=tq, tk=tk, grp=NSA_GROUP),
        grid_spec=grid_spec,
        out_shape=jax.ShapeDtypeStruct((t, kv_heads * gw), F32),
        compiler_params=_params("parallel", "arbitrary"),
        name="nsa_sel_attn",
    )(qi_tab, kj_tab, qh, mb, ks_aug, vs)


def _nsa_win_kernel(q_ref, k0_ref, k1_ref, k2_ref, v0_ref, v1_ref, v2_ref, ocmp_ref, osel_ref, zg_ref, y_ref,
                    *, tq, grp):
    d = HEAD_DIM
    qi = pl.program_id(1)
    rows = grp * tq
    k = jnp.concatenate([k0_ref[0], k1_ref[0], k2_ref[0]], axis=0)
    v = jnp.concatenate([v0_ref[0], v1_ref[0], v2_ref[0]], axis=0)
    s = _dot_nt(_stack_heads(q_ref, grp), k)
    tok = qi * tq + lax.broadcasted_iota(jnp.int32, (rows, 3 * tq), 0) % tq
    kpos = (qi - 2) * tq + lax.broadcasted_iota(jnp.int32, (rows, 3 * tq), 1)
    rel = tok - kpos
    ok = (rel >= 0) & (rel < WINDOW) & (kpos >= 0)
    sm = jnp.where(ok, s, NEG_INF)
    p = jnp.where(ok, jnp.exp(sm - jnp.max(sm, axis=-1, keepdims=True)), 0.0)
    o = _dot(p / jnp.sum(p, axis=-1, keepdims=True), v)
    gates = _sigmoid(zg_ref[...])
    for j in range(grp):
        cs = slice(j * d, (j + 1) * d)
        y = (gates[:, 3 * j:3 * j + 1] * ocmp_ref[:, cs] + gates[:, 3 * j + 1:3 * j + 2] * osel_ref[:, cs]
             + gates[:, 3 * j + 2:3 * j + 3] * o[j * tq:(j + 1) * tq])
        y_ref[:, cs] = y.astype(y_ref.dtype)


def nsa_win_combine(qh, kw, vw, o_cmp, o_sel, zs, *, kv_heads, tq):
    t = qh.shape[0]
    d = HEAD_DIM
    gw = NSA_GROUP * d
    kv_specs = [pl.BlockSpec((1, tq, d), functools.partial(lambda g, i, b: (g, jnp.maximum(i - b, 0), 0), b=b))
                for b in (2, 1, 0)]
    gspec = pl.BlockSpec((tq, gw), lambda g, i: (i, g))
    return pl.pallas_call(
        functools.partial(_nsa_win_kernel, tq=tq, grp=NSA_GROUP),
        grid=(kv_heads, t // tq),
        in_specs=[gspec] + kv_specs + kv_specs + [gspec, gspec, pl.BlockSpec((tq, 128), lambda g, i: (i, g))],
        out_specs=gspec,
        out_shape=jax.ShapeDtypeStruct((t, kv_heads * gw), MXU_DTYPE),
        compiler_params=_params("parallel", "parallel"),
        name="nsa_win_combine",
    )(qh, kw, kw, kw, vw, vw, vw, o_cmp, o_sel, zs)


def _split_in_proj(w_in, b_in, sizes):
    starts = np.concatenate([[0], np.cumsum(sizes)])
    i_gate, i_mli, i_mlf = 11, 16, 17
    g0, g1 = starts[i_gate], starts[i_gate + 1]
    half = (g1 - g0) // 2

    def pad(a, width):
        return jnp.pad(a, [(0, 0)] * (a.ndim - 1) + [(0, width - a.shape[-1])])

    def arrange(a):
        main = jnp.concatenate([a[..., :g0], a[..., g1:starts[i_mli]]], axis=-1)
        small = jnp.concatenate([pad(a[..., g0:g0 + half], 128), pad(a[..., g0 + half:g1], 128),
                                 pad(a[..., starts[i_mli]:starts[i_mlf + 1]], 128)], axis=-1)
        return main, small

    return arrange(w_in), arrange(b_in)


def kernel(x, mem, norm_mix, w_in, b_in, hgrn_lb_logits, hgrn_norm, nsa_q_norm, nsa_k_norm, nsa_cmp_pos, nsa_cmp_w, mlstm_conv, mlstm_norm, w_out, norm_xattn, norm_mem, xa_wq, xa_wk, xa_wv, xa_wo, xa_q_norm, xa_k_norm, norm_mlp, mlp_w1, mlp_w2):
    _, t, d_model = x.shape
    depth = w_in.shape[0]
    d = HEAD_DIM
    hg_heads = ml_heads = d_model // (4 * d)
    nsa_heads = d_model // (2 * d)
    kv_heads = nsa_heads // NSA_GROUP
    hw, nw, kvw = hg_heads * d, nsa_heads * d, kv_heads * d
    sizes = (hw,) * 4 + (nw,) + (kvw,) * 6 + (3 * nsa_heads,) + (hw,) * 4 + (ml_heads, ml_heads)
    col_nsa_q = 4 * hw
    col_nsa_kc = col_nsa_q + nw
    col_ml = col_nsa_kc + 6 * kvw
    xa_heads = 4
    bf = MXU_DTYPE

    half = d // 2
    inv_freq = ROPE_THETA ** (-jnp.arange(half, dtype=F32) / half)

    def rope_tables(pos):
        ang = pos[:, None] * inv_freq[None, :]
        cos, sin = jnp.cos(ang), jnp.sin(ang)
        return jnp.concatenate([cos, cos], axis=-1), jnp.concatenate([-sin, sin], axis=-1)

    cos_t, sin_t = rope_tables(jnp.arange(t, dtype=F32))
    ncp = t // CMP_STRIDE
    n_cmp, n_sel = ncp - 1, t // SEL_BLOCK
    cos_c, sin_c = rope_tables(jnp.arange(ncp, dtype=F32) * CMP_STRIDE + (CMP_LEN - 1))
    et = jnp.asarray(np.arange(t)[:, None] // SEL_BLOCK == np.arange(SEL_LANES)[None, :], bf)
    c_start = np.arange(ncp)[:, None] * CMP_STRIDE
    s_start = np.arange(SEL_LANES)[None, :] * SEL_BLOCK
    overlap = ((c_start < s_start + SEL_BLOCK) & (c_start + CMP_LEN > s_start)
               & (np.arange(ncp)[:, None] < n_cmp) & (np.arange(SEL_LANES)[None, :] < n_sel))
    ov = jnp.asarray(overlap, bf)

    h = x.reshape(t, d_model)
    mem2 = mem.reshape(mem.shape[1], d_model)
    for layer in range(depth):
        (w_main, w_small), (b_main, b_small) = _split_in_proj(w_in[layer], b_in[layer], sizes)
        z = norm_matmul(h, norm_mix[layer], w_main.astype(bf), b_main, tm=512, tn=w_main.shape[1] // 4)
        zs = norm_matmul(h, norm_mix[layer], w_small.astype(bf), b_small, tm=512, tn=w_small.shape[1])
        gt = zs[:, 2 * 128:2 * 128 + 8].T

        y_hg = hgrn2(z, hgrn_lb_logits, hgrn_norm[layer], layer=layer, heads=hg_heads, tb=512)
        y_ml = mlstm(z, zs, gt, mlstm_conv[layer], mlstm_norm[layer], heads=ml_heads, tb=512, col0=col_ml)

        qh, ks_aug, vs, kw, vw = nsa_prep(z, cos_t, sin_t, et, nsa_q_norm[layer], nsa_k_norm[layer],
                                          heads=nsa_heads, kv_heads=kv_heads, tq=512, col_q=col_nsa_q)
        kc, vc = nsa_compress(z, nsa_cmp_pos[layer], nsa_cmp_w[layer].astype(bf), nsa_k_norm[layer],
                              cos_c, sin_c, kv_heads=kv_heads, col_kc=col_nsa_kc)
        o_cmp, mb = nsa_cmp_select(qh, kc, vc, ov, kv_heads=kv_heads, tq=128)
        o_sel = nsa_sel_attn(qh, mb, ks_aug, vs, kv_heads=kv_heads, tq=128, tk=512)
        y_ns = nsa_win_combine(qh, kw, vw, o_cmp, o_sel, zs, kv_heads=kv_heads, tq=256)

        kb = hw
        terms = [(y_hg, 0, 0, kb)] + [(y_ns, c, 1 + c, kb) for c in range(nw // kb)] + [(y_ml, 0, 1 + nw // kb, kb)]
        h = matmul_residual(terms, w_out[layer].astype(bf), h, tm=512)

        k_mem, v_mem = xa_kv(mem2, norm_mem[layer], xa_wk[layer].astype(bf), xa_wv[layer].astype(bf),
                             xa_k_norm[layer], heads=xa_heads)
        q_x = norm_matmul(h, norm_xattn[layer], xa_wq[layer].astype(bf), jnp.zeros((d_model,), F32),
                          tm=512, tn=d_model)
        o_x = xa_attn(q_x, xa_q_norm[layer], k_mem, v_mem, heads=xa_heads, tm=512)
        h = matmul_residual([(o_x, c, c, kb) for c in range(d_model // kb)], xa_wo[layer].astype(bf), h, tm=512)

        h = mlp(h, norm_mlp[layer], mlp_w1[layer].astype(bf), mlp_w2[layer].astype(bf), tm=512, tf=1024)
    return h.reshape(x.shape)
```

```python
import functools

import jax
import jax.numpy as jnp
import numpy as np
from jax import lax
from jax.experimental import pallas as pl
from jax.experimental.pallas import tpu as pltpu

F32 = jnp.float32
MXU_DTYPE = jnp.bfloat16

EPS = 1e-6
NEG_INF = -1e30
HEAD_DIM = 128
ROPE_THETA = 10000.0

V7X_VMEM_LIMIT_BYTES = 56 * 1024 * 1024


def _params(*semantics):
    return pltpu.CompilerParams(dimension_semantics=semantics, vmem_limit_bytes=V7X_VMEM_LIMIT_BYTES)


def _rms(x, gain):
    return x * lax.rsqrt(jnp.mean(x * x, axis=-1, keepdims=True) + EPS) * gain


def _dot(a, b):
    return jnp.dot(a.astype(MXU_DTYPE), b.astype(MXU_DTYPE), preferred_element_type=F32)


def _dot_nt(a, b):
    return lax.dot_general(a.astype(MXU_DTYPE), b.astype(MXU_DTYPE), (((1,), (1,)), ((), ())),
                           preferred_element_type=F32)


def _dot_tn(a, b):
    return lax.dot_general(a.astype(MXU_DTYPE), b.astype(MXU_DTYPE), (((0,), (0,)), ((), ())),
                           preferred_element_type=F32)


def _norm_matmul_kernel(x_ref, g_ref, w_ref, b_ref, o_ref, h_scr):
    @pl.when(pl.program_id(1) == 0)
    def _():
        h_scr[...] = _rms(x_ref[...], g_ref[...]).astype(h_scr.dtype)

    o_ref[...] = (jnp.dot(h_scr[...], w_ref[...], preferred_element_type=F32) + b_ref[...]).astype(o_ref.dtype)


def norm_matmul(x, gain, w, bias, *, tm, tn, out_dtype=F32):
    m, d = x.shape
    n = w.shape[1]
    return pl.pallas_call(
        _norm_matmul_kernel,
        grid=(m // tm, n // tn),
        in_specs=[
            pl.BlockSpec((tm, d), lambda i, j: (i, 0)),
            pl.BlockSpec((1, d), lambda i, j: (0, 0)),
            pl.BlockSpec((d, tn), lambda i, j: (0, j)),
            pl.BlockSpec((1, tn), lambda i, j: (0, j)),
        ],
        out_specs=pl.BlockSpec((tm, tn), lambda i, j: (i, j)),
        out_shape=jax.ShapeDtypeStruct((m, n), out_dtype),
        scratch_shapes=[pltpu.VMEM((tm, d), MXU_DTYPE)],
        compiler_params=_params("parallel", "arbitrary"),
        name="norm_matmul",
    )(x, gain.reshape(1, d), w, bias.reshape(1, n))


def _matmul_res_kernel(*refs, n_terms):
    ys, ws = refs[:n_terms], refs[n_terms:2 * n_terms]
    res_ref, o_ref = refs[2 * n_terms], refs[2 * n_terms + 1]
    acc = res_ref[...]
    for y_ref, w_ref in zip(ys, ws):
        acc = acc + jnp.dot(y_ref[...].astype(MXU_DTYPE), w_ref[...], preferred_element_type=F32)
    o_ref[...] = acc


def matmul_residual(terms, w, res, *, tm):
    m, n = res.shape
    kb = terms[0][3]
    y_specs = [pl.BlockSpec((tm, kb), functools.partial(lambda i, c: (i, c), c=cb)) for _, cb, _, _ in terms]
    w_specs = [pl.BlockSpec((kb, n), functools.partial(lambda i, r: (r, 0), r=rb)) for _, _, rb, _ in terms]
    return pl.pallas_call(
        functools.partial(_matmul_res_kernel, n_terms=len(terms)),
        grid=(m // tm,),
        in_specs=y_specs + w_specs + [pl.BlockSpec((tm, n), lambda i: (i, 0))],
        out_specs=pl.BlockSpec((tm, n), lambda i: (i, 0)),
        out_shape=jax.ShapeDtypeStruct((m, n), F32),
        compiler_params=_params("parallel"),
        name="matmul_residual",
    )(*[t[0] for t in terms], *([w] * len(terms)), res)


def _mlp_kernel(x_ref, g_ref, w1_ref, w2_ref, o_ref, h_scr):
    @pl.when(pl.program_id(1) == 0)
    def _():
        x = x_ref[...]
        h_scr[...] = _rms(x, g_ref[...]).astype(h_scr.dtype)
        o_ref[...] = x

    u = jnp.dot(h_scr[...], w1_ref[...], preferred_element_type=F32)
    a = jnp.square(jnp.maximum(u, 0.0)).astype(MXU_DTYPE)
    o_ref[...] += jnp.dot(a, w2_ref[...], preferred_element_type=F32)


def mlp(x, gain, w1, w2, *, tm, tf):
    m, d = x.shape
    ff = w1.shape[1]
    return pl.pallas_call(
        _mlp_kernel,
        grid=(m // tm, ff // tf),
        in_specs=[
            pl.BlockSpec((tm, d), lambda i, f: (i, 0)),
            pl.BlockSpec((1, d), lambda i, f: (0, 0)),
            pl.BlockSpec((d, tf), lambda i, f: (0, f)),
            pl.BlockSpec((tf, d), lambda i, f: (f, 0)),
        ],
        out_specs=pl.BlockSpec((tm, d), lambda i, f: (i, 0)),
        out_shape=jax.ShapeDtypeStruct((m, d), F32),
        scratch_shapes=[pltpu.VMEM((tm, d), MXU_DTYPE)],
        compiler_params=_params("parallel", "arbitrary"),
        name="mlp",
    )(x, gain.reshape(1, d), w1, w2)


def _xa_kv_kernel(mem_ref, g_ref, wk_ref, wv_ref, kg_ref, k_ref, v_ref):
    mn = _rms(mem_ref[...], g_ref[...]).astype(MXU_DTYPE)
    k = jnp.dot(mn, wk_ref[...], preferred_element_type=F32)
    k_ref[...] = _rms(k, kg_ref[...]).astype(k_ref.dtype)
    v_ref[...] = jnp.dot(mn, wv_ref[...], preferred_element_type=F32).astype(v_ref.dtype)


def xa_kv(mem, gain, wk, wv, k_gain, *, heads):
    ml, d = mem.shape
    hd = d // heads
    return pl.pallas_call(
        _xa_kv_kernel,
        grid=(heads,),
        in_specs=[
            pl.BlockSpec((ml, d), lambda h: (0, 0)),
            pl.BlockSpec((1, d), lambda h: (0, 0)),
            pl.BlockSpec((d, hd), lambda h: (0, h)),
            pl.BlockSpec((d, hd), lambda h: (0, h)),
            pl.BlockSpec((1, hd), lambda h: (0, 0)),
        ],
        out_specs=[pl.BlockSpec((ml, hd), lambda h: (0, h))] * 2,
        out_shape=[jax.ShapeDtypeStruct((ml, d), MXU_DTYPE)] * 2,
        compiler_params=_params("parallel"),
        name="xa_kv",
    )(mem, gain.reshape(1, d), wk, wv, k_gain.reshape(1, hd))


def _xa_attn_kernel(q_ref, qg_ref, k_ref, v_ref, o_ref, *, heads, hd):
    scale = hd ** -0.5
    for h in range(heads):
        cs = slice(h * hd, (h + 1) * hd)
        q = _rms(q_ref[:, cs], qg_ref[...]) * scale
        s = _dot_nt(q, k_ref[:, cs])
        p = jnp.exp(s - jnp.max(s, axis=-1, keepdims=True))
        p = p / jnp.sum(p, axis=-1, keepdims=True)
        o_ref[:, cs] = _dot(p, v_ref[:, cs]).astype(o_ref.dtype)


def xa_attn(q, q_gain, k, v, *, heads, tm):
    m, d = q.shape
    ml = k.shape[0]
    hd = d // heads
    return pl.pallas_call(
        functools.partial(_xa_attn_kernel, heads=heads, hd=hd),
        grid=(m // tm,),
        in_specs=[
            pl.BlockSpec((tm, d), lambda i: (i, 0)),
            pl.BlockSpec((1, hd), lambda i: (0, 0)),
            pl.BlockSpec((ml, d), lambda i: (0, 0)),
            pl.BlockSpec((ml, d), lambda i: (0, 0)),
        ],
        out_specs=pl.BlockSpec((tm, d), lambda i: (i, 0)),
        out_shape=jax.ShapeDtypeStruct((m, d), MXU_DTYPE),
        compiler_params=_params("parallel"),
        name="xa_attn",
    )(q, q_gain.reshape(1, hd), k, v)


def _seg_cumsum_rows(x, seg):
    pos = lax.broadcasted_iota(jnp.int32, x.shape, 0) % seg
    k = 1
    while k < seg:
        x = x + jnp.where(pos >= k, pltpu.roll(x, k, axis=0), 0.0)
        k *= 2
    return x


def _seg_cumsum_lanes(x, seg):
    pos = lax.broadcasted_iota(jnp.int32, x.shape, 1) % seg
    k = 1
    while k < seg:
        x = x + jnp.where(pos >= k, pltpu.roll(x, k, axis=1), 0.0)
        k *= 2
    return x


def _sigmoid(x):
    return 1.0 / (1.0 + jnp.exp(-x))


def _log_sigmoid(x):
    return jnp.minimum(x, 0.0) - jnp.log(1.0 + jnp.exp(-jnp.abs(x)))


HG_SUB = 16
HG_MIN_FORGET = 1e-20


def _hgrn2_kernel(q_ref, f_ref, i_ref, g_ref, lbl_ref, ng_ref, y_ref, st_ref, *, layer, heads, tb):
    dk = HEAD_DIM

    @pl.when(pl.program_id(0) == 0)
    def _():
        st_ref[...] = jnp.zeros_like(st_ref)

    logits = lbl_ref[...]
    e = jnp.exp(logits - jnp.max(logits, axis=0, keepdims=True))
    probs = e / jnp.sum(e, axis=0, keepdims=True)
    lb_all = jnp.zeros_like(probs[0:1])
    for l in range(1, layer + 1):
        lb_all = lb_all + probs[l:l + 1]
    row = lax.broadcasted_iota(jnp.int32, (HG_SUB, dk), 0)

    def body(c, carry):
        r = pl.ds(pl.multiple_of(c * HG_SUB, HG_SUB), HG_SUB)
        for h in range(heads):
            cs = slice(h * dk, (h + 1) * dk)
            lb = lb_all[:, cs]
            q, zf, v, g = q_ref[r, cs], f_ref[r, cs], i_ref[r, cs], g_ref[r, cs]
            forget = lb + (1.0 - lb) * _sigmoid(zf)
            log_f = jnp.log(jnp.maximum(forget, HG_MIN_FORGET))
            key = (1.0 - lb) * _sigmoid(-zf)
            b = _seg_cumsum_rows(log_f, HG_SUB)
            b_end = b[HG_SUB - 1:HG_SUB]
            st = st_ref[h]
            o = _dot_nt(q * jnp.exp(b), st)
            for s in range(HG_SUB):
                decay = jnp.exp(jnp.where(row >= s, b - b[s:s + 1], NEG_INF))
                a = jnp.sum(q * decay * key[s:s + 1], axis=-1, keepdims=True)
                o = o + a * v[s:s + 1]
            st_ref[h] = jnp.exp(b_end) * st + _dot_tn(v, key * jnp.exp(b_end - b))
            y = _rms(o, ng_ref[:, cs]) * (g * _sigmoid(g))
            y_ref[r, cs] = y.astype(y_ref.dtype)
        return carry

    lax.fori_loop(0, tb // HG_SUB, body, 0)


def hgrn2(z, lb_logits, norm_gain, *, layer, heads, tb):
    t = z.shape[0]
    w = heads * HEAD_DIM
    depth = lb_logits.shape[0]
    return pl.pallas_call(
        functools.partial(_hgrn2_kernel, layer=layer, heads=heads, tb=tb),
        grid=(t // tb,),
        in_specs=[pl.BlockSpec((tb, w), functools.partial(lambda i, c: (i, c), c=c)) for c in range(4)] + [
            pl.BlockSpec((depth, w), lambda i: (0, 0)),
            pl.BlockSpec((1, w), lambda i: (0, 0)),
        ],
        out_specs=pl.BlockSpec((tb, w), lambda i: (i, 0)),
        out_shape=jax.ShapeDtypeStruct((t, w), MXU_DTYPE),
        scratch_shapes=[pltpu.VMEM((heads, HEAD_DIM, HEAD_DIM), F32)],
        compiler_params=_params("arbitrary"),
        name="hgrn2",
    )(z, z, z, z, lb_logits, norm_gain.reshape(1, w))


ML_CHUNK = 64
ML_CONV = 4
ML_TAIL = 8
ZS_ML_BLOCK = 2
ZS_ML_I = 0
ZS_ML_F = 4


def _mlstm_kernel(q_ref, k_ref, v_ref, o_ref, zs_ref, gt_ref, cw_ref, ng_ref, y_ref,
                  conv_scr, c_scr, n_scr, m_scr, *, heads, tb):
    d = HEAD_DIM
    w = heads * d

    @pl.when(pl.program_id(0) == 0)
    def _():
        conv_scr[0:ML_TAIL, :] = jnp.zeros((ML_TAIL, 2 * w), F32)
        c_scr[...] = jnp.zeros_like(c_scr)
        n_scr[...] = jnp.zeros_like(n_scr)
        m_scr[...] = jnp.zeros_like(m_scr)

    conv_scr[ML_TAIL:ML_TAIL + tb, 0:w] = q_ref[...]
    conv_scr[ML_TAIL:ML_TAIL + tb, w:2 * w] = k_ref[...]
    acc = jnp.zeros((tb, 2 * w), F32)
    for j in range(ML_CONV):
        off = ML_TAIL - (ML_CONV - 1) + j
        acc = acc + conv_scr[off:off + tb, :] * cw_ref[j:j + 1, :]
    conv_scr[0:ML_TAIL, :] = conv_scr[tb:tb + ML_TAIL, :]
    qk = acc * _sigmoid(acc)

    zs = zs_ref[...]
    b_col_all = _seg_cumsum_rows(_log_sigmoid(zs), ML_CHUNK)
    gt = gt_ref[...]
    b_row_all = _seg_cumsum_lanes(_log_sigmoid(gt), ML_CHUNK)
    tri = (lax.broadcasted_iota(jnp.int32, (ML_CHUNK, ML_CHUNK), 0)
           >= lax.broadcasted_iota(jnp.int32, (ML_CHUNK, ML_CHUNK), 1))

    for c in range(tb // ML_CHUNK):
        rs = slice(c * ML_CHUNK, (c + 1) * ML_CHUNK)
        for h in range(heads):
            cs = slice(h * d, (h + 1) * d)
            qc = qk[rs, cs] * (d ** -0.5)
            kc = qk[rs, w + h * d:w + (h + 1) * d]
            vc = v_ref[rs, cs]
            b_col = b_col_all[rs, ZS_ML_F + h:ZS_ML_F + h + 1]
            b_row = b_row_all[heads + h:heads + h + 1, rs]
            li_row = gt[h:h + 1, rs]
            c_mem, n_mem, m_prev = c_scr[h], n_scr[h], m_scr[h]

            d_log = jnp.where(tri, b_col - b_row + li_row, NEG_INF)
            inter_log = b_col + m_prev
            m_t = jnp.maximum(inter_log, jnp.max(d_log, axis=-1, keepdims=True))
            w_intra = jnp.exp(d_log - m_t)
            w_inter = jnp.exp(inter_log - m_t)
            s = _dot_nt(qc, kc) * w_intra
            num = w_inter * _dot(qc, c_mem) + _dot(s, vc)
            qn = w_inter * jnp.sum(qc * n_mem, axis=-1, keepdims=True) + jnp.sum(s, axis=-1, keepdims=True)
            hcell = num / jnp.maximum(jnp.abs(qn), jnp.exp(-m_t))

            b_end = b_col[ML_CHUNK - 1:ML_CHUNK]
            w_state_col = b_end - b_col + zs[rs, ZS_ML_I + h:ZS_ML_I + h + 1]
            m_new = jnp.maximum(b_end + m_prev, jnp.max(w_state_col, axis=0, keepdims=True))
            carry_decay = jnp.exp(b_end + m_prev - m_new)
            k_w = kc * jnp.exp(w_state_col - m_new)
            c_scr[h] = carry_decay * c_mem + _dot_tn(k_w, vc)
            n_scr[h] = carry_decay * n_mem + jnp.sum(k_w, axis=0, keepdims=True)
            m_scr[h] = m_new

            y = _rms(hcell, ng_ref[:, cs]) * _sigmoid(o_ref[rs, cs])
            y_ref[rs, cs] = y.astype(y_ref.dtype)


def mlstm(z, zs, gt, conv_w, norm_gain, *, heads, tb, col0):
    t = z.shape[0]
    w = heads * HEAD_DIM
    cb = col0 // w
    return pl.pallas_call(
        functools.partial(_mlstm_kernel, heads=heads, tb=tb),
        grid=(t // tb,),
        in_specs=[pl.BlockSpec((tb, w), functools.partial(lambda i, c: (i, c), c=cb + c)) for c in range(4)] + [
            pl.BlockSpec((tb, 128), lambda i: (i, ZS_ML_BLOCK)),
            pl.BlockSpec((8, tb), lambda i: (0, i)),
            pl.BlockSpec((ML_CONV, 2 * w), lambda i: (0, 0)),
            pl.BlockSpec((1, w), lambda i: (0, 0)),
        ],
        out_specs=pl.BlockSpec((tb, w), lambda i: (i, 0)),
        out_shape=jax.ShapeDtypeStruct((t, w), MXU_DTYPE),
        scratch_shapes=[
            pltpu.VMEM((tb + ML_TAIL, 2 * w), F32),
            pltpu.VMEM((heads, HEAD_DIM, HEAD_DIM), F32),
            pltpu.VMEM((heads, 1, HEAD_DIM), F32),
            pltpu.VMEM((heads, 1, 1), F32),
        ],
        compiler_params=_params("arbitrary"),
        name="mlstm",
    )(z, z, z, z, zs, gt, conv_w, norm_gain.reshape(1, w))


NSA_GROUP = 4
CMP_LEN = 32
CMP_STRIDE = 16
SEL_BLOCK = 64
N_SELECT = 16
WINDOW = 512
FORCE_BONUS = 1e4
SEL_LANES = 128
LOG2_E = 1.4426950408889634


def _rope(x, cos_t, sin_t):
    return x * cos_t + pltpu.roll(x, HEAD_DIM // 2, axis=1) * sin_t


def _nsa_prep_kernel(q_ref, ks_ref, vs_ref, kw_ref, vw_ref, cos_ref, sin_ref, et_ref, qg_ref, kg_ref,
                     qh_ref, ksa_ref, vso_ref, kwr_ref, vwo_ref, *, heads, kv_heads):
    d = HEAD_DIM
    cos_t, sin_t = cos_ref[...], sin_ref[...]
    for h in range(heads):
        cs = slice(h * d, (h + 1) * d)
        q = _rope(_rms(q_ref[:, cs], qg_ref[...]), cos_t, sin_t) * (d ** -0.5 * LOG2_E)
        qh_ref[cs, :] = q.T.astype(qh_ref.dtype)
    for g in range(kv_heads):
        cs = slice(g * d, (g + 1) * d)
        ksa_ref[g, :, 0:d] = _rope(_rms(ks_ref[:, cs], kg_ref[1:2]), cos_t, sin_t).astype(ksa_ref.dtype)
        ksa_ref[g, :, d:2 * d] = et_ref[...]
        kwr_ref[g] = _rope(_rms(kw_ref[:, cs], kg_ref[2:3]), cos_t, sin_t).astype(kwr_ref.dtype)
        vso_ref[g] = vs_ref[:, cs].T.astype(vso_ref.dtype)
        vwo_ref[g] = vw_ref[:, cs].T.astype(vwo_ref.dtype)


def nsa_prep(z, cos_t, sin_t, et, q_gain, k_gains, *, heads, kv_heads, tq, col_q):
    t = z.shape[0]
    d = HEAD_DIM
    kvw = kv_heads * d
    qb = col_q // (heads * d)
    kb = (col_q + heads * d) // kvw

    def zcol(width, c):
        return pl.BlockSpec((tq, width), functools.partial(lambda i, c: (i, c), c=c))

    k_out = pl.BlockSpec((kv_heads, tq, d), lambda i: (0, i, 0))
    k_shape = jax.ShapeDtypeStruct((kv_heads, t, d), MXU_DTYPE)
    vt_out = pl.BlockSpec((kv_heads, d, tq), lambda i: (0, 0, i))
    vt_shape = jax.ShapeDtypeStruct((kv_heads, d, t), MXU_DTYPE)
    return pl.pallas_call(
        functools.partial(_nsa_prep_kernel, heads=heads, kv_heads=kv_heads),
        grid=(t // tq,),
        in_specs=[zcol(heads * d, qb), zcol(kvw, kb + 2), zcol(kvw, kb + 3), zcol(kvw, kb + 4), zcol(kvw, kb + 5),
                  pl.BlockSpec((tq, d), lambda i: (i, 0)), pl.BlockSpec((tq, d), lambda i: (i, 0)),
                  pl.BlockSpec((tq, SEL_LANES), lambda i: (i, 0)),
                  pl.BlockSpec((1, d), lambda i: (0, 0)), pl.BlockSpec((3, d), lambda i: (0, 0))],
        out_specs=[pl.BlockSpec((heads * d, tq), lambda i: (0, i)),
                   pl.BlockSpec((kv_heads, tq, 2 * d), lambda i: (0, i, 0)), vt_out, k_out, vt_out],
        out_shape=[jax.ShapeDtypeStruct((heads * d, t), MXU_DTYPE),
                   jax.ShapeDtypeStruct((kv_heads, t, 2 * d), MXU_DTYPE), vt_shape, k_shape, vt_shape],
        compiler_params=_params("parallel"),
        name="nsa_prep",
    )(z, z, z, z, z, cos_t, sin_t, et, q_gain.reshape(1, d), k_gains)


def _nsa_compress_kernel(zk_ref, zv_ref, pe_ref, w_ref, kg_ref, cos_ref, sin_ref, kc_ref, vc_ref, *, ncp):
    d = HEAD_DIM
    half = CMP_LEN // 2
    row = lax.broadcasted_iota(jnp.int32, (ncp, d), 0)

    def compress(z_ref, which):
        lo = jnp.zeros((ncp, d), F32)
        hi = jnp.zeros((ncp, d), F32)
        for l in range(half):
            xl = z_ref[pl.ds(l, ncp, stride=CMP_STRIDE), :]
            lo = lo + _dot(xl + pe_ref[which, l:l + 1, :], w_ref[which, l])
            hi = hi + _dot(xl + pe_ref[which, half + l:half + l + 1, :], w_ref[which, half + l])
        out = lo + pltpu.roll(hi, ncp - 1, axis=0)
        return jnp.where(row < ncp - 1, out, 0.0)

    kc_ref[0] = _rope(_rms(compress(zk_ref, 0), kg_ref[0:1]), cos_ref[...], sin_ref[...]).astype(kc_ref.dtype)
    vc_ref[0] = compress(zv_ref, 1).T.astype(vc_ref.dtype)


def nsa_compress(z, cmp_pos, cmp_w, k_gains, cos_c, sin_c, *, kv_heads, col_kc):
    t = z.shape[0]
    d = HEAD_DIM
    ncp = t // CMP_STRIDE
    kb = col_kc // d
    out_spec = pl.BlockSpec((1, ncp, d), lambda g: (g, 0, 0))
    out_shape = jax.ShapeDtypeStruct((kv_heads, ncp, d), MXU_DTYPE)
    return pl.pallas_call(
        functools.partial(_nsa_compress_kernel, ncp=ncp),
        grid=(kv_heads,),
        in_specs=[pl.BlockSpec((t, d), lambda g: (0, kb + g)),
                  pl.BlockSpec((t, d), lambda g: (0, kb + kv_heads + g)),
                  pl.BlockSpec((2, CMP_LEN, d), lambda g: (0, 0, 0)),
                  pl.BlockSpec((2, CMP_LEN, d, d), lambda g: (0, 0, 0, 0)),
                  pl.BlockSpec((3, d), lambda g: (0, 0)),
                  pl.BlockSpec((ncp, d), lambda g: (0, 0)), pl.BlockSpec((ncp, d), lambda g: (0, 0))],
        out_specs=[out_spec, pl.BlockSpec((1, d, ncp), lambda g: (g, 0, 0))],
        out_shape=[out_shape, jax.ShapeDtypeStruct((kv_heads, d, ncp), MXU_DTYPE)],
        compiler_params=_params("parallel"),
        name="nsa_compress",
    )(z, z, cmp_pos, cmp_w, k_gains, cos_c, sin_c)


def _nsa_cmp_kernel(qt_ref, kc_ref, vct_ref, ovt_ref, ocmp_ref, mb_ref, *, tq, grp, ncp):
    d = HEAD_DIM
    qi = pl.program_id(1)
    cmp_end = lax.broadcasted_iota(jnp.int32, (ncp, tq), 0) * CMP_STRIDE + (CMP_LEN - 1)
    vis = cmp_end <= qi * tq + lax.broadcasted_iota(jnp.int32, (ncp, tq), 1)
    psum = None

    def scores(j):
        return jnp.dot(kc_ref[0], qt_ref[j * d:(j + 1) * d, :], preferred_element_type=F32)

    s_next = scores(0)
    for j in range(grp):
        s = s_next
        if j + 1 < grp:
            s_next = scores(j + 1)
        sm = jnp.where(vis, s, NEG_INF)
        p = jnp.where(vis, jnp.exp2(sm - jnp.max(sm, axis=0, keepdims=True)), 0.0)
        l = jnp.sum(p, axis=0, keepdims=True)
        p = p * (1.0 / jnp.where(l > 0.0, l, 1.0))
        ocmp_ref[j * d:(j + 1) * d, :] = _dot(vct_ref[0], p)
        psum = p if psum is None else psum + p
    hi = psum.astype(MXU_DTYPE)
    lo = (psum - hi.astype(F32)).astype(MXU_DTYPE)
    imp = (jnp.dot(ovt_ref[...], hi, preferred_element_type=F32)
           + jnp.dot(ovt_ref[...], lo, preferred_element_type=F32))
    blk = lax.broadcasted_iota(jnp.int32, (SEL_LANES, tq), 0)
    cur = (qi * tq + lax.broadcasted_iota(jnp.int32, (SEL_LANES, tq), 1)) // SEL_BLOCK
    forced = (blk == 0) | (blk == cur) | (blk == cur - 1)
    score = jnp.where(blk <= cur, imp + jnp.where(forced, FORCE_BONUS, 0.0), NEG_INF)
    blk_f = blk.astype(F32)
    bias = jnp.full((SEL_LANES, tq), NEG_INF, F32)
    for _ in range(N_SELECT):
        mx = jnp.max(score, axis=0, keepdims=True)
        first = jnp.min(jnp.where(score == mx, blk_f, float(SEL_LANES)), axis=0, keepdims=True)
        pick = blk_f == first
        bias = jnp.where(pick, jnp.where(mx > 0.5 * NEG_INF, 0.0, NEG_INF), bias)
        score = jnp.where(pick, -jnp.inf, score)
    mb_ref[0] = bias.astype(mb_ref.dtype)


def nsa_cmp_select(qt, kc, vct, ovt, *, kv_heads, tq):
    t = qt.shape[1]
    d = HEAD_DIM
    ncp = kc.shape[1]
    gw = NSA_GROUP * d
    return pl.pallas_call(
        functools.partial(_nsa_cmp_kernel, tq=tq, grp=NSA_GROUP, ncp=ncp),
        grid=(kv_heads, t // tq),
        in_specs=[pl.BlockSpec((gw, tq), lambda g, i: (g, i)),
                  pl.BlockSpec((1, ncp, d), lambda g, i: (g, 0, 0)),
                  pl.BlockSpec((1, d, ncp), lambda g, i: (g, 0, 0)),
                  pl.BlockSpec((SEL_LANES, ncp), lambda g, i: (0, 0))],
        out_specs=[pl.BlockSpec((gw, tq), lambda g, i: (g, i)),
                   pl.BlockSpec((1, SEL_LANES, tq), lambda g, i: (g, 0, i))],
        out_shape=[jax.ShapeDtypeStruct((kv_heads * gw, t), F32),
                   jax.ShapeDtypeStruct((kv_heads, SEL_LANES, t), MXU_DTYPE)],
        compiler_params=_params("parallel", "parallel"),
        name="nsa_cmp_select",
    )(qt, kc, vct, ovt)


def _nsa_sel_kernel(qi_tab, kj_tab, qt_ref, mb_ref, ka_ref, vt_ref, o_ref, qa_scr, m_scr, l_scr, acc_scr,
                    *, tq, tk, grp):
    d = HEAD_DIM
    step = pl.program_id(1)
    qi, kj = qi_tab[step], kj_tab[step]
    last = (qi * tq + tq - 1) // tk

    @pl.when(kj == 0)
    def _():
        for j in range(grp):
            qa_scr[j, 0:d, :] = qt_ref[j * d:(j + 1) * d, :]
            qa_scr[j, d:2 * d, :] = mb_ref[0]
        m_scr[...] = jnp.full_like(m_scr, NEG_INF)
        l_scr[...] = jnp.zeros_like(l_scr)
        acc_scr[...] = jnp.zeros_like(acc_scr)

    def update(diagonal):
        if diagonal:
            causal = (kj * tk + lax.broadcasted_iota(jnp.int32, (tk, tq), 0)
                      <= qi * tq + lax.broadcasted_iota(jnp.int32, (tk, tq), 1))
        def scores(j):
            return jnp.dot(ka_ref[0], qa_scr[j], preferred_element_type=F32)

        s_next = scores(0)
        for j in range(grp):
            s = s_next
            if j + 1 < grp:
                s_next = scores(j + 1)
            if diagonal:
                s = jnp.where(causal, s, NEG_INF)
            m_prev = m_scr[j]
            m_new = jnp.maximum(m_prev, jnp.max(s, axis=0, keepdims=True))
            alpha = jnp.exp2(m_prev - m_new)
            p = jnp.exp2(s - m_new)
            l_scr[j] = alpha * l_scr[j] + jnp.sum(p, axis=0, keepdims=True)
            acc_scr[j] = alpha * acc_scr[j] + _dot(vt_ref[0], p)
            m_scr[j] = m_new

    @pl.when(kj < last)
    def _():
        update(False)

    @pl.when(kj == last)
    def _():
        update(True)
        for j in range(grp):
            o_ref[j * d:(j + 1) * d, :] = acc_scr[j] * (1.0 / l_scr[j])


def nsa_sel_attn(qt, mb, ks_aug, vst, *, kv_heads, tq, tk):
    t = qt.shape[1]
    d = HEAD_DIM
    gw = NSA_GROUP * d
    pairs = [(qi, kj) for qi in range(t // tq) for kj in range((qi * tq + tq - 1) // tk + 1)]
    qi_tab = jnp.asarray(np.array([p[0] for p in pairs], np.int32))
    kj_tab = jnp.asarray(np.array([p[1] for p in pairs], np.int32))
    grid_spec = pltpu.PrefetchScalarGridSpec(
        num_scalar_prefetch=2,
        grid=(kv_heads, len(pairs)),
        in_specs=[pl.BlockSpec((gw, tq), lambda g, s, qt, kt: (g, qt[s])),
                  pl.BlockSpec((1, SEL_LANES, tq), lambda g, s, qt, kt: (g, 0, qt[s])),
                  pl.BlockSpec((1, tk, 2 * d), lambda g, s, qt, kt: (g, kt[s], 0)),
                  pl.BlockSpec((1, d, tk), lambda g, s, qt, kt: (g, 0, kt[s]))],
        out_specs=pl.BlockSpec((gw, tq), lambda g, s, qt, kt: (g, qt[s])),
        scratch_shapes=[pltpu.VMEM((NSA_GROUP, 2 * d, tq), MXU_DTYPE), pltpu.VMEM((NSA_GROUP, 1, tq), F32),
                        pltpu.VMEM((NSA_GROUP, 1, tq), F32), pltpu.VMEM((NSA_GROUP, d, tq), F32)],
    )
    return pl.pallas_call(
        functools.partial(_nsa_sel_kernel, tq=tq, tk=tk, grp=NSA_GROUP),
        grid_spec=grid_spec,
        out_shape=jax.ShapeDtypeStruct((kv_heads * gw, t), F32),
        compiler_params=_params("parallel", "arbitrary"),
        name="nsa_sel_attn",
    )(qi_tab, kj_tab, qt, mb, ks_aug, vst)


def _nsa_win_kernel(qt_ref, k0_ref, k1_ref, k2_ref, v0_ref, v1_ref, v2_ref, ocmp_ref, osel_ref, zg_ref, y_ref,
                    *, tq, grp):
    d = HEAD_DIM
    qi = pl.program_id(1)
    k = jnp.concatenate([k0_ref[0], k1_ref[0], k2_ref[0]], axis=0)
    vt = jnp.concatenate([v0_ref[0], v1_ref[0], v2_ref[0]], axis=1)
    kpos = (qi - 2) * tq + lax.broadcasted_iota(jnp.int32, (3 * tq, tq), 0)
    rel = qi * tq + lax.broadcasted_iota(jnp.int32, (3 * tq, tq), 1) - kpos
    ok = (rel >= 0) & (rel < WINDOW) & (kpos >= 0)
    gates_t = _sigmoid(zg_ref[...]).T

    def scores(j):
        return jnp.dot(k, qt_ref[j * d:(j + 1) * d, :], preferred_element_type=F32)

    s_next = scores(0)
    for j in range(grp):
        rs = slice(j * d, (j + 1) * d)
        s = s_next
        if j + 1 < grp:
            s_next = scores(j + 1)
        sm = jnp.where(ok, s, NEG_INF)
        p = jnp.where(ok, jnp.exp2(sm - jnp.max(sm, axis=0, keepdims=True)), 0.0)
        o = _dot(vt, p) * (1.0 / jnp.sum(p, axis=0, keepdims=True))
        y = (gates_t[3 * j:3 * j + 1] * ocmp_ref[rs, :] + gates_t[3 * j + 1:3 * j + 2] * osel_ref[rs, :]
             + gates_t[3 * j + 2:3 * j + 3] * o)
        y_ref[:, rs] = y.T.astype(y_ref.dtype)


def nsa_win_combine(qt, kw, vwt, o_cmp, o_sel, zs, *, kv_heads, tq):
    t = qt.shape[1]
    d = HEAD_DIM
    gw = NSA_GROUP * d
    k_specs = [pl.BlockSpec((1, tq, d), functools.partial(lambda g, i, b: (g, jnp.maximum(i - b, 0), 0), b=b))
               for b in (2, 1, 0)]
    v_specs = [pl.BlockSpec((1, d, tq), functools.partial(lambda g, i, b: (g, 0, jnp.maximum(i - b, 0)), b=b))
               for b in (2, 1, 0)]
    tspec = pl.BlockSpec((gw, tq), lambda g, i: (g, i))
    return pl.pallas_call(
        functools.partial(_nsa_win_kernel, tq=tq, grp=NSA_GROUP),
        grid=(kv_heads, t // tq),
        in_specs=[tspec] + k_specs + v_specs + [tspec, tspec, pl.BlockSpec((tq, 128), lambda g, i: (i, g))],
        out_specs=pl.BlockSpec((tq, gw), lambda g, i: (i, g)),
        out_shape=jax.ShapeDtypeStruct((t, kv_heads * gw), MXU_DTYPE),
        compiler_params=_params("parallel", "parallel"),
        name="nsa_win_combine",
    )(qt, kw, kw, kw, vwt, vwt, vwt, o_cmp, o_sel, zs)


def _split_in_proj(w_in, b_in, sizes):
    starts = np.concatenate([[0], np.cumsum(sizes)])
    i_gate, i_mli, i_mlf = 11, 16, 17
    g0, g1 = starts[i_gate], starts[i_gate + 1]
    half = (g1 - g0) // 2

    def pad(a, width):
        return jnp.pad(a, [(0, 0)] * (a.ndim - 1) + [(0, width - a.shape[-1])])

    def arrange(a):
        main = jnp.concatenate([a[..., :g0], a[..., g1:starts[i_mli]]], axis=-1)
        small = jnp.concatenate([pad(a[..., g0:g0 + half], 128), pad(a[..., g0 + half:g1], 128),
                                 pad(a[..., starts[i_mli]:starts[i_mlf + 1]], 128)], axis=-1)
        return main, small

    return arrange(w_in), arrange(b_in)


def kernel(x, mem, norm_mix, w_in, b_in, hgrn_lb_logits, hgrn_norm, nsa_q_norm, nsa_k_norm, nsa_cmp_pos, nsa_cmp_w, mlstm_conv, mlstm_norm, w_out, norm_xattn, norm_mem, xa_wq, xa_wk, xa_wv, xa_wo, xa_q_norm, xa_k_norm, norm_mlp, mlp_w1, mlp_w2):
    _, t, d_model = x.shape
    depth = w_in.shape[0]
    d = HEAD_DIM
    hg_heads = ml_heads = d_model // (4 * d)
    nsa_heads = d_model // (2 * d)
    kv_heads = nsa_heads // NSA_GROUP
    hw, nw, kvw = hg_heads * d, nsa_heads * d, kv_heads * d
    sizes = (hw,) * 4 + (nw,) + (kvw,) * 6 + (3 * nsa_heads,) + (hw,) * 4 + (ml_heads, ml_heads)
    col_nsa_q = 4 * hw
    col_nsa_kc = col_nsa_q + nw
    col_ml = col_nsa_kc + 6 * kvw
    xa_heads = 4
    bf = MXU_DTYPE

    half = d // 2
    inv_freq = ROPE_THETA ** (-jnp.arange(half, dtype=F32) / half)

    def rope_tables(pos):
        ang = pos[:, None] * inv_freq[None, :]
        cos, sin = jnp.cos(ang), jnp.sin(ang)
        return jnp.concatenate([cos, cos], axis=-1), jnp.concatenate([-sin, sin], axis=-1)

    cos_t, sin_t = rope_tables(jnp.arange(t, dtype=F32))
    ncp = t // CMP_STRIDE
    n_cmp, n_sel = ncp - 1, t // SEL_BLOCK
    cos_c, sin_c = rope_tables(jnp.arange(ncp, dtype=F32) * CMP_STRIDE + (CMP_LEN - 1))
    et = jnp.asarray(np.arange(t)[:, None] // SEL_BLOCK == np.arange(SEL_LANES)[None, :], bf)
    c_start = np.arange(ncp)[:, None] * CMP_STRIDE
    s_start = np.arange(SEL_LANES)[None, :] * SEL_BLOCK
    overlap = ((c_start < s_start + SEL_BLOCK) & (c_start + CMP_LEN > s_start)
               & (np.arange(ncp)[:, None] < n_cmp) & (np.arange(SEL_LANES)[None, :] < n_sel))
    ovt = jnp.asarray(overlap.T, bf)

    h = x.reshape(t, d_model)
    mem2 = mem.reshape(mem.shape[1], d_model)
    for layer in range(depth):
        (w_main, w_small), (b_main, b_small) = _split_in_proj(w_in[layer], b_in[layer], sizes)
        z = norm_matmul(h, norm_mix[layer], w_main.astype(bf), b_main, tm=512, tn=w_main.shape[1] // 4)
        zs = norm_matmul(h, norm_mix[layer], w_small.astype(bf), b_small, tm=512, tn=w_small.shape[1])
        gt = zs[:, 2 * 128:2 * 128 + 8].T

        y_hg = hgrn2(z, hgrn_lb_logits, hgrn_norm[layer], layer=layer, heads=hg_heads, tb=512)
        y_ml = mlstm(z, zs, gt, mlstm_conv[layer], mlstm_norm[layer], heads=ml_heads, tb=512, col0=col_ml)

        qt, ks_aug, vst, kw, vwt = nsa_prep(z, cos_t, sin_t, et, nsa_q_norm[layer], nsa_k_norm[layer],
                                            heads=nsa_heads, kv_heads=kv_heads, tq=512, col_q=col_nsa_q)
        kc, vct = nsa_compress(z, nsa_cmp_pos[layer], nsa_cmp_w[layer].astype(bf), nsa_k_norm[layer],
                               cos_c, sin_c, kv_heads=kv_heads, col_kc=col_nsa_kc)
        o_cmp, mb = nsa_cmp_select(qt, kc, vct, ovt, kv_heads=kv_heads, tq=256)
        o_sel = nsa_sel_attn(qt, mb, ks_aug, vst, kv_heads=kv_heads, tq=256, tk=512)
        y_ns = nsa_win_combine(qt, kw, vwt, o_cmp, o_sel, zs, kv_heads=kv_heads, tq=256)

        kb = hw
        terms = [(y_hg, 0, 0, kb)] + [(y_ns, c, 1 + c, kb) for c in range(nw // kb)] + [(y_ml, 0, 1 + nw // kb, kb)]
        h = matmul_residual(terms, w_out[layer].astype(bf), h, tm=512)

        k_mem, v_mem = xa_kv(mem2, norm_mem[layer], xa_wk[layer].astype(bf), xa_wv[layer].astype(bf),
                             xa_k_norm[layer], heads=xa_heads)
        q_x = norm_matmul(h, norm_xattn[layer], xa_wq[layer].astype(bf), jnp.zeros((d_model,), F32),
                          tm=512, tn=d_model)
        o_x = xa_attn(q_x, xa_q_norm[layer], k_mem, v_mem, heads=xa_heads, tm=512)
        h = matmul_residual([(o_x, c, c, kb) for c in range(d_model // kb)], xa_wo[layer].astype(bf), h, tm=512)

        h = mlp(h, norm_mlp[layer], mlp_w1[layer].astype(bf), mlp_w2[layer].astype(bf), tm=512, tf=1024)
    return h.reshape(x.shape)
```

```python
import functools

import jax
import jax.numpy as jnp
import numpy as np
from jax import lax
from jax.experimental import pallas as pl
from jax.experimental.pallas import tpu as pltpu

F32 = jnp.float32
MXU_DTYPE = jnp.bfloat16

EPS = 1e-6
NEG_INF = -1e30
HEAD_DIM = 128
ROPE_THETA = 10000.0

V7X_VMEM_LIMIT_BYTES = 56 * 1024 * 1024


def _params(*semantics):
    return pltpu.CompilerParams(dimension_semantics=semantics, vmem_limit_bytes=V7X_VMEM_LIMIT_BYTES)


def _rms(x, gain):
    return x * lax.rsqrt(jnp.mean(x * x, axis=-1, keepdims=True) + EPS) * gain


def _dot(a, b):
    return jnp.dot(a.astype(MXU_DTYPE), b.astype(MXU_DTYPE), preferred_element_type=F32)


def _dot_nt(a, b):
    return lax.dot_general(a.astype(MXU_DTYPE), b.astype(MXU_DTYPE), (((1,), (1,)), ((), ())),
                           preferred_element_type=F32)


def _dot_tn(a, b):
    return lax.dot_general(a.astype(MXU_DTYPE), b.astype(MXU_DTYPE), (((0,), (0,)), ((), ())),
                           preferred_element_type=F32)


def _norm_matmul_kernel(x_ref, g_ref, w_ref, b_ref, o_ref):
    h = _rms(x_ref[...], g_ref[...]).astype(MXU_DTYPE)
    o_ref[...] = (jnp.dot(h, w_ref[...], preferred_element_type=F32) + b_ref[...]).astype(o_ref.dtype)


def norm_matmul(x, gain, w, bias, *, tm, tn, out_dtype=F32):
    m, d = x.shape
    n = w.shape[1]
    return pl.pallas_call(
        _norm_matmul_kernel,
        grid=(n // tn, m // tm),
        in_specs=[
            pl.BlockSpec((tm, d), lambda j, i: (i, 0)),
            pl.BlockSpec((1, d), lambda j, i: (0, 0)),
            pl.BlockSpec((d, tn), lambda j, i: (0, j)),
            pl.BlockSpec((1, tn), lambda j, i: (0, j)),
        ],
        out_specs=pl.BlockSpec((tm, tn), lambda j, i: (i, j)),
        out_shape=jax.ShapeDtypeStruct((m, n), out_dtype),
        compiler_params=_params("parallel", "parallel"),
        name="norm_matmul",
    )(x, gain.reshape(1, d), w, bias.reshape(1, n))


def _matmul_res_kernel(*refs, n_terms):
    ys, ws = refs[:n_terms], refs[n_terms:2 * n_terms]
    res_ref, o_ref = refs[2 * n_terms], refs[2 * n_terms + 1]
    acc = res_ref[...]
    for y_ref, w_ref in zip(ys, ws):
        acc = acc + jnp.dot(y_ref[...].astype(MXU_DTYPE), w_ref[...], preferred_element_type=F32)
    o_ref[...] = acc


def matmul_residual(terms, w, res, *, tm):
    m, n = res.shape
    kb = terms[0][3]
    y_specs = [pl.BlockSpec((tm, kb), functools.partial(lambda i, c: (i, c), c=cb)) for _, cb, _, _ in terms]
    w_specs = [pl.BlockSpec((kb, n), functools.partial(lambda i, r: (r, 0), r=rb)) for _, _, rb, _ in terms]
    return pl.pallas_call(
        functools.partial(_matmul_res_kernel, n_terms=len(terms)),
        grid=(m // tm,),
        in_specs=y_specs + w_specs + [pl.BlockSpec((tm, n), lambda i: (i, 0))],
        out_specs=pl.BlockSpec((tm, n), lambda i: (i, 0)),
        out_shape=jax.ShapeDtypeStruct((m, n), F32),
        compiler_params=_params("parallel"),
        name="matmul_residual",
    )(*[t[0] for t in terms], *([w] * len(terms)), res)


def _mlp_kernel(x_ref, g_ref, w1_ref, w2_ref, o_ref, h_scr):
    @pl.when(pl.program_id(1) == 0)
    def _():
        x = x_ref[...]
        h_scr[...] = _rms(x, g_ref[...]).astype(h_scr.dtype)
        o_ref[...] = x

    u = jnp.dot(h_scr[...], w1_ref[...], preferred_element_type=F32)
    a = jnp.square(jnp.maximum(u, 0.0)).astype(MXU_DTYPE)
    o_ref[...] += jnp.dot(a, w2_ref[...], preferred_element_type=F32)


def mlp(x, gain, w1, w2, *, tm, tf):
    m, d = x.shape
    ff = w1.shape[1]
    return pl.pallas_call(
        _mlp_kernel,
        grid=(m // tm, ff // tf),
        in_specs=[
            pl.BlockSpec((tm, d), lambda i, f: (i, 0)),
            pl.BlockSpec((1, d), lambda i, f: (0, 0)),
            pl.BlockSpec((d, tf), lambda i, f: (0, f)),
            pl.BlockSpec((tf, d), lambda i, f: (f, 0)),
        ],
        out_specs=pl.BlockSpec((tm, d), lambda i, f: (i, 0)),
        out_shape=jax.ShapeDtypeStruct((m, d), F32),
        scratch_shapes=[pltpu.VMEM((tm, d), MXU_DTYPE)],
        compiler_params=_params("parallel", "arbitrary"),
        name="mlp",
    )(x, gain.reshape(1, d), w1, w2)


def _xa_kv_kernel(mem_ref, g_ref, wk_ref, wv_ref, kg_ref, k_ref, v_ref):
    mn = _rms(mem_ref[...], g_ref[...]).astype(MXU_DTYPE)
    k = jnp.dot(mn, wk_ref[...], preferred_element_type=F32)
    k_ref[...] = _rms(k, kg_ref[...]).astype(k_ref.dtype)
    v_ref[...] = jnp.dot(mn, wv_ref[...], preferred_element_type=F32).astype(v_ref.dtype)


def xa_kv(mem, gain, wk, wv, k_gain, *, heads):
    ml, d = mem.shape
    hd = d // heads
    return pl.pallas_call(
        _xa_kv_kernel,
        grid=(heads,),
        in_specs=[
            pl.BlockSpec((ml, d), lambda h: (0, 0)),
            pl.BlockSpec((1, d), lambda h: (0, 0)),
            pl.BlockSpec((d, hd), lambda h: (0, h)),
            pl.BlockSpec((d, hd), lambda h: (0, h)),
            pl.BlockSpec((1, hd), lambda h: (0, 0)),
        ],
        out_specs=[pl.BlockSpec((ml, hd), lambda h: (0, h))] * 2,
        out_shape=[jax.ShapeDtypeStruct((ml, d), MXU_DTYPE)] * 2,
        compiler_params=_params("parallel"),
        name="xa_kv",
    )(mem, gain.reshape(1, d), wk, wv, k_gain.reshape(1, hd))


def _xa_attn_kernel(q_ref, qg_ref, k_ref, v_ref, o_ref, *, heads, hd):
    scale = hd ** -0.5
    for h in range(heads):
        cs = slice(h * hd, (h + 1) * hd)
        q = _rms(q_ref[:, cs], qg_ref[...]) * scale
        s = _dot_nt(q, k_ref[:, cs])
        p = jnp.exp(s - jnp.max(s, axis=-1, keepdims=True))
        p = p / jnp.sum(p, axis=-1, keepdims=True)
        o_ref[:, cs] = _dot(p, v_ref[:, cs]).astype(o_ref.dtype)


def xa_attn(q, q_gain, k, v, *, heads, tm):
    m, d = q.shape
    ml = k.shape[0]
    hd = d // heads
    return pl.pallas_call(
        functools.partial(_xa_attn_kernel, heads=heads, hd=hd),
        grid=(m // tm,),
        in_specs=[
            pl.BlockSpec((tm, d), lambda i: (i, 0)),
            pl.BlockSpec((1, hd), lambda i: (0, 0)),
            pl.BlockSpec((ml, d), lambda i: (0, 0)),
            pl.BlockSpec((ml, d), lambda i: (0, 0)),
        ],
        out_specs=pl.BlockSpec((tm, d), lambda i: (i, 0)),
        out_shape=jax.ShapeDtypeStruct((m, d), MXU_DTYPE),
        compiler_params=_params("parallel"),
        name="xa_attn",
    )(q, q_gain.reshape(1, hd), k, v)


def _seg_cumsum_rows(x, seg):
    pos = lax.broadcasted_iota(jnp.int32, x.shape, 0) % seg
    k = 1
    while k < seg:
        x = x + jnp.where(pos >= k, pltpu.roll(x, k, axis=0), 0.0)
        k *= 2
    return x


def _seg_cumsum_lanes(x, seg):
    pos = lax.broadcasted_iota(jnp.int32, x.shape, 1) % seg
    k = 1
    while k < seg:
        x = x + jnp.where(pos >= k, pltpu.roll(x, k, axis=1), 0.0)
        k *= 2
    return x


def _sigmoid(x):
    return 1.0 / (1.0 + jnp.exp(-x))


def _log_sigmoid(x):
    return jnp.minimum(x, 0.0) - jnp.log(1.0 + jnp.exp(-jnp.abs(x)))


HG_SUB = 16
HG_MIN_FORGET = 1e-20


def _hgrn2_kernel(q_ref, f_ref, i_ref, g_ref, lbl_ref, ng_ref, y_ref, st_ref, *, layer, heads, tb):
    dk = HEAD_DIM

    @pl.when(pl.program_id(0) == 0)
    def _():
        st_ref[...] = jnp.zeros_like(st_ref)

    logits = lbl_ref[...]
    e = jnp.exp(logits - jnp.max(logits, axis=0, keepdims=True))
    probs = e / jnp.sum(e, axis=0, keepdims=True)
    lb_all = jnp.zeros_like(probs[0:1])
    for l in range(1, layer + 1):
        lb_all = lb_all + probs[l:l + 1]
    row = lax.broadcasted_iota(jnp.int32, (HG_SUB, dk), 0)

    def body(c, carry):
        r = pl.ds(pl.multiple_of(c * HG_SUB, HG_SUB), HG_SUB)
        pre = []
        for h in range(heads):
            cs = slice(h * dk, (h + 1) * dk)
            lb = lb_all[:, cs]
            q, zf, v = q_ref[r, cs], f_ref[r, cs], i_ref[r, cs]
            forget = lb + (1.0 - lb) * _sigmoid(zf)
            log_f = jnp.log(jnp.maximum(forget, HG_MIN_FORGET))
            key = (1.0 - lb) * _sigmoid(-zf)
            b = _seg_cumsum_rows(log_f, HG_SUB)
            pre.append((q, v, key, b, b[HG_SUB - 1:HG_SUB]))
        inter = [_dot_nt(q * jnp.exp(b), st_ref[h]) for h, (q, v, key, b, b_end) in enumerate(pre)]
        upd = [_dot_tn(v, key * jnp.exp(b_end - b)) for q, v, key, b, b_end in pre]
        for h in range(heads):
            cs = slice(h * dk, (h + 1) * dk)
            q, v, key, b, b_end = pre[h]
            o = jnp.zeros((HG_SUB, dk), F32)
            for s in range(HG_SUB):
                decay = jnp.exp(jnp.where(row >= s, b - b[s:s + 1], NEG_INF))
                a = jnp.sum(q * decay * key[s:s + 1], axis=-1, keepdims=True)
                o = o + a * v[s:s + 1]
            o = o + inter[h]
            st_ref[h] = jnp.exp(b_end) * st_ref[h] + upd[h]
            g = g_ref[r, cs]
            y = _rms(o, ng_ref[:, cs]) * (g * _sigmoid(g))
            y_ref[r, cs] = y.astype(y_ref.dtype)
        return carry

    lax.fori_loop(0, tb // HG_SUB, body, 0)


def hgrn2(z, lb_logits, norm_gain, *, layer, heads, tb):
    t = z.shape[0]
    w = heads * HEAD_DIM
    depth = lb_logits.shape[0]
    return pl.pallas_call(
        functools.partial(_hgrn2_kernel, layer=layer, heads=heads, tb=tb),
        grid=(t // tb,),
        in_specs=[pl.BlockSpec((tb, w), functools.partial(lambda i, c: (i, c), c=c)) for c in range(4)] + [
            pl.BlockSpec((depth, w), lambda i: (0, 0)),
            pl.BlockSpec((1, w), lambda i: (0, 0)),
        ],
        out_specs=pl.BlockSpec((tb, w), lambda i: (i, 0)),
        out_shape=jax.ShapeDtypeStruct((t, w), MXU_DTYPE),
        scratch_shapes=[pltpu.VMEM((heads, HEAD_DIM, HEAD_DIM), F32)],
        compiler_params=_params("arbitrary"),
        name="hgrn2",
    )(z, z, z, z, lb_logits, norm_gain.reshape(1, w))


ML_CHUNK = 64
ML_CONV = 4
ML_TAIL = 8
ZS_ML_I = 0
ZS_ML_F = 4


def _mlstm_kernel(q_ref, k_ref, v_ref, o_ref, zs_ref, gt_ref, cw_ref, ng_ref, y_ref,
                  conv_scr, c_scr, n_scr, m_scr, *, heads, tb):
    d = HEAD_DIM
    w = heads * d

    @pl.when(pl.program_id(0) == 0)
    def _():
        conv_scr[0:ML_TAIL, :] = jnp.zeros((ML_TAIL, 2 * w), F32)
        c_scr[...] = jnp.zeros_like(c_scr)
        n_scr[...] = jnp.zeros_like(n_scr)
        m_scr[...] = jnp.zeros_like(m_scr)

    conv_scr[ML_TAIL:ML_TAIL + tb, 0:w] = q_ref[...]
    conv_scr[ML_TAIL:ML_TAIL + tb, w:2 * w] = k_ref[...]
    acc = jnp.zeros((tb, 2 * w), F32)
    for j in range(ML_CONV):
        off = ML_TAIL - (ML_CONV - 1) + j
        acc = acc + conv_scr[off:off + tb, :] * cw_ref[j:j + 1, :]
    conv_scr[0:ML_TAIL, :] = conv_scr[tb:tb + ML_TAIL, :]
    qk = acc * _sigmoid(acc)

    zs = zs_ref[...]
    b_col_all = _seg_cumsum_rows(_log_sigmoid(zs), ML_CHUNK)
    gt = gt_ref[...]
    b_row_all = _seg_cumsum_lanes(_log_sigmoid(gt), ML_CHUNK)
    tri = (lax.broadcasted_iota(jnp.int32, (ML_CHUNK, ML_CHUNK), 0)
           >= lax.broadcasted_iota(jnp.int32, (ML_CHUNK, ML_CHUNK), 1))

    def rows(c):
        return slice(c * ML_CHUNK, (c + 1) * ML_CHUNK)

    def qk_scores(c):
        return [_dot_nt(qk[rows(c), h * d:(h + 1) * d] * (d ** -0.5), qk[rows(c), w + h * d:w + (h + 1) * d])
                for h in range(heads)]

    def state_free(c, s_raw):
        rs = rows(c)
        out = []
        for h in range(heads):
            b_col = b_col_all[rs, ZS_ML_F + h:ZS_ML_F + h + 1]
            b_row = b_row_all[heads + h:heads + h + 1, rs]
            li_row = gt[h:h + 1, rs]
            d_log = jnp.where(tri, b_col - b_row + li_row, NEG_INF)
            dmax = jnp.max(d_log, axis=-1, keepdims=True)
            s = s_raw[h] * jnp.exp(d_log - dmax)
            b_end = b_col[ML_CHUNK - 1:ML_CHUNK]
            w_state = b_end - b_col + zs[rs, ZS_ML_I + h:ZS_ML_I + h + 1]
            wmax = jnp.max(w_state, axis=0, keepdims=True)
            k_w = qk[rs, w + h * d:w + (h + 1) * d] * jnp.exp(w_state - wmax)
            out.append(dict(b_col=b_col, dmax=dmax, b_end=b_end, wmax=wmax, s=s, k_w=k_w,
                            ssum=jnp.sum(s, axis=-1, keepdims=True), ksum=jnp.sum(k_w, axis=0, keepdims=True)))
        for h in range(heads):
            vc = v_ref[rs, h * d:(h + 1) * d]
            out[h]["intra"] = _dot(out[h]["s"], vc)
            out[h]["upd"] = _dot_tn(out[h]["k_w"], vc)
        return out

    n_chunks = tb // ML_CHUNK
    scores = {0: qk_scores(0)}
    if n_chunks > 1:
        scores[1] = qk_scores(1)
    free = {0: state_free(0, scores.pop(0))}
    for c in range(n_chunks):
        rs = rows(c)
        if c + 2 < n_chunks:
            scores[c + 2] = qk_scores(c + 2)
        inter = [_dot(qk[rs, h * d:(h + 1) * d] * (d ** -0.5), c_scr[h]) for h in range(heads)]
        if c + 1 < n_chunks:
            free[c + 1] = state_free(c + 1, scores.pop(c + 1))
        cur = free.pop(c)
        for h in range(heads):
            cs = slice(h * d, (h + 1) * d)
            f = cur[h]
            m_prev, n_mem = m_scr[h], n_scr[h]
            inter_log = f["b_col"] + m_prev
            m_t = jnp.maximum(inter_log, f["dmax"])
            r_intra = jnp.exp(f["dmax"] - m_t)
            w_inter = jnp.exp(inter_log - m_t)
            qc = qk[rs, cs] * (d ** -0.5)
            num = w_inter * inter[h] + r_intra * f["intra"]
            qn = w_inter * jnp.sum(qc * n_mem, axis=-1, keepdims=True) + r_intra * f["ssum"]
            hcell = num / jnp.maximum(jnp.abs(qn), jnp.exp(-m_t))
            m_new = jnp.maximum(f["b_end"] + m_prev, f["wmax"])
            carry_decay = jnp.exp(f["b_end"] + m_prev - m_new)
            r_state = jnp.exp(f["wmax"] - m_new)
            c_scr[h] = carry_decay * c_scr[h] + r_state * f["upd"]
            n_scr[h] = carry_decay * n_mem + r_state * f["ksum"]
            m_scr[h] = m_new
            y = _rms(hcell, ng_ref[:, cs]) * _sigmoid(o_ref[rs, cs])
            y_ref[rs, cs] = y.astype(y_ref.dtype)


def mlstm(z, zs, gt, conv_w, norm_gain, *, heads, tb, col0, zs_block):
    t = z.shape[0]
    w = heads * HEAD_DIM
    cb = col0 // w
    return pl.pallas_call(
        functools.partial(_mlstm_kernel, heads=heads, tb=tb),
        grid=(t // tb,),
        in_specs=[pl.BlockSpec((tb, w), functools.partial(lambda i, c: (i, c), c=cb + c)) for c in range(4)] + [
            pl.BlockSpec((tb, 128), lambda i: (i, zs_block)),
            pl.BlockSpec((8, tb), lambda i: (0, i)),
            pl.BlockSpec((ML_CONV, 2 * w), lambda i: (0, 0)),
            pl.BlockSpec((1, w), lambda i: (0, 0)),
        ],
        out_specs=pl.BlockSpec((tb, w), lambda i: (i, 0)),
        out_shape=jax.ShapeDtypeStruct((t, w), MXU_DTYPE),
        scratch_shapes=[
            pltpu.VMEM((tb + ML_TAIL, 2 * w), F32),
            pltpu.VMEM((heads, HEAD_DIM, HEAD_DIM), F32),
            pltpu.VMEM((heads, 1, HEAD_DIM), F32),
            pltpu.VMEM((heads, 1, 1), F32),
        ],
        compiler_params=_params("arbitrary"),
        name="mlstm",
    )(z, z, z, z, zs, gt, conv_w, norm_gain.reshape(1, w))


NSA_GROUP = 4
CMP_LEN = 32
CMP_STRIDE = 16
SEL_BLOCK = 64
N_SELECT = 16
WINDOW = 512
FORCE_BONUS = 1e4
SEL_LANES = 128
LOG2_E = 1.4426950408889634


def _rope(x, cos_t, sin_t):
    return x * cos_t + pltpu.roll(x, HEAD_DIM // 2, axis=1) * sin_t


def _nsa_prep_kernel(q_ref, ks_ref, vs_ref, kw_ref, vw_ref, cos_ref, sin_ref, et_ref, qg_ref, kg_ref,
                     qh_ref, ksa_ref, vso_ref, kwr_ref, vwo_ref, *, heads, kv_heads):
    d = HEAD_DIM
    cos_t, sin_t = cos_ref[...], sin_ref[...]
    for h in range(heads):
        cs = slice(h * d, (h + 1) * d)
        q = _rope(_rms(q_ref[:, cs], qg_ref[...]), cos_t, sin_t) * (d ** -0.5 * LOG2_E)
        qh_ref[cs, :] = q.T.astype(qh_ref.dtype)
    for g in range(kv_heads):
        cs = slice(g * d, (g + 1) * d)
        ksa_ref[g, :, 0:d] = _rope(_rms(ks_ref[:, cs], kg_ref[1:2]), cos_t, sin_t).astype(ksa_ref.dtype)
        ksa_ref[g, :, d:2 * d] = et_ref[...]
        kwr_ref[g] = _rope(_rms(kw_ref[:, cs], kg_ref[2:3]), cos_t, sin_t).astype(kwr_ref.dtype)
        vso_ref[g] = vs_ref[:, cs].T.astype(vso_ref.dtype)
        vwo_ref[g] = vw_ref[:, cs].T.astype(vwo_ref.dtype)


def nsa_prep(z, cos_t, sin_t, et, q_gain, k_gains, *, heads, kv_heads, tq, col_q):
    t = z.shape[0]
    d = HEAD_DIM
    kvw = kv_heads * d
    qb = col_q // (heads * d)
    kb = (col_q + heads * d) // kvw

    def zcol(width, c):
        return pl.BlockSpec((tq, width), functools.partial(lambda i, c: (i, c), c=c))

    k_out = pl.BlockSpec((kv_heads, tq, d), lambda i: (0, i, 0))
    k_shape = jax.ShapeDtypeStruct((kv_heads, t, d), MXU_DTYPE)
    vt_out = pl.BlockSpec((kv_heads, d, tq), lambda i: (0, 0, i))
    vt_shape = jax.ShapeDtypeStruct((kv_heads, d, t), MXU_DTYPE)
    return pl.pallas_call(
        functools.partial(_nsa_prep_kernel, heads=heads, kv_heads=kv_heads),
        grid=(t // tq,),
        in_specs=[zcol(heads * d, qb), zcol(kvw, kb + 2), zcol(kvw, kb + 3), zcol(kvw, kb + 4), zcol(kvw, kb + 5),
                  pl.BlockSpec((tq, d), lambda i: (i, 0)), pl.BlockSpec((tq, d), lambda i: (i, 0)),
                  pl.BlockSpec((tq, SEL_LANES), lambda i: (i, 0)),
                  pl.BlockSpec((1, d), lambda i: (0, 0)), pl.BlockSpec((3, d), lambda i: (0, 0))],
        out_specs=[pl.BlockSpec((heads * d, tq), lambda i: (0, i)),
                   pl.BlockSpec((kv_heads, tq, 2 * d), lambda i: (0, i, 0)), vt_out, k_out, vt_out],
        out_shape=[jax.ShapeDtypeStruct((heads * d, t), MXU_DTYPE),
                   jax.ShapeDtypeStruct((kv_heads, t, 2 * d), MXU_DTYPE), vt_shape, k_shape, vt_shape],
        compiler_params=_params("parallel"),
        name="nsa_prep",
    )(z, z, z, z, z, cos_t, sin_t, et, q_gain.reshape(1, d), k_gains)


def _nsa_compress_kernel(zk_ref, zv_ref, pe_ref, w_ref, kg_ref, cos_ref, sin_ref, kc_ref, vc_ref, *, ncp):
    d = HEAD_DIM
    half = CMP_LEN // 2
    row = lax.broadcasted_iota(jnp.int32, (ncp, d), 0)

    def compress(z_ref, which):
        lo = jnp.zeros((ncp, d), F32)
        hi = jnp.zeros((ncp, d), F32)
        for l in range(half):
            xl = z_ref[pl.ds(l, ncp, stride=CMP_STRIDE), :]
            lo = lo + _dot(xl + pe_ref[which, l:l + 1, :], w_ref[which, l])
            hi = hi + _dot(xl + pe_ref[which, half + l:half + l + 1, :], w_ref[which, half + l])
        out = lo + pltpu.roll(hi, ncp - 1, axis=0)
        return jnp.where(row < ncp - 1, out, 0.0)

    kc_ref[0] = _rope(_rms(compress(zk_ref, 0), kg_ref[0:1]), cos_ref[...], sin_ref[...]).astype(kc_ref.dtype)
    vc_ref[0] = compress(zv_ref, 1).T.astype(vc_ref.dtype)


def nsa_compress(z, cmp_pos, cmp_w, k_gains, cos_c, sin_c, *, kv_heads, col_kc):
    t = z.shape[0]
    d = HEAD_DIM
    ncp = t // CMP_STRIDE
    kb = col_kc // d
    out_spec = pl.BlockSpec((1, ncp, d), lambda g: (g, 0, 0))
    out_shape = jax.ShapeDtypeStruct((kv_heads, ncp, d), MXU_DTYPE)
    return pl.pallas_call(
        functools.partial(_nsa_compress_kernel, ncp=ncp),
        grid=(kv_heads,),
        in_specs=[pl.BlockSpec((t, d), lambda g: (0, kb + g)),
                  pl.BlockSpec((t, d), lambda g: (0, kb + kv_heads + g)),
                  pl.BlockSpec((2, CMP_LEN, d), lambda g: (0, 0, 0)),
                  pl.BlockSpec((2, CMP_LEN, d, d), lambda g: (0, 0, 0, 0)),
                  pl.BlockSpec((3, d), lambda g: (0, 0)),
                  pl.BlockSpec((ncp, d), lambda g: (0, 0)), pl.BlockSpec((ncp, d), lambda g: (0, 0))],
        out_specs=[out_spec, pl.BlockSpec((1, d, ncp), lambda g: (g, 0, 0))],
        out_shape=[out_shape, jax.ShapeDtypeStruct((kv_heads, d, ncp), MXU_DTYPE)],
        compiler_params=_params("parallel"),
        name="nsa_compress",
    )(z, z, cmp_pos, cmp_w, k_gains, cos_c, sin_c)


def _nsa_cmp_kernel(qt_ref, kc_ref, vct_ref, ovt_ref, ocmp_ref, mb_ref, *, tq, grp, ncp):
    d = HEAD_DIM
    qi = pl.program_id(1)
    cmp_end = lax.broadcasted_iota(jnp.int32, (ncp, tq), 0) * CMP_STRIDE + (CMP_LEN - 1)
    vis = cmp_end <= qi * tq + lax.broadcasted_iota(jnp.int32, (ncp, tq), 1)
    psum = None

    def scores(j):
        return jnp.dot(kc_ref[0], qt_ref[j * d:(j + 1) * d, :], preferred_element_type=F32)

    s_next = scores(0)
    for j in range(grp):
        s = s_next
        if j + 1 < grp:
            s_next = scores(j + 1)
        sm = jnp.where(vis, s, NEG_INF)
        p = jnp.where(vis, jnp.exp2(sm - jnp.max(sm, axis=0, keepdims=True)), 0.0)
        l = jnp.sum(p, axis=0, keepdims=True)
        p = p * (1.0 / jnp.where(l > 0.0, l, 1.0))
        ocmp_ref[j * d:(j + 1) * d, :] = _dot(vct_ref[0], p)
        psum = p if psum is None else psum + p
    hi = psum.astype(MXU_DTYPE)
    lo = (psum - hi.astype(F32)).astype(MXU_DTYPE)
    imp = (jnp.dot(ovt_ref[...], hi, preferred_element_type=F32)
           + jnp.dot(ovt_ref[...], lo, preferred_element_type=F32))
    blk = lax.broadcasted_iota(jnp.int32, (SEL_LANES, tq), 0)
    cur = (qi * tq + lax.broadcasted_iota(jnp.int32, (SEL_LANES, tq), 1)) // SEL_BLOCK
    forced = (blk == 0) | (blk == cur) | (blk == cur - 1)
    score = jnp.where(blk <= cur, imp + jnp.where(forced, FORCE_BONUS, 0.0), NEG_INF)
    blk_f = blk.astype(F32)
    bias = jnp.full((SEL_LANES, tq), NEG_INF, F32)
    for _ in range(N_SELECT):
        mx = jnp.max(score, axis=0, keepdims=True)
        first = jnp.min(jnp.where(score == mx, blk_f, float(SEL_LANES)), axis=0, keepdims=True)
        pick = blk_f == first
        bias = jnp.where(pick, jnp.where(mx > 0.5 * NEG_INF, 0.0, NEG_INF), bias)
        score = jnp.where(pick, -jnp.inf, score)
    mb_ref[0] = bias.astype(mb_ref.dtype)


def nsa_cmp_select(qt, kc, vct, ovt, *, kv_heads, tq):
    t = qt.shape[1]
    d = HEAD_DIM
    ncp = kc.shape[1]
    gw = NSA_GROUP * d
    return pl.pallas_call(
        functools.partial(_nsa_cmp_kernel, tq=tq, grp=NSA_GROUP, ncp=ncp),
        grid=(kv_heads, t // tq),
        in_specs=[pl.BlockSpec((gw, tq), lambda g, i: (g, i)),
                  pl.BlockSpec((1, ncp, d), lambda g, i: (g, 0, 0)),
                  pl.BlockSpec((1, d, ncp), lambda g, i: (g, 0, 0)),
                  pl.BlockSpec((SEL_LANES, ncp), lambda g, i: (0, 0))],
        out_specs=[pl.BlockSpec((gw, tq), lambda g, i: (g, i)),
                   pl.BlockSpec((1, SEL_LANES, tq), lambda g, i: (g, 0, i))],
        out_shape=[jax.ShapeDtypeStruct((kv_heads * gw, t), F32),
                   jax.ShapeDtypeStruct((kv_heads, SEL_LANES, t), MXU_DTYPE)],
        compiler_params=_params("parallel", "parallel"),
        name="nsa_cmp_select",
    )(qt, kc, vct, ovt)


def _nsa_sel_kernel(qi_tab, kj_tab, qt_ref, mb_ref, ka_ref, vt_ref, o_ref, qa_scr, m_scr, l_scr, acc_scr,
                    *, tq, tk, grp, hpu):
    d = HEAD_DIM
    step = pl.program_id(1)
    qi, kj = qi_tab[step], kj_tab[step]
    last = (qi * tq + tq - 1) // tk
    units = grp // hpu
    uw = hpu * tq

    @pl.when(kj == 0)
    def _():
        for j in range(grp):
            u, ls = j // hpu, slice((j % hpu) * tq, (j % hpu + 1) * tq)
            qa_scr[u, 0:d, ls] = qt_ref[j * d:(j + 1) * d, :]
            qa_scr[u, d:2 * d, ls] = mb_ref[0]
        m_scr[...] = jnp.full_like(m_scr, NEG_INF)
        l_scr[...] = jnp.zeros_like(l_scr)
        acc_scr[...] = jnp.zeros_like(acc_scr)

    def update(diagonal):
        if diagonal:
            causal = (kj * tk + lax.broadcasted_iota(jnp.int32, (tk, uw), 0)
                      <= qi * tq + lax.broadcasted_iota(jnp.int32, (tk, uw), 1) % tq)

        def scores(u):
            return jnp.dot(ka_ref[0], qa_scr[u], preferred_element_type=F32)

        s_next = scores(0)
        for u in range(units):
            s = s_next
            if u + 1 < units:
                s_next = scores(u + 1)
            if diagonal:
                s = jnp.where(causal, s, NEG_INF)
            m_prev = m_scr[u]
            m_new = jnp.maximum(m_prev, jnp.max(s, axis=0, keepdims=True))
            alpha = jnp.exp2(m_prev - m_new)
            p = jnp.exp2(s - m_new)
            l_scr[u] = alpha * l_scr[u] + jnp.sum(p, axis=0, keepdims=True)
            acc_scr[u] = alpha * acc_scr[u] + _dot(vt_ref[0], p)
            m_scr[u] = m_new

    @pl.when(kj < last)
    def _():
        update(False)

    @pl.when(kj == last)
    def _():
        update(True)
        for j in range(grp):
            u, ls = j // hpu, slice((j % hpu) * tq, (j % hpu + 1) * tq)
            o_ref[j * d:(j + 1) * d, :] = acc_scr[u, :, ls] * (1.0 / l_scr[u, :, ls])


def nsa_sel_attn(qt, mb, ks_aug, vst, *, kv_heads, tq, tk, hpu):
    t = qt.shape[1]
    d = HEAD_DIM
    gw = NSA_GROUP * d
    pairs = [(qi, kj) for qi in range(t // tq) for kj in range((qi * tq + tq - 1) // tk + 1)]
    qi_tab = jnp.asarray(np.array([p[0] for p in pairs], np.int32))
    kj_tab = jnp.asarray(np.array([p[1] for p in pairs], np.int32))
    units, uw = NSA_GROUP // hpu, hpu * tq
    grid_spec = pltpu.PrefetchScalarGridSpec(
        num_scalar_prefetch=2,
        grid=(kv_heads, len(pairs)),
        in_specs=[pl.BlockSpec((gw, tq), lambda g, s, qt, kt: (g, qt[s])),
                  pl.BlockSpec((1, SEL_LANES, tq), lambda g, s, qt, kt: (g, 0, qt[s])),
                  pl.BlockSpec((1, tk, 2 * d), lambda g, s, qt, kt: (g, kt[s], 0)),
                  pl.BlockSpec((1, d, tk), lambda g, s, qt, kt: (g, 0, kt[s]))],
        out_specs=pl.BlockSpec((gw, tq), lambda g, s, qt, kt: (g, qt[s])),
        scratch_shapes=[pltpu.VMEM((units, 2 * d, uw), MXU_DTYPE), pltpu.VMEM((units, 1, uw), F32),
                        pltpu.VMEM((units, 1, uw), F32), pltpu.VMEM((units, d, uw), F32)],
    )
    return pl.pallas_call(
        functools.partial(_nsa_sel_kernel, tq=tq, tk=tk, grp=NSA_GROUP, hpu=hpu),
        grid_spec=grid_spec,
        out_shape=jax.ShapeDtypeStruct((kv_heads * gw, t), F32),
        compiler_params=_params("parallel", "arbitrary"),
        name="nsa_sel_attn",
    )(qi_tab, kj_tab, qt, mb, ks_aug, vst)


def _nsa_win_kernel(qt_ref, k0_ref, k1_ref, k2_ref, v0_ref, v1_ref, v2_ref, ocmp_ref, osel_ref, zg_ref, y_ref,
                    *, tq, grp):
    d = HEAD_DIM
    qi = pl.program_id(1)
    k = jnp.concatenate([k0_ref[0], k1_ref[0], k2_ref[0]], axis=0)
    vt = jnp.concatenate([v0_ref[0], v1_ref[0], v2_ref[0]], axis=1)
    kpos = (qi - 2) * tq + lax.broadcasted_iota(jnp.int32, (3 * tq, tq), 0)
    rel = qi * tq + lax.broadcasted_iota(jnp.int32, (3 * tq, tq), 1) - kpos
    ok = (rel >= 0) & (rel < WINDOW) & (kpos >= 0)
    gates_t = _sigmoid(zg_ref[...]).T

    def scores(j):
        return jnp.dot(k, qt_ref[j * d:(j + 1) * d, :], preferred_element_type=F32)

    s_next = scores(0)
    for j in range(grp):
        rs = slice(j * d, (j + 1) * d)
        s = s_next
        if j + 1 < grp:
            s_next = scores(j + 1)
        sm = jnp.where(ok, s, NEG_INF)
        p = jnp.where(ok, jnp.exp2(sm - jnp.max(sm, axis=0, keepdims=True)), 0.0)
        o = _dot(vt, p) * (1.0 / jnp.sum(p, axis=0, keepdims=True))
        y = (gates_t[3 * j:3 * j + 1] * ocmp_ref[rs, :] + gates_t[3 * j + 1:3 * j + 2] * osel_ref[rs, :]
             + gates_t[3 * j + 2:3 * j + 3] * o)
        y_ref[:, rs] = y.T.astype(y_ref.dtype)


def nsa_win_combine(qt, kw, vwt, o_cmp, o_sel, zs, *, kv_heads, tq, zs_block0):
    t = qt.shape[1]
    d = HEAD_DIM
    gw = NSA_GROUP * d
    k_specs = [pl.BlockSpec((1, tq, d), functools.partial(lambda g, i, b: (g, jnp.maximum(i - b, 0), 0), b=b))
               for b in (2, 1, 0)]
    v_specs = [pl.BlockSpec((1, d, tq), functools.partial(lambda g, i, b: (g, 0, jnp.maximum(i - b, 0)), b=b))
               for b in (2, 1, 0)]
    tspec = pl.BlockSpec((gw, tq), lambda g, i: (g, i))
    return pl.pallas_call(
        functools.partial(_nsa_win_kernel, tq=tq, grp=NSA_GROUP),
        grid=(kv_heads, t // tq),
        in_specs=[tspec] + k_specs + v_specs + [tspec, tspec,
                                                pl.BlockSpec((tq, 128), lambda g, i: (i, zs_block0 + g))],
        out_specs=pl.BlockSpec((tq, gw), lambda g, i: (i, g)),
        out_shape=jax.ShapeDtypeStruct((t, kv_heads * gw), MXU_DTYPE),
        compiler_params=_params("parallel", "parallel"),
        name="nsa_win_combine",
    )(qt, kw, kw, kw, vwt, vwt, vwt, o_cmp, o_sel, zs)


def _split_in_proj(w_in, b_in, sizes):
    starts = np.concatenate([[0], np.cumsum(sizes)])
    i_gate, i_mli, i_mlf = 11, 16, 17
    g0, g1 = starts[i_gate], starts[i_gate + 1]
    half = (g1 - g0) // 2

    def pad(a, width):
        return jnp.pad(a, [(0, 0)] * (a.ndim - 1) + [(0, width - a.shape[-1])])

    def arrange(a):
        part_b = jnp.concatenate([a[..., g1:starts[i_mli]], pad(a[..., g0:g0 + half], 128),
                                  pad(a[..., g0 + half:g1], 128),
                                  pad(a[..., starts[i_mli]:starts[i_mlf + 1]], 128)], axis=-1)
        return a[..., :g0], part_b

    return arrange(w_in), arrange(b_in)


def kernel(x, mem, norm_mix, w_in, b_in, hgrn_lb_logits, hgrn_norm, nsa_q_norm, nsa_k_norm, nsa_cmp_pos, nsa_cmp_w, mlstm_conv, mlstm_norm, w_out, norm_xattn, norm_mem, xa_wq, xa_wk, xa_wv, xa_wo, xa_q_norm, xa_k_norm, norm_mlp, mlp_w1, mlp_w2):
    _, t, d_model = x.shape
    depth = w_in.shape[0]
    d = HEAD_DIM
    hg_heads = ml_heads = d_model // (4 * d)
    nsa_heads = d_model // (2 * d)
    kv_heads = nsa_heads // NSA_GROUP
    hw, nw, kvw = hg_heads * d, nsa_heads * d, kv_heads * d
    sizes = (hw,) * 4 + (nw,) + (kvw,) * 6 + (3 * nsa_heads,) + (hw,) * 4 + (ml_heads, ml_heads)
    col_nsa_q = 4 * hw
    col_nsa_kc = col_nsa_q + nw
    col_ml = col_nsa_kc + 6 * kvw
    xa_heads = 4
    bf = MXU_DTYPE

    half = d // 2
    inv_freq = ROPE_THETA ** (-jnp.arange(half, dtype=F32) / half)

    def rope_tables(pos):
        ang = pos[:, None] * inv_freq[None, :]
        cos, sin = jnp.cos(ang), jnp.sin(ang)
        return jnp.concatenate([cos, cos], axis=-1), jnp.concatenate([-sin, sin], axis=-1)

    cos_t, sin_t = rope_tables(jnp.arange(t, dtype=F32))
    ncp = t // CMP_STRIDE
    n_cmp, n_sel = ncp - 1, t // SEL_BLOCK
    cos_c, sin_c = rope_tables(jnp.arange(ncp, dtype=F32) * CMP_STRIDE + (CMP_LEN - 1))
    et = jnp.asarray(np.arange(t)[:, None] // SEL_BLOCK == np.arange(SEL_LANES)[None, :], bf)
    c_start = np.arange(ncp)[:, None] * CMP_STRIDE
    s_start = np.arange(SEL_LANES)[None, :] * SEL_BLOCK
    overlap = ((c_start < s_start + SEL_BLOCK) & (c_start + CMP_LEN > s_start)
               & (np.arange(ncp)[:, None] < n_cmp) & (np.arange(SEL_LANES)[None, :] < n_sel))
    ovt = jnp.asarray(overlap.T, bf)

    h = x.reshape(t, d_model)
    mem2 = mem.reshape(mem.shape[1], d_model)
    for layer in range(depth):
        (w_a, w_b), (b_a, b_b) = _split_in_proj(w_in[layer], b_in[layer], sizes)
        z = norm_matmul(h, norm_mix[layer], w_a.astype(bf), b_a, tm=512, tn=w_a.shape[1] // 3)
        zb = norm_matmul(h, norm_mix[layer], w_b.astype(bf), b_b, tm=512, tn=w_b.shape[1])
        gate_block = 4 * hw // 128
        gt = zb[:, (gate_block + 2) * 128:(gate_block + 2) * 128 + 8].T

        y_hg = hgrn2(z, hgrn_lb_logits, hgrn_norm[layer], layer=layer, heads=hg_heads, tb=512)
        y_ml = mlstm(zb, zb, gt, mlstm_conv[layer], mlstm_norm[layer], heads=ml_heads, tb=512, col0=0,
                     zs_block=gate_block + 2)

        qt, ks_aug, vst, kw, vwt = nsa_prep(z, cos_t, sin_t, et, nsa_q_norm[layer], nsa_k_norm[layer],
                                            heads=nsa_heads, kv_heads=kv_heads, tq=512, col_q=col_nsa_q)
        kc, vct = nsa_compress(z, nsa_cmp_pos[layer], nsa_cmp_w[layer].astype(bf), nsa_k_norm[layer],
                               cos_c, sin_c, kv_heads=kv_heads, col_kc=col_nsa_kc)
        o_cmp, mb = nsa_cmp_select(qt, kc, vct, ovt, kv_heads=kv_heads, tq=256)
        o_sel = nsa_sel_attn(qt, mb, ks_aug, vst, kv_heads=kv_heads, tq=512, tk=512, hpu=1)
        y_ns = nsa_win_combine(qt, kw, vwt, o_cmp, o_sel, zb, kv_heads=kv_heads, tq=256, zs_block0=gate_block)

        kb = hw
        terms = [(y_hg, 0, 0, kb)] + [(y_ns, c, 1 + c, kb) for c in range(nw // kb)] + [(y_ml, 0, 1 + nw // kb, kb)]
        h = matmul_residual(terms, w_out[layer].astype(bf), h, tm=512)

        k_mem, v_mem = xa_kv(mem2, norm_mem[layer], xa_wk[layer].astype(bf), xa_wv[layer].astype(bf),
                             xa_k_norm[layer], heads=xa_heads)
        q_x = norm_matmul(h, norm_xattn[layer], xa_wq[layer].astype(bf), jnp.zeros((d_model,), F32),
                          tm=512, tn=d_model)
        o_x = xa_attn(q_x, xa_q_norm[layer], k_mem, v_mem, heads=xa_heads, tm=512)
        h = matmul_residual([(o_x, c, c, kb) for c in range(d_model // kb)], xa_wo[layer].astype(bf), h, tm=512)

        h = mlp(h, norm_mlp[layer], mlp_w1[layer].astype(bf), mlp_w2[layer].astype(bf), tm=512, tf=1024)
    return h.reshape(x.shape)
```

```python
import functools

import jax
import jax.numpy as jnp
import numpy as np
from jax import lax
from jax.experimental import pallas as pl
from jax.experimental.pallas import tpu as pltpu

F32 = jnp.float32
MXU_DTYPE = jnp.bfloat16

EPS = 1e-6
NEG_INF = -1e30
HEAD_DIM = 128
ROPE_THETA = 10000.0

V7X_VMEM_LIMIT_BYTES = 56 * 1024 * 1024


def _params(*semantics):
    return pltpu.CompilerParams(dimension_semantics=semantics, vmem_limit_bytes=V7X_VMEM_LIMIT_BYTES)


def _rms(x, gain):
    return x * lax.rsqrt(jnp.mean(x * x, axis=-1, keepdims=True) + EPS) * gain


def _dot(a, b):
    return jnp.dot(a.astype(MXU_DTYPE), b.astype(MXU_DTYPE), preferred_element_type=F32)


def _dot_nt(a, b):
    return lax.dot_general(a.astype(MXU_DTYPE), b.astype(MXU_DTYPE), (((1,), (1,)), ((), ())),
                           preferred_element_type=F32)


def _dot_tn(a, b):
    return lax.dot_general(a.astype(MXU_DTYPE), b.astype(MXU_DTYPE), (((0,), (0,)), ((), ())),
                           preferred_element_type=F32)


def _rms_cast_kernel(x_ref, g_ref, o_ref):
    o_ref[...] = _rms(x_ref[...], g_ref[...]).astype(o_ref.dtype)


def rms_cast(x, gain, *, tm):
    m, d = x.shape
    return pl.pallas_call(
        _rms_cast_kernel,
        grid=(m // tm,),
        in_specs=[pl.BlockSpec((tm, d), lambda i: (i, 0)), pl.BlockSpec((1, d), lambda i: (0, 0))],
        out_specs=pl.BlockSpec((tm, d), lambda i: (i, 0)),
        out_shape=jax.ShapeDtypeStruct((m, d), MXU_DTYPE),
        compiler_params=_params("parallel"),
        name="rms_cast",
    )(x, gain.reshape(1, d))


def _matmul_bias_kernel(h_ref, w_ref, b_ref, o_ref):
    o_ref[...] = jnp.dot(h_ref[...], w_ref[...], preferred_element_type=F32) + b_ref[...]


def _matmul_kernel(h_ref, w_ref, o_ref):
    o_ref[...] = jnp.dot(h_ref[...], w_ref[...], preferred_element_type=F32)


def matmul_cols(h, w, bias, *, layer, n_tiles, tm, tn):
    m, d = h.shape
    in_specs = [pl.BlockSpec((tm, d), lambda j, i: (i, 0)),
                pl.BlockSpec((None, d, tn), lambda j, i: (layer, 0, j))]
    args = [h, w]
    if bias is not None:
        in_specs.append(pl.BlockSpec((None, 1, tn), lambda j, i: (layer, 0, j)))
        args.append(bias.reshape(bias.shape[0], 1, bias.shape[1]))
    return pl.pallas_call(
        _matmul_kernel if bias is None else _matmul_bias_kernel,
        grid=(n_tiles, m // tm),
        in_specs=in_specs,
        out_specs=pl.BlockSpec((tm, tn), lambda j, i: (i, j)),
        out_shape=jax.ShapeDtypeStruct((m, n_tiles * tn), F32),
        compiler_params=_params("parallel", "parallel"),
        name="matmul_cols",
    )(*args)


def _matmul_res_kernel(*refs, n_terms, with_norm):
    ys, ws = refs[:n_terms], refs[n_terms:2 * n_terms]
    res_ref = refs[2 * n_terms]
    acc = res_ref[...]
    for y_ref, w_ref in zip(ys, ws):
        acc = acc + jnp.dot(y_ref[...].astype(MXU_DTYPE), w_ref[...], preferred_element_type=F32)
    if with_norm:
        g_ref, o_ref, hn_ref = refs[2 * n_terms + 1:]
        hn_ref[...] = _rms(acc, g_ref[...]).astype(hn_ref.dtype)
    else:
        o_ref = refs[2 * n_terms + 1]
    o_ref[...] = acc


def matmul_residual(terms, w, res, next_gain, *, layer, kb, tm):
    m, n = res.shape
    y_specs = [pl.BlockSpec((tm, kb), functools.partial(lambda i, c: (i, c), c=cb)) for _, cb, _ in terms]
    w_specs = [pl.BlockSpec((None, kb, n), functools.partial(lambda i, r: (layer, r, 0), r=rb)) for _, _, rb in terms]
    row_spec = pl.BlockSpec((tm, n), lambda i: (i, 0))
    with_norm = next_gain is not None
    return pl.pallas_call(
        functools.partial(_matmul_res_kernel, n_terms=len(terms), with_norm=with_norm),
        grid=(m // tm,),
        in_specs=y_specs + w_specs + [row_spec] + ([pl.BlockSpec((1, n), lambda i: (0, 0))] if with_norm else []),
        out_specs=[row_spec, row_spec] if with_norm else row_spec,
        out_shape=([jax.ShapeDtypeStruct((m, n), F32), jax.ShapeDtypeStruct((m, n), MXU_DTYPE)] if with_norm
                   else jax.ShapeDtypeStruct((m, n), F32)),
        compiler_params=_params("parallel"),
        name="matmul_residual",
    )(*[t[0] for t in terms], *([w] * len(terms)), res, *([next_gain.reshape(1, n)] if with_norm else []))


def _mlp_kernel(*refs, with_norm):
    if with_norm:
        x_ref, g_ref, w1_ref, w2_ref, gn_ref, o_ref, hn_ref, h_scr = refs
    else:
        x_ref, g_ref, w1_ref, w2_ref, o_ref, h_scr = refs

    @pl.when(pl.program_id(1) == 0)
    def _():
        x = x_ref[...]
        h_scr[...] = _rms(x, g_ref[...]).astype(h_scr.dtype)
        o_ref[...] = x

    u = jnp.dot(h_scr[...], w1_ref[...], preferred_element_type=F32)
    a = jnp.square(jnp.maximum(u, 0.0)).astype(MXU_DTYPE)
    o_ref[...] += jnp.dot(a, w2_ref[...], preferred_element_type=F32)

    if with_norm:
        @pl.when(pl.program_id(1) == pl.num_programs(1) - 1)
        def _():
            hn_ref[...] = _rms(o_ref[...], gn_ref[...]).astype(hn_ref.dtype)


def mlp(x, gain, w1, w2, next_gain, *, layer, tm, tf):
    m, d = x.shape
    ff = w1.shape[2]
    row_spec = pl.BlockSpec((tm, d), lambda i, f: (i, 0))
    gain_spec = pl.BlockSpec((1, d), lambda i, f: (0, 0))
    with_norm = next_gain is not None
    return pl.pallas_call(
        functools.partial(_mlp_kernel, with_norm=with_norm),
        grid=(m // tm, ff // tf),
        in_specs=[row_spec, gain_spec,
                  pl.BlockSpec((None, d, tf), lambda i, f: (layer, 0, f)),
                  pl.BlockSpec((None, tf, d), lambda i, f: (layer, f, 0))] + ([gain_spec] if with_norm else []),
        out_specs=[row_spec, row_spec] if with_norm else row_spec,
        out_shape=([jax.ShapeDtypeStruct((m, d), F32), jax.ShapeDtypeStruct((m, d), MXU_DTYPE)] if with_norm
                   else jax.ShapeDtypeStruct((m, d), F32)),
        scratch_shapes=[pltpu.VMEM((tm, d), MXU_DTYPE)],
        compiler_params=_params("parallel", "arbitrary"),
        name="mlp",
    )(x, gain.reshape(1, d), w1, w2, *([next_gain.reshape(1, d)] if with_norm else []))


def _xa_kv_kernel(mem_ref, g_ref, wk_ref, wv_ref, kg_ref, k_ref, v_ref):
    mn = _rms(mem_ref[...], g_ref[...]).astype(MXU_DTYPE)
    k = jnp.dot(mn, wk_ref[...], preferred_element_type=F32)
    k_ref[...] = _rms(k, kg_ref[...]).astype(k_ref.dtype)
    v_ref[...] = jnp.dot(mn, wv_ref[...], preferred_element_type=F32).astype(v_ref.dtype)


def xa_kv(mem, gain, wk, wv, k_gain, *, layer, heads):
    ml, d = mem.shape
    hd = d // heads
    return pl.pallas_call(
        _xa_kv_kernel,
        grid=(heads,),
        in_specs=[
            pl.BlockSpec((ml, d), lambda h: (0, 0)),
            pl.BlockSpec((1, d), lambda h: (0, 0)),
            pl.BlockSpec((None, d, hd), lambda h: (layer, 0, h)),
            pl.BlockSpec((None, d, hd), lambda h: (layer, 0, h)),
            pl.BlockSpec((1, hd), lambda h: (0, 0)),
        ],
        out_specs=[pl.BlockSpec((ml, hd), lambda h: (0, h))] * 2,
        out_shape=[jax.ShapeDtypeStruct((ml, d), MXU_DTYPE)] * 2,
        compiler_params=_params("parallel"),
        name="xa_kv",
    )(mem, gain.reshape(1, d), wk, wv, k_gain.reshape(1, hd))


def _xa_attn_kernel(q_ref, qg_ref, k_ref, v_ref, o_ref, *, heads, hd):
    scale = hd ** -0.5
    for h in range(heads):
        cs = slice(h * hd, (h + 1) * hd)
        q = _rms(q_ref[:, cs], qg_ref[...]) * scale
        s = _dot_nt(q, k_ref[:, cs])
        p = jnp.exp(s - jnp.max(s, axis=-1, keepdims=True))
        p = p / jnp.sum(p, axis=-1, keepdims=True)
        o_ref[:, cs] = _dot(p, v_ref[:, cs]).astype(o_ref.dtype)


def xa_attn(q, q_gain, k, v, *, heads, tm):
    m, d = q.shape
    ml = k.shape[0]
    hd = d // heads
    return pl.pallas_call(
        functools.partial(_xa_attn_kernel, heads=heads, hd=hd),
        grid=(m // tm,),
        in_specs=[
            pl.BlockSpec((tm, d), lambda i: (i, 0)),
            pl.BlockSpec((1, hd), lambda i: (0, 0)),
            pl.BlockSpec((ml, d), lambda i: (0, 0)),
            pl.BlockSpec((ml, d), lambda i: (0, 0)),
        ],
        out_specs=pl.BlockSpec((tm, d), lambda i: (i, 0)),
        out_shape=jax.ShapeDtypeStruct((m, d), MXU_DTYPE),
        compiler_params=_params("parallel"),
        name="xa_attn",
    )(q, q_gain.reshape(1, hd), k, v)


def _seg_cumsum_rows(x, seg):
    pos = lax.broadcasted_iota(jnp.int32, x.shape, 0) % seg
    k = 1
    while k < seg:
        x = x + jnp.where(pos >= k, pltpu.roll(x, k, axis=0), 0.0)
        k *= 2
    return x


def _seg_cumsum_lanes(x, seg):
    pos = lax.broadcasted_iota(jnp.int32, x.shape, 1) % seg
    k = 1
    while k < seg:
        x = x + jnp.where(pos >= k, pltpu.roll(x, k, axis=1), 0.0)
        k *= 2
    return x


def _sigmoid(x):
    return 1.0 / (1.0 + jnp.exp(-x))


def _log_sigmoid(x):
    return jnp.minimum(x, 0.0) - jnp.log(1.0 + jnp.exp(-jnp.abs(x)))


HG_SUB = 16
HG_MIN_FORGET = 1e-20


def _hgrn2_kernel(q_ref, f_ref, i_ref, g_ref, lbl_ref, ng_ref, y_ref, st_ref, *, layer, heads, tb):
    dk = HEAD_DIM

    @pl.when(pl.program_id(0) == 0)
    def _():
        st_ref[...] = jnp.zeros_like(st_ref)

    logits = lbl_ref[...]
    e = jnp.exp(logits - jnp.max(logits, axis=0, keepdims=True))
    probs = e / jnp.sum(e, axis=0, keepdims=True)
    lb_all = jnp.zeros_like(probs[0:1])
    for l in range(1, layer + 1):
        lb_all = lb_all + probs[l:l + 1]
    row = lax.broadcasted_iota(jnp.int32, (HG_SUB, dk), 0)

    def body(c, carry):
        r = pl.ds(pl.multiple_of(c * HG_SUB, HG_SUB), HG_SUB)
        pre = []
        for h in range(heads):
            cs = slice(h * dk, (h + 1) * dk)
            lb = lb_all[:, cs]
            q, zf, v = q_ref[r, cs], f_ref[r, cs], i_ref[r, cs]
            forget = lb + (1.0 - lb) * _sigmoid(zf)
            log_f = jnp.log(jnp.maximum(forget, HG_MIN_FORGET))
            key = (1.0 - lb) * _sigmoid(-zf)
            b = _seg_cumsum_rows(log_f, HG_SUB)
            pre.append((q, v, key, b, b[HG_SUB - 1:HG_SUB]))
        inter = [_dot_nt(q * jnp.exp(b), st_ref[h]) for h, (q, v, key, b, b_end) in enumerate(pre)]
        upd = [_dot_tn(v, key * jnp.exp(b_end - b)) for q, v, key, b, b_end in pre]
        for h in range(heads):
            cs = slice(h * dk, (h + 1) * dk)
            q, v, key, b, b_end = pre[h]
            o = jnp.zeros((HG_SUB, dk), F32)
            for s in range(HG_SUB):
                decay = jnp.exp(jnp.where(row >= s, b - b[s:s + 1], NEG_INF))
                a = jnp.sum(q * decay * key[s:s + 1], axis=-1, keepdims=True)
                o = o + a * v[s:s + 1]
            o = o + inter[h]
            st_ref[h] = jnp.exp(b_end) * st_ref[h] + upd[h]
            g = g_ref[r, cs]
            y = _rms(o, ng_ref[:, cs]) * (g * _sigmoid(g))
            y_ref[r, cs] = y.astype(y_ref.dtype)
        return carry

    lax.fori_loop(0, tb // HG_SUB, body, 0)


def hgrn2(z, lb_logits, norm_gain, *, layer, heads, tb):
    t = z.shape[0]
    w = heads * HEAD_DIM
    depth = lb_logits.shape[0]
    return pl.pallas_call(
        functools.partial(_hgrn2_kernel, layer=layer, heads=heads, tb=tb),
        grid=(t // tb,),
        in_specs=[pl.BlockSpec((tb, w), functools.partial(lambda i, c: (i, c), c=c)) for c in range(4)] + [
            pl.BlockSpec((depth, w), lambda i: (0, 0)),
            pl.BlockSpec((1, w), lambda i: (0, 0)),
        ],
        out_specs=pl.BlockSpec((tb, w), lambda i: (i, 0)),
        out_shape=jax.ShapeDtypeStruct((t, w), MXU_DTYPE),
        scratch_shapes=[pltpu.VMEM((heads, HEAD_DIM, HEAD_DIM), F32)],
        compiler_params=_params("arbitrary"),
        name="hgrn2",
    )(z, z, z, z, lb_logits, norm_gain.reshape(1, w))


ML_CHUNK = 64
ML_CONV = 4
ML_TAIL = 8
ZS_ML_I = 0
ZS_ML_F = 4


def _mlstm_kernel(q_ref, k_ref, v_ref, o_ref, zs_ref, gt_ref, cw_ref, ng_ref, y_ref,
                  conv_scr, c_scr, n_scr, m_scr, *, heads, tb):
    d = HEAD_DIM
    w = heads * d

    @pl.when(pl.program_id(0) == 0)
    def _():
        conv_scr[0:ML_TAIL, :] = jnp.zeros((ML_TAIL, 2 * w), F32)
        c_scr[...] = jnp.zeros_like(c_scr)
        n_scr[...] = jnp.zeros_like(n_scr)
        m_scr[...] = jnp.zeros_like(m_scr)

    conv_scr[ML_TAIL:ML_TAIL + tb, 0:w] = q_ref[...]
    conv_scr[ML_TAIL:ML_TAIL + tb, w:2 * w] = k_ref[...]
    acc = jnp.zeros((tb, 2 * w), F32)
    for j in range(ML_CONV):
        off = ML_TAIL - (ML_CONV - 1) + j
        acc = acc + conv_scr[off:off + tb, :] * cw_ref[j:j + 1, :]
    conv_scr[0:ML_TAIL, :] = conv_scr[tb:tb + ML_TAIL, :]
    qk = acc * _sigmoid(acc)

    zs = zs_ref[...]
    b_col_all = _seg_cumsum_rows(_log_sigmoid(zs), ML_CHUNK)
    gt = gt_ref[...]
    b_row_all = _seg_cumsum_lanes(_log_sigmoid(gt), ML_CHUNK)
    tri = (lax.broadcasted_iota(jnp.int32, (ML_CHUNK, ML_CHUNK), 0)
           >= lax.broadcasted_iota(jnp.int32, (ML_CHUNK, ML_CHUNK), 1))

    def rows(c):
        return slice(c * ML_CHUNK, (c + 1) * ML_CHUNK)

    def qk_scores(c):
        return [_dot_nt(qk[rows(c), h * d:(h + 1) * d] * (d ** -0.5), qk[rows(c), w + h * d:w + (h + 1) * d])
                for h in range(heads)]

    def state_free(c, s_raw):
        rs = rows(c)
        out = []
        for h in range(heads):
            b_col = b_col_all[rs, ZS_ML_F + h:ZS_ML_F + h + 1]
            b_row = b_row_all[heads + h:heads + h + 1, rs]
            li_row = gt[h:h + 1, rs]
            d_log = jnp.where(tri, b_col - b_row + li_row, NEG_INF)
            dmax = jnp.max(d_log, axis=-1, keepdims=True)
            s = s_raw[h] * jnp.exp(d_log - dmax)
            b_end = b_col[ML_CHUNK - 1:ML_CHUNK]
            w_state = b_end - b_col + zs[rs, ZS_ML_I + h:ZS_ML_I + h + 1]
            wmax = jnp.max(w_state, axis=0, keepdims=True)
            k_w = qk[rs, w + h * d:w + (h + 1) * d] * jnp.exp(w_state - wmax)
            out.append(dict(b_col=b_col, dmax=dmax, b_end=b_end, wmax=wmax, s=s, k_w=k_w,
                            ssum=jnp.sum(s, axis=-1, keepdims=True), ksum=jnp.sum(k_w, axis=0, keepdims=True)))
        for h in range(heads):
            vc = v_ref[rs, h * d:(h + 1) * d]
            out[h]["intra"] = _dot(out[h]["s"], vc)
            out[h]["upd"] = _dot_tn(out[h]["k_w"], vc)
        return out

    n_chunks = tb // ML_CHUNK
    scores = {0: qk_scores(0)}
    if n_chunks > 1:
        scores[1] = qk_scores(1)
    free = {0: state_free(0, scores.pop(0))}
    for c in range(n_chunks):
        rs = rows(c)
        if c + 2 < n_chunks:
            scores[c + 2] = qk_scores(c + 2)
        inter = [_dot(qk[rs, h * d:(h + 1) * d] * (d ** -0.5), c_scr[h]) for h in range(heads)]
        if c + 1 < n_chunks:
            free[c + 1] = state_free(c + 1, scores.pop(c + 1))
        cur = free.pop(c)
        for h in range(heads):
            cs = slice(h * d, (h + 1) * d)
            f = cur[h]
            m_prev, n_mem = m_scr[h], n_scr[h]
            inter_log = f["b_col"] + m_prev
            m_t = jnp.maximum(inter_log, f["dmax"])
            r_intra = jnp.exp(f["dmax"] - m_t)
            w_inter = jnp.exp(inter_log - m_t)
            qc = qk[rs, cs] * (d ** -0.5)
            num = w_inter * inter[h] + r_intra * f["intra"]
            qn = w_inter * jnp.sum(qc * n_mem, axis=-1, keepdims=True) + r_intra * f["ssum"]
            hcell = num / jnp.maximum(jnp.abs(qn), jnp.exp(-m_t))
            m_new = jnp.maximum(f["b_end"] + m_prev, f["wmax"])
            carry_decay = jnp.exp(f["b_end"] + m_prev - m_new)
            r_state = jnp.exp(f["wmax"] - m_new)
            c_scr[h] = carry_decay * c_scr[h] + r_state * f["upd"]
            n_scr[h] = carry_decay * n_mem + r_state * f["ksum"]
            m_scr[h] = m_new
            y = _rms(hcell, ng_ref[:, cs]) * _sigmoid(o_ref[rs, cs])
            y_ref[rs, cs] = y.astype(y_ref.dtype)


def mlstm(z, zs, gt, conv_w, norm_gain, *, heads, tb, col0, zs_block):
    t = z.shape[0]
    w = heads * HEAD_DIM
    cb = col0 // w
    return pl.pallas_call(
        functools.partial(_mlstm_kernel, heads=heads, tb=tb),
        grid=(t // tb,),
        in_specs=[pl.BlockSpec((tb, w), functools.partial(lambda i, c: (i, c), c=cb + c)) for c in range(4)] + [
            pl.BlockSpec((tb, 128), lambda i: (i, zs_block)),
            pl.BlockSpec((8, tb), lambda i: (0, i)),
            pl.BlockSpec((ML_CONV, 2 * w), lambda i: (0, 0)),
            pl.BlockSpec((1, w), lambda i: (0, 0)),
        ],
        out_specs=pl.BlockSpec((tb, w), lambda i: (i, 0)),
        out_shape=jax.ShapeDtypeStruct((t, w), MXU_DTYPE),
        scratch_shapes=[
            pltpu.VMEM((tb + ML_TAIL, 2 * w), F32),
            pltpu.VMEM((heads, HEAD_DIM, HEAD_DIM), F32),
            pltpu.VMEM((heads, 1, HEAD_DIM), F32),
            pltpu.VMEM((heads, 1, 1), F32),
        ],
        compiler_params=_params("arbitrary"),
        name="mlstm",
    )(z, z, z, z, zs, gt, conv_w, norm_gain.reshape(1, w))


NSA_GROUP = 4
CMP_LEN = 32
CMP_STRIDE = 16
SEL_BLOCK = 64
N_SELECT = 16
WINDOW = 512
FORCE_BONUS = 1e4
SEL_LANES = 128
LOG2_E = 1.4426950408889634


def _rope(x, cos_t, sin_t):
    return x * cos_t + pltpu.roll(x, HEAD_DIM // 2, axis=1) * sin_t


def _nsa_prep_kernel(q_ref, ks_ref, vs_ref, kw_ref, vw_ref, cos_ref, sin_ref, et_ref, qg_ref, kg_ref,
                     qh_ref, ksa_ref, vso_ref, kwr_ref, vwo_ref, *, heads, kv_heads):
    d = HEAD_DIM
    cos_t, sin_t = cos_ref[...], sin_ref[...]
    for h in range(heads):
        cs = slice(h * d, (h + 1) * d)
        q = _rope(_rms(q_ref[:, cs], qg_ref[...]), cos_t, sin_t) * (d ** -0.5 * LOG2_E)
        qh_ref[cs, :] = q.T.astype(qh_ref.dtype)
    for g in range(kv_heads):
        cs = slice(g * d, (g + 1) * d)
        ksa_ref[g, :, 0:d] = _rope(_rms(ks_ref[:, cs], kg_ref[1:2]), cos_t, sin_t).astype(ksa_ref.dtype)
        ksa_ref[g, :, d:2 * d] = et_ref[...]
        kwr_ref[g] = _rope(_rms(kw_ref[:, cs], kg_ref[2:3]), cos_t, sin_t).astype(kwr_ref.dtype)
        vso_ref[g] = vs_ref[:, cs].T.astype(vso_ref.dtype)
        vwo_ref[g] = vw_ref[:, cs].T.astype(vwo_ref.dtype)


def nsa_prep(z, cos_t, sin_t, et, q_gain, k_gains, *, heads, kv_heads, tq, col_q):
    t = z.shape[0]
    d = HEAD_DIM
    kvw = kv_heads * d
    qb = col_q // (heads * d)
    kb = (col_q + heads * d) // kvw

    def zcol(width, c):
        return pl.BlockSpec((tq, width), functools.partial(lambda i, c: (i, c), c=c))

    k_out = pl.BlockSpec((kv_heads, tq, d), lambda i: (0, i, 0))
    k_shape = jax.ShapeDtypeStruct((kv_heads, t, d), MXU_DTYPE)
    vt_out = pl.BlockSpec((kv_heads, d, tq), lambda i: (0, 0, i))
    vt_shape = jax.ShapeDtypeStruct((kv_heads, d, t), MXU_DTYPE)
    return pl.pallas_call(
        functools.partial(_nsa_prep_kernel, heads=heads, kv_heads=kv_heads),
        grid=(t // tq,),
        in_specs=[zcol(heads * d, qb), zcol(kvw, kb + 2), zcol(kvw, kb + 3), zcol(kvw, kb + 4), zcol(kvw, kb + 5),
                  pl.BlockSpec((tq, d), lambda i: (i, 0)), pl.BlockSpec((tq, d), lambda i: (i, 0)),
                  pl.BlockSpec((tq, SEL_LANES), lambda i: (i, 0)),
                  pl.BlockSpec((1, d), lambda i: (0, 0)), pl.BlockSpec((3, d), lambda i: (0, 0))],
        out_specs=[pl.BlockSpec((heads * d, tq), lambda i: (0, i)),
                   pl.BlockSpec((kv_heads, tq, 2 * d), lambda i: (0, i, 0)), vt_out, k_out, vt_out],
        out_shape=[jax.ShapeDtypeStruct((heads * d, t), MXU_DTYPE),
                   jax.ShapeDtypeStruct((kv_heads, t, 2 * d), MXU_DTYPE), vt_shape, k_shape, vt_shape],
        compiler_params=_params("parallel"),
        name="nsa_prep",
    )(z, z, z, z, z, cos_t, sin_t, et, q_gain.reshape(1, d), k_gains)


def _nsa_compress_kernel(zk_ref, zv_ref, pe_ref, w_ref, kg_ref, cos_ref, sin_ref, kc_ref, vc_ref, *, ncp):
    d = HEAD_DIM
    half = CMP_LEN // 2
    row = lax.broadcasted_iota(jnp.int32, (ncp, d), 0)

    def compress(z_ref, which):
        lo = jnp.zeros((ncp, d), F32)
        hi = jnp.zeros((ncp, d), F32)
        for l in range(half):
            xl = z_ref[pl.ds(l, ncp, stride=CMP_STRIDE), :]
            lo = lo + _dot(xl + pe_ref[which, l:l + 1, :], w_ref[which, l])
            hi = hi + _dot(xl + pe_ref[which, half + l:half + l + 1, :], w_ref[which, half + l])
        out = lo + pltpu.roll(hi, ncp - 1, axis=0)
        return jnp.where(row < ncp - 1, out, 0.0)

    kc_ref[0] = _rope(_rms(compress(zk_ref, 0), kg_ref[0:1]), cos_ref[...], sin_ref[...]).astype(kc_ref.dtype)
    vc_ref[0] = compress(zv_ref, 1).T.astype(vc_ref.dtype)


def nsa_compress(z, cmp_pos, cmp_w, k_gains, cos_c, sin_c, *, kv_heads, col_kc):
    t = z.shape[0]
    d = HEAD_DIM
    ncp = t // CMP_STRIDE
    kb = col_kc // d
    out_spec = pl.BlockSpec((1, ncp, d), lambda g: (g, 0, 0))
    out_shape = jax.ShapeDtypeStruct((kv_heads, ncp, d), MXU_DTYPE)
    return pl.pallas_call(
        functools.partial(_nsa_compress_kernel, ncp=ncp),
        grid=(kv_heads,),
        in_specs=[pl.BlockSpec((t, d), lambda g: (0, kb + g)),
                  pl.BlockSpec((t, d), lambda g: (0, kb + kv_heads + g)),
                  pl.BlockSpec((2, CMP_LEN, d), lambda g: (0, 0, 0)),
                  pl.BlockSpec((2, CMP_LEN, d, d), lambda g: (0, 0, 0, 0)),
                  pl.BlockSpec((3, d), lambda g: (0, 0)),
                  pl.BlockSpec((ncp, d), lambda g: (0, 0)), pl.BlockSpec((ncp, d), lambda g: (0, 0))],
        out_specs=[out_spec, pl.BlockSpec((1, d, ncp), lambda g: (g, 0, 0))],
        out_shape=[out_shape, jax.ShapeDtypeStruct((kv_heads, d, ncp), MXU_DTYPE)],
        compiler_params=_params("parallel"),
        name="nsa_compress",
    )(z, z, cmp_pos, cmp_w, k_gains, cos_c, sin_c)


def _nsa_cmp_kernel(qt_ref, kc_ref, vct_ref, ovt_ref, ocmp_ref, mb_ref, *, tq, grp, ncp):
    d = HEAD_DIM
    qi = pl.program_id(1)
    cmp_end = lax.broadcasted_iota(jnp.int32, (ncp, tq), 0) * CMP_STRIDE + (CMP_LEN - 1)
    vis = cmp_end <= qi * tq + lax.broadcasted_iota(jnp.int32, (ncp, tq), 1)
    psum = None

    def scores(j):
        return jnp.dot(kc_ref[0], qt_ref[j * d:(j + 1) * d, :], preferred_element_type=F32)

    s_next = scores(0)
    for j in range(grp):
        s = s_next
        if j + 1 < grp:
            s_next = scores(j + 1)
        sm = jnp.where(vis, s, NEG_INF)
        p = jnp.where(vis, jnp.exp2(sm - jnp.max(sm, axis=0, keepdims=True)), 0.0)
        l = jnp.sum(p, axis=0, keepdims=True)
        p = p * (1.0 / jnp.where(l > 0.0, l, 1.0))
        ocmp_ref[j * d:(j + 1) * d, :] = _dot(vct_ref[0], p)
        psum = p if psum is None else psum + p
    hi = psum.astype(MXU_DTYPE)
    lo = (psum - hi.astype(F32)).astype(MXU_DTYPE)
    imp = (jnp.dot(ovt_ref[...], hi, preferred_element_type=F32)
           + jnp.dot(ovt_ref[...], lo, preferred_element_type=F32))
    blk = lax.broadcasted_iota(jnp.int32, (SEL_LANES, tq), 0)
    cur = (qi * tq + lax.broadcasted_iota(jnp.int32, (SEL_LANES, tq), 1)) // SEL_BLOCK
    forced = (blk == 0) | (blk == cur) | (blk == cur - 1)
    score = jnp.where(blk <= cur, imp + jnp.where(forced, FORCE_BONUS, 0.0), NEG_INF)
    blk_f = blk.astype(F32)
    bias = jnp.full((SEL_LANES, tq), NEG_INF, F32)
    for _ in range(N_SELECT):
        mx = jnp.max(score, axis=0, keepdims=True)
        first = jnp.min(jnp.where(score == mx, blk_f, float(SEL_LANES)), axis=0, keepdims=True)
        pick = blk_f == first
        bias = jnp.where(pick, jnp.where(mx > 0.5 * NEG_INF, 0.0, NEG_INF), bias)
        score = jnp.where(pick, -jnp.inf, score)
    mb_ref[0] = bias.astype(mb_ref.dtype)


def nsa_cmp_select(qt, kc, vct, ovt, *, kv_heads, tq):
    t = qt.shape[1]
    d = HEAD_DIM
    ncp = kc.shape[1]
    gw = NSA_GROUP * d
    return pl.pallas_call(
        functools.partial(_nsa_cmp_kernel, tq=tq, grp=NSA_GROUP, ncp=ncp),
        grid=(kv_heads, t // tq),
        in_specs=[pl.BlockSpec((gw, tq), lambda g, i: (g, i)),
                  pl.BlockSpec((1, ncp, d), lambda g, i: (g, 0, 0)),
                  pl.BlockSpec((1, d, ncp), lambda g, i: (g, 0, 0)),
                  pl.BlockSpec((SEL_LANES, ncp), lambda g, i: (0, 0))],
        out_specs=[pl.BlockSpec((gw, tq), lambda g, i: (g, i)),
                   pl.BlockSpec((1, SEL_LANES, tq), lambda g, i: (g, 0, i))],
        out_shape=[jax.ShapeDtypeStruct((kv_heads * gw, t), F32),
                   jax.ShapeDtypeStruct((kv_heads, SEL_LANES, t), MXU_DTYPE)],
        compiler_params=_params("parallel", "parallel"),
        name="nsa_cmp_select",
    )(qt, kc, vct, ovt)


def _nsa_sel_kernel(qi_tab, kj_tab, qt_ref, mb_ref, ka_ref, vt_ref, o_ref, qa_scr, m_scr, l_scr, acc_scr,
                    *, tq, tk, grp, hpu):
    d = HEAD_DIM
    step = pl.program_id(1)
    qi, kj = qi_tab[step], kj_tab[step]
    last = (qi * tq + tq - 1) // tk
    units = grp // hpu
    uw = hpu * tq

    @pl.when(kj == 0)
    def _():
        for j in range(grp):
            u, ls = j // hpu, slice((j % hpu) * tq, (j % hpu + 1) * tq)
            qa_scr[u, 0:d, ls] = qt_ref[j * d:(j + 1) * d, :]
            qa_scr[u, d:2 * d, ls] = mb_ref[0]
        m_scr[...] = jnp.full_like(m_scr, NEG_INF)
        l_scr[...] = jnp.zeros_like(l_scr)
        acc_scr[...] = jnp.zeros_like(acc_scr)

    def update(diagonal):
        if diagonal:
            causal = (kj * tk + lax.broadcasted_iota(jnp.int32, (tk, uw), 0)
                      <= qi * tq + lax.broadcasted_iota(jnp.int32, (tk, uw), 1) % tq)

        def scores(u):
            return jnp.dot(ka_ref[0], qa_scr[u], preferred_element_type=F32)

        s_next = scores(0)
        for u in range(units):
            s = s_next
            if u + 1 < units:
                s_next = scores(u + 1)
            if diagonal:
                s = jnp.where(causal, s, NEG_INF)
            m_prev = m_scr[u]
            m_new = jnp.maximum(m_prev, jnp.max(s, axis=0, keepdims=True))
            alpha = jnp.exp2(m_prev - m_new)
            p = jnp.exp2(s - m_new)
            l_scr[u] = alpha * l_scr[u] + jnp.sum(p, axis=0, keepdims=True)
            acc_scr[u] = alpha * acc_scr[u] + _dot(vt_ref[0], p)
            m_scr[u] = m_new

    @pl.when(kj < last)
    def _():
        update(False)

    @pl.when(kj == last)
    def _():
        update(True)
        for j in range(grp):
            u, ls = j // hpu, slice((j % hpu) * tq, (j % hpu + 1) * tq)
            o_ref[j * d:(j + 1) * d, :] = acc_scr[u, :, ls] * (1.0 / l_scr[u, :, ls])


def nsa_sel_attn(qt, mb, ks_aug, vst, *, kv_heads, tq, tk, hpu):
    t = qt.shape[1]
    d = HEAD_DIM
    gw = NSA_GROUP * d
    pairs = [(qi, kj) for qi in range(t // tq) for kj in range((qi * tq + tq - 1) // tk + 1)]
    qi_tab = jnp.asarray(np.array([p[0] for p in pairs], np.int32))
    kj_tab = jnp.asarray(np.array([p[1] for p in pairs], np.int32))
    units, uw = NSA_GROUP // hpu, hpu * tq
    grid_spec = pltpu.PrefetchScalarGridSpec(
        num_scalar_prefetch=2,
        grid=(kv_heads, len(pairs)),
        in_specs=[pl.BlockSpec((gw, tq), lambda g, s, qt, kt: (g, qt[s])),
                  pl.BlockSpec((1, SEL_LANES, tq), lambda g, s, qt, kt: (g, 0, qt[s])),
                  pl.BlockSpec((1, tk, 2 * d), lambda g, s, qt, kt: (g, kt[s], 0)),
                  pl.BlockSpec((1, d, tk), lambda g, s, qt, kt: (g, 0, kt[s]))],
        out_specs=pl.BlockSpec((gw, tq), lambda g, s, qt, kt: (g, qt[s])),
        scratch_shapes=[pltpu.VMEM((units, 2 * d, uw), MXU_DTYPE), pltpu.VMEM((units, 1, uw), F32),
                        pltpu.VMEM((units, 1, uw), F32), pltpu.VMEM((units, d, uw), F32)],
    )
    return pl.pallas_call(
        functools.partial(_nsa_sel_kernel, tq=tq, tk=tk, grp=NSA_GROUP, hpu=hpu),
        grid_spec=grid_spec,
        out_shape=jax.ShapeDtypeStruct((kv_heads * gw, t), F32),
        compiler_params=_params("parallel", "arbitrary"),
        name="nsa_sel_attn",
    )(qi_tab, kj_tab, qt, mb, ks_aug, vst)


def _nsa_win_kernel(qt_ref, k0_ref, k1_ref, k2_ref, v0_ref, v1_ref, v2_ref, ocmp_ref, osel_ref, zg_ref, y_ref,
                    *, tq, grp):
    d = HEAD_DIM
    qi = pl.program_id(1)
    k = jnp.concatenate([k0_ref[0], k1_ref[0], k2_ref[0]], axis=0)
    vt = jnp.concatenate([v0_ref[0], v1_ref[0], v2_ref[0]], axis=1)
    kpos = (qi - 2) * tq + lax.broadcasted_iota(jnp.int32, (3 * tq, tq), 0)
    rel = qi * tq + lax.broadcasted_iota(jnp.int32, (3 * tq, tq), 1) - kpos
    ok = (rel >= 0) & (rel < WINDOW) & (kpos >= 0)
    gates_t = _sigmoid(zg_ref[...]).T

    def scores(j):
        return jnp.dot(k, qt_ref[j * d:(j + 1) * d, :], preferred_element_type=F32)

    s_next = scores(0)
    for j in range(grp):
        rs = slice(j * d, (j + 1) * d)
        s = s_next
        if j + 1 < grp:
            s_next = scores(j + 1)
        sm = jnp.where(ok, s, NEG_INF)
        p = jnp.where(ok, jnp.exp2(sm - jnp.max(sm, axis=0, keepdims=True)), 0.0)
        o = _dot(vt, p) * (1.0 / jnp.sum(p, axis=0, keepdims=True))
        y = (gates_t[3 * j:3 * j + 1] * ocmp_ref[rs, :] + gates_t[3 * j + 1:3 * j + 2] * osel_ref[rs, :]
             + gates_t[3 * j + 2:3 * j + 3] * o)
        y_ref[:, rs] = y.T.astype(y_ref.dtype)


def nsa_win_combine(qt, kw, vwt, o_cmp, o_sel, zs, *, kv_heads, tq, zs_block0):
    t = qt.shape[1]
    d = HEAD_DIM
    gw = NSA_GROUP * d
    k_specs = [pl.BlockSpec((1, tq, d), functools.partial(lambda g, i, b: (g, jnp.maximum(i - b, 0), 0), b=b))
               for b in (2, 1, 0)]
    v_specs = [pl.BlockSpec((1, d, tq), functools.partial(lambda g, i, b: (g, 0, jnp.maximum(i - b, 0)), b=b))
               for b in (2, 1, 0)]
    tspec = pl.BlockSpec((gw, tq), lambda g, i: (g, i))
    return pl.pallas_call(
        functools.partial(_nsa_win_kernel, tq=tq, grp=NSA_GROUP),
        grid=(kv_heads, t // tq),
        in_specs=[tspec] + k_specs + v_specs + [tspec, tspec,
                                                pl.BlockSpec((tq, 128), lambda g, i: (i, zs_block0 + g))],
        out_specs=pl.BlockSpec((tq, gw), lambda g, i: (i, g)),
        out_shape=jax.ShapeDtypeStruct((t, kv_heads * gw), MXU_DTYPE),
        compiler_params=_params("parallel", "parallel"),
        name="nsa_win_combine",
    )(qt, kw, kw, kw, vwt, vwt, vwt, o_cmp, o_sel, zs)


def _split_in_proj(w_in, b_in, sizes):
    starts = np.concatenate([[0], np.cumsum(sizes)])
    i_gate, i_mli, i_mlf = 11, 16, 17
    g0, g1 = starts[i_gate], starts[i_gate + 1]
    half = (g1 - g0) // 2

    def pad(a, width):
        return jnp.pad(a, [(0, 0)] * (a.ndim - 1) + [(0, width - a.shape[-1])])

    def part_b(a):
        return jnp.concatenate([a[..., g1:starts[i_mli]], pad(a[..., g0:g0 + half], 128),
                                pad(a[..., g0 + half:g1], 128),
                                pad(a[..., starts[i_mli]:starts[i_mlf + 1]], 128)], axis=-1)

    return g0, part_b(w_in), part_b(b_in)


def kernel(x, mem, norm_mix, w_in, b_in, hgrn_lb_logits, hgrn_norm, nsa_q_norm, nsa_k_norm, nsa_cmp_pos, nsa_cmp_w, mlstm_conv, mlstm_norm, w_out, norm_xattn, norm_mem, xa_wq, xa_wk, xa_wv, xa_wo, xa_q_norm, xa_k_norm, norm_mlp, mlp_w1, mlp_w2):
    _, t, d_model = x.shape
    depth = w_in.shape[0]
    d = HEAD_DIM
    hg_heads = ml_heads = d_model // (4 * d)
    nsa_heads = d_model // (2 * d)
    kv_heads = nsa_heads // NSA_GROUP
    hw, nw, kvw = hg_heads * d, nsa_heads * d, kv_heads * d
    sizes = (hw,) * 4 + (nw,) + (kvw,) * 6 + (3 * nsa_heads,) + (hw,) * 4 + (ml_heads, ml_heads)
    col_nsa_q = 4 * hw
    col_nsa_kc = col_nsa_q + nw
    xa_heads = 4
    bf = MXU_DTYPE

    half = d // 2
    inv_freq = ROPE_THETA ** (-jnp.arange(half, dtype=F32) / half)

    def rope_tables(pos):
        ang = pos[:, None] * inv_freq[None, :]
        cos, sin = jnp.cos(ang), jnp.sin(ang)
        return jnp.concatenate([cos, cos], axis=-1), jnp.concatenate([-sin, sin], axis=-1)

    cos_t, sin_t = rope_tables(jnp.arange(t, dtype=F32))
    ncp = t // CMP_STRIDE
    n_cmp, n_sel = ncp - 1, t // SEL_BLOCK
    cos_c, sin_c = rope_tables(jnp.arange(ncp, dtype=F32) * CMP_STRIDE + (CMP_LEN - 1))
    et = jnp.asarray(np.arange(t)[:, None] // SEL_BLOCK == np.arange(SEL_LANES)[None, :], bf)
    c_start = np.arange(ncp)[:, None] * CMP_STRIDE
    s_start = np.arange(SEL_LANES)[None, :] * SEL_BLOCK
    overlap = ((c_start < s_start + SEL_BLOCK) & (c_start + CMP_LEN > s_start)
               & (np.arange(ncp)[:, None] < n_cmp) & (np.arange(SEL_LANES)[None, :] < n_sel))
    ovt = jnp.asarray(overlap.T, bf)

    w_in_c = w_in.astype(bf)
    n_a, w_b, b_b = _split_in_proj(w_in_c, b_in, sizes)
    w_out_c, wq_c, wk_c, wv_c, wo_c = (a.astype(bf) for a in (w_out, xa_wq, xa_wk, xa_wv, xa_wo))
    w1_c, w2_c, cmp_w_c = mlp_w1.astype(bf), mlp_w2.astype(bf), nsa_cmp_w.astype(bf)

    h = x.reshape(t, d_model)
    mem2 = mem.reshape(mem.shape[1], d_model)
    hn = rms_cast(h, norm_mix[0], tm=512)
    for layer in range(depth):
        z = matmul_cols(hn, w_in_c, b_in, layer=layer, n_tiles=3, tm=512, tn=n_a // 3)
        zb = matmul_cols(hn, w_b, b_b, layer=layer, n_tiles=1, tm=512, tn=w_b.shape[2])
        gate_block = 4 * hw // 128
        gt = zb[:, (gate_block + 2) * 128:(gate_block + 2) * 128 + 8].T

        y_hg = hgrn2(z, hgrn_lb_logits, hgrn_norm[layer], layer=layer, heads=hg_heads, tb=512)
        y_ml = mlstm(zb, zb, gt, mlstm_conv[layer], mlstm_norm[layer], heads=ml_heads, tb=512, col0=0,
                     zs_block=gate_block + 2)

        qt, ks_aug, vst, kw, vwt = nsa_prep(z, cos_t, sin_t, et, nsa_q_norm[layer], nsa_k_norm[layer],
                                            heads=nsa_heads, kv_heads=kv_heads, tq=512, col_q=col_nsa_q)
        kc, vct = nsa_compress(z, nsa_cmp_pos[layer], cmp_w_c[layer], nsa_k_norm[layer],
                               cos_c, sin_c, kv_heads=kv_heads, col_kc=col_nsa_kc)
        o_cmp, mb = nsa_cmp_select(qt, kc, vct, ovt, kv_heads=kv_heads, tq=256)
        o_sel = nsa_sel_attn(qt, mb, ks_aug, vst, kv_heads=kv_heads, tq=512, tk=512, hpu=1)
        y_ns = nsa_win_combine(qt, kw, vwt, o_cmp, o_sel, zb, kv_heads=kv_heads, tq=256, zs_block0=gate_block)

        kb = hw
        terms = [(y_hg, 0, 0)] + [(y_ns, c, 1 + c) for c in range(nw // kb)] + [(y_ml, 0, 1 + nw // kb)]
        h, hn = matmul_residual(terms, w_out_c, h, norm_xattn[layer], layer=layer, kb=kb, tm=512)

        k_mem, v_mem = xa_kv(mem2, norm_mem[layer], wk_c, wv_c, xa_k_norm[layer], layer=layer, heads=xa_heads)
        q_x = matmul_cols(hn, wq_c, None, layer=layer, n_tiles=1, tm=512, tn=d_model)
        o_x = xa_attn(q_x, xa_q_norm[layer], k_mem, v_mem, heads=xa_heads, tm=512)
        h = matmul_residual([(o_x, c, c) for c in range(d_model // kb)], wo_c, h, None, layer=layer, kb=kb, tm=512)

        if layer + 1 < depth:
            h, hn = mlp(h, norm_mlp[layer], w1_c, w2_c, norm_mix[layer + 1], layer=layer, tm=512, tf=1024)
        else:
            h = mlp(h, norm_mlp[layer], w1_c, w2_c, None, layer=layer, tm=512, tf=1024)
    return h.reshape(x.shape)
```

```python
import functools

import jax
import jax.numpy as jnp
import numpy as np
from jax import lax
from jax.experimental import pallas as pl
from jax.experimental.pallas import tpu as pltpu

F32 = jnp.float32
MXU_DTYPE = jnp.bfloat16

EPS = 1e-6
NEG_INF = -1e30
LOG2_E = 1.4426950408889634
HEAD_DIM = 128
ROPE_THETA = 10000.0

V7X_VMEM_LIMIT_BYTES = 56 * 1024 * 1024


def _params(*semantics):
    return pltpu.CompilerParams(dimension_semantics=semantics, vmem_limit_bytes=V7X_VMEM_LIMIT_BYTES)


def _rms(x, gain):
    return x * lax.rsqrt(jnp.mean(x * x, axis=-1, keepdims=True) + EPS) * gain


def _dot(a, b):
    return jnp.dot(a.astype(MXU_DTYPE), b.astype(MXU_DTYPE), preferred_element_type=F32)


def _dot_nt(a, b):
    return lax.dot_general(a.astype(MXU_DTYPE), b.astype(MXU_DTYPE), (((1,), (1,)), ((), ())),
                           preferred_element_type=F32)


def _dot_tn(a, b):
    return lax.dot_general(a.astype(MXU_DTYPE), b.astype(MXU_DTYPE), (((0,), (0,)), ((), ())),
                           preferred_element_type=F32)


def _rms_cast_kernel(x_ref, g_ref, o_ref):
    o_ref[...] = _rms(x_ref[...], g_ref[...]).astype(o_ref.dtype)


def rms_cast(x, gain, *, tm):
    m, d = x.shape
    return pl.pallas_call(
        _rms_cast_kernel,
        grid=(m // tm,),
        in_specs=[pl.BlockSpec((tm, d), lambda i: (i, 0)), pl.BlockSpec((1, d), lambda i: (0, 0))],
        out_specs=pl.BlockSpec((tm, d), lambda i: (i, 0)),
        out_shape=jax.ShapeDtypeStruct((m, d), MXU_DTYPE),
        compiler_params=_params("parallel"),
        name="rms_cast",
    )(x, gain.reshape(1, d))


def _matmul_bias_kernel(h_ref, w_ref, b_ref, o_ref):
    o_ref[...] = jnp.dot(h_ref[...], w_ref[...], preferred_element_type=F32) + b_ref[...]


def _matmul_kernel(h_ref, w_ref, o_ref):
    o_ref[...] = jnp.dot(h_ref[...], w_ref[...], preferred_element_type=F32)


def matmul_cols(h, w, bias, *, layer, n_tiles, tm, tn):
    m, d = h.shape
    in_specs = [pl.BlockSpec((tm, d), lambda j, i: (i, 0)),
                pl.BlockSpec((None, d, tn), lambda j, i: (layer, 0, j))]
    args = [h, w]
    if bias is not None:
        in_specs.append(pl.BlockSpec((None, 1, tn), lambda j, i: (layer, 0, j)))
        args.append(bias.reshape(bias.shape[0], 1, bias.shape[1]))
    return pl.pallas_call(
        _matmul_kernel if bias is None else _matmul_bias_kernel,
        grid=(n_tiles, m // tm),
        in_specs=in_specs,
        out_specs=pl.BlockSpec((tm, tn), lambda j, i: (i, j)),
        out_shape=jax.ShapeDtypeStruct((m, n_tiles * tn), F32),
        compiler_params=_params("parallel", "parallel"),
        name="matmul_cols",
    )(*args)


def _matmul_res_kernel(*refs, n_terms, with_norm):
    ys, ws = refs[:n_terms], refs[n_terms:2 * n_terms]
    res_ref = refs[2 * n_terms]
    acc = res_ref[...]
    for y_ref, w_ref in zip(ys, ws):
        acc = acc + jnp.dot(y_ref[...].astype(MXU_DTYPE), w_ref[...], preferred_element_type=F32)
    if with_norm:
        g_ref, o_ref, hn_ref = refs[2 * n_terms + 1:]
        hn_ref[...] = _rms(acc, g_ref[...]).astype(hn_ref.dtype)
    else:
        o_ref = refs[2 * n_terms + 1]
    o_ref[...] = acc


def matmul_residual(terms, w, res, next_gain, *, layer, kb, tm):
    m, n = res.shape
    y_specs = [pl.BlockSpec((tm, kb), functools.partial(lambda i, c: (i, c), c=cb)) for _, cb, _ in terms]
    w_specs = [pl.BlockSpec((None, kb, n), functools.partial(lambda i, r: (layer, r, 0), r=rb)) for _, _, rb in terms]
    row_spec = pl.BlockSpec((tm, n), lambda i: (i, 0))
    with_norm = next_gain is not None
    return pl.pallas_call(
        functools.partial(_matmul_res_kernel, n_terms=len(terms), with_norm=with_norm),
        grid=(m // tm,),
        in_specs=y_specs + w_specs + [row_spec] + ([pl.BlockSpec((1, n), lambda i: (0, 0))] if with_norm else []),
        out_specs=[row_spec, row_spec] if with_norm else row_spec,
        out_shape=([jax.ShapeDtypeStruct((m, n), F32), jax.ShapeDtypeStruct((m, n), MXU_DTYPE)] if with_norm
                   else jax.ShapeDtypeStruct((m, n), F32)),
        compiler_params=_params("parallel"),
        name="matmul_residual",
    )(*[t[0] for t in terms], *([w] * len(terms)), res, *([next_gain.reshape(1, n)] if with_norm else []))


def _mlp_kernel(*refs, with_norm):
    if with_norm:
        x_ref, g_ref, w1_ref, w2_ref, gn_ref, o_ref, hn_ref, h_scr = refs
    else:
        x_ref, g_ref, w1_ref, w2_ref, o_ref, h_scr = refs

    @pl.when(pl.program_id(1) == 0)
    def _():
        x = x_ref[...]
        h_scr[...] = _rms(x, g_ref[...]).astype(h_scr.dtype)
        o_ref[...] = x

    u = jnp.dot(h_scr[...], w1_ref[...], preferred_element_type=F32)
    a = jnp.square(jnp.maximum(u, 0.0)).astype(MXU_DTYPE)
    o_ref[...] += jnp.dot(a, w2_ref[...], preferred_element_type=F32)

    if with_norm:
        @pl.when(pl.program_id(1) == pl.num_programs(1) - 1)
        def _():
            hn_ref[...] = _rms(o_ref[...], gn_ref[...]).astype(hn_ref.dtype)


def mlp(x, gain, w1, w2, next_gain, *, layer, tm, tf):
    m, d = x.shape
    ff = w1.shape[2]
    row_spec = pl.BlockSpec((tm, d), lambda i, f: (i, 0))
    gain_spec = pl.BlockSpec((1, d), lambda i, f: (0, 0))
    with_norm = next_gain is not None
    return pl.pallas_call(
        functools.partial(_mlp_kernel, with_norm=with_norm),
        grid=(m // tm, ff // tf),
        in_specs=[row_spec, gain_spec,
                  pl.BlockSpec((None, d, tf), lambda i, f: (layer, 0, f)),
                  pl.BlockSpec((None, tf, d), lambda i, f: (layer, f, 0))] + ([gain_spec] if with_norm else []),
        out_specs=[row_spec, row_spec] if with_norm else row_spec,
        out_shape=([jax.ShapeDtypeStruct((m, d), F32), jax.ShapeDtypeStruct((m, d), MXU_DTYPE)] if with_norm
                   else jax.ShapeDtypeStruct((m, d), F32)),
        scratch_shapes=[pltpu.VMEM((tm, d), MXU_DTYPE)],
        compiler_params=_params("parallel", "arbitrary"),
        name="mlp",
    )(x, gain.reshape(1, d), w1, w2, *([next_gain.reshape(1, d)] if with_norm else []))


def _xa_kv_kernel(mem_ref, g_ref, wk_ref, wv_ref, kg_ref, k_ref, v_ref):
    mn = _rms(mem_ref[...], g_ref[...]).astype(MXU_DTYPE)
    k = jnp.dot(mn, wk_ref[...], preferred_element_type=F32)
    k_ref[...] = _rms(k, kg_ref[...]).astype(k_ref.dtype)
    v_ref[...] = jnp.dot(mn, wv_ref[...], preferred_element_type=F32).astype(v_ref.dtype)


def xa_kv(mem, gain, wk, wv, k_gain, *, layer, heads):
    ml, d = mem.shape
    hd = d // heads
    return pl.pallas_call(
        _xa_kv_kernel,
        grid=(heads,),
        in_specs=[
            pl.BlockSpec((ml, d), lambda h: (0, 0)),
            pl.BlockSpec((1, d), lambda h: (0, 0)),
            pl.BlockSpec((None, d, hd), lambda h: (layer, 0, h)),
            pl.BlockSpec((None, d, hd), lambda h: (layer, 0, h)),
            pl.BlockSpec((1, hd), lambda h: (0, 0)),
        ],
        out_specs=[pl.BlockSpec((ml, hd), lambda h: (0, h))] * 2,
        out_shape=[jax.ShapeDtypeStruct((ml, d), MXU_DTYPE)] * 2,
        compiler_params=_params("parallel"),
        name="xa_kv",
    )(mem, gain.reshape(1, d), wk, wv, k_gain.reshape(1, hd))


def _xa_attn_kernel(q_ref, qg_ref, k_ref, v_ref, o_ref, *, heads, hd):
    scale = hd ** -0.5
    for h in range(heads):
        cs = slice(h * hd, (h + 1) * hd)
        q = _rms(q_ref[:, cs], qg_ref[...]) * scale
        s = _dot_nt(q, k_ref[:, cs])
        p = jnp.exp(s - jnp.max(s, axis=-1, keepdims=True))
        p = p / jnp.sum(p, axis=-1, keepdims=True)
        o_ref[:, cs] = _dot(p, v_ref[:, cs]).astype(o_ref.dtype)


def xa_attn(q, q_gain, k, v, *, heads, tm):
    m, d = q.shape
    ml = k.shape[0]
    hd = d // heads
    return pl.pallas_call(
        functools.partial(_xa_attn_kernel, heads=heads, hd=hd),
        grid=(m // tm,),
        in_specs=[
            pl.BlockSpec((tm, d), lambda i: (i, 0)),
            pl.BlockSpec((1, hd), lambda i: (0, 0)),
            pl.BlockSpec((ml, d), lambda i: (0, 0)),
            pl.BlockSpec((ml, d), lambda i: (0, 0)),
        ],
        out_specs=pl.BlockSpec((tm, d), lambda i: (i, 0)),
        out_shape=jax.ShapeDtypeStruct((m, d), MXU_DTYPE),
        compiler_params=_params("parallel"),
        name="xa_attn",
    )(q, q_gain.reshape(1, hd), k, v)


def _seg_cumsum_rows(x, seg):
    pos = lax.broadcasted_iota(jnp.int32, x.shape, 0) % seg
    k = 1
    while k < seg:
        x = x + jnp.where(pos >= k, pltpu.roll(x, k, axis=0), 0.0)
        k *= 2
    return x


def _seg_cumsum_lanes(x, seg):
    pos = lax.broadcasted_iota(jnp.int32, x.shape, 1) % seg
    k = 1
    while k < seg:
        x = x + jnp.where(pos >= k, pltpu.roll(x, k, axis=1), 0.0)
        k *= 2
    return x


def _sigmoid(x):
    return 1.0 / (1.0 + jnp.exp(-x))


def _log_sigmoid(x):
    return jnp.minimum(x, 0.0) - jnp.log(1.0 + jnp.exp(-jnp.abs(x)))


HG_SUB = 16
HG_MIN_FORGET = 1e-20


def _hgrn2_kernel(q_ref, f_ref, i_ref, g_ref, lbl_ref, ng_ref, y_ref, st_ref, *, layer, heads, tb):
    dk = HEAD_DIM

    @pl.when(pl.program_id(0) == 0)
    def _():
        st_ref[...] = jnp.zeros_like(st_ref)

    logits = lbl_ref[...]
    e = jnp.exp(logits - jnp.max(logits, axis=0, keepdims=True))
    probs = e / jnp.sum(e, axis=0, keepdims=True)
    lb_all = jnp.zeros_like(probs[0:1])
    for l in range(1, layer + 1):
        lb_all = lb_all + probs[l:l + 1]
    half = HG_SUB // 2
    row_half = lax.broadcasted_iota(jnp.int32, (half, dk), 0)

    def body(c, carry):
        r = pl.ds(pl.multiple_of(c * HG_SUB, HG_SUB), HG_SUB)
        pre = []
        for h in range(heads):
            cs = slice(h * dk, (h + 1) * dk)
            lb = lb_all[:, cs]
            q, zf, v = q_ref[r, cs], f_ref[r, cs], i_ref[r, cs]
            forget = lb + (1.0 - lb) * _sigmoid(zf)
            log_f = jnp.log(jnp.maximum(forget, HG_MIN_FORGET))
            key = (1.0 - lb) * _sigmoid(-zf)
            b = _seg_cumsum_rows(log_f, HG_SUB) * LOG2_E
            pre.append((q, v, key, b, b[HG_SUB - 1:HG_SUB]))
        inter = [_dot_nt(q * jnp.exp2(b), st_ref[h]) for h, (q, v, key, b, b_end) in enumerate(pre)]
        upd = [_dot_tn(v, key * jnp.exp2(b_end - b)) for q, v, key, b, b_end in pre]
        for h in range(heads):
            cs = slice(h * dk, (h + 1) * dk)
            q, v, key, b, b_end = pre[h]
            q_lo, q_hi, b_lo, b_hi = q[0:half], q[half:], b[0:half], b[half:]
            o_lo = jnp.zeros((half, dk), F32)
            o_hi = jnp.zeros((half, dk), F32)
            for s in range(HG_SUB):
                bs, ks, vs = b[s:s + 1], key[s:s + 1], v[s:s + 1]
                if s < half:
                    decay = jnp.exp2(jnp.where(row_half >= s, b_lo - bs, NEG_INF))
                    o_lo = o_lo + jnp.sum(q_lo * decay * ks, axis=-1, keepdims=True) * vs
                    decay = jnp.exp2(b_hi - bs)
                else:
                    decay = jnp.exp2(jnp.where(row_half >= s - half, b_hi - bs, NEG_INF))
                o_hi = o_hi + jnp.sum(q_hi * decay * ks, axis=-1, keepdims=True) * vs
            o = jnp.concatenate([o_lo, o_hi], axis=0) + inter[h]
            st_ref[h] = jnp.exp2(b_end) * st_ref[h] + upd[h]
            g = g_ref[r, cs]
            y = _rms(o, ng_ref[:, cs]) * (g * _sigmoid(g))
            y_ref[r, cs] = y.astype(y_ref.dtype)
        return carry

    lax.fori_loop(0, tb // HG_SUB, body, 0)


def hgrn2(z, lb_logits, norm_gain, *, layer, heads, tb):
    t = z.shape[0]
    w = heads * HEAD_DIM
    depth = lb_logits.shape[0]
    return pl.pallas_call(
        functools.partial(_hgrn2_kernel, layer=layer, heads=heads, tb=tb),
        grid=(t // tb,),
        in_specs=[pl.BlockSpec((tb, w), functools.partial(lambda i, c: (i, c), c=c)) for c in range(4)] + [
            pl.BlockSpec((depth, w), lambda i: (0, 0)),
            pl.BlockSpec((1, w), lambda i: (0, 0)),
        ],
        out_specs=pl.BlockSpec((tb, w), lambda i: (i, 0)),
        out_shape=jax.ShapeDtypeStruct((t, w), MXU_DTYPE),
        scratch_shapes=[pltpu.VMEM((heads, HEAD_DIM, HEAD_DIM), F32)],
        compiler_params=_params("arbitrary"),
        name="hgrn2",
    )(z, z, z, z, lb_logits, norm_gain.reshape(1, w))


ML_CHUNK = 64
ML_CONV = 4
ML_TAIL = 8
ML_AUG = HEAD_DIM + 16
ZS_ML_I = 0
ZS_ML_F = 4


def _mlstm_kernel(q_ref, k_ref, v_ref, o_ref, zs_ref, gt_ref, cw_ref, ng_ref, y_ref,
                  conv_scr, st_scr, m_scr, *, heads, tb):
    d = HEAD_DIM
    w = heads * d

    @pl.when(pl.program_id(0) == 0)
    def _():
        conv_scr[0:ML_TAIL, :] = jnp.zeros((ML_TAIL, 2 * w), F32)
        st_scr[...] = jnp.zeros_like(st_scr)
        m_scr[...] = jnp.zeros_like(m_scr)

    conv_scr[ML_TAIL:ML_TAIL + tb, 0:w] = q_ref[...]
    conv_scr[ML_TAIL:ML_TAIL + tb, w:2 * w] = k_ref[...]
    acc = jnp.zeros((tb, 2 * w), F32)
    for j in range(ML_CONV):
        off = ML_TAIL - (ML_CONV - 1) + j
        acc = acc + conv_scr[off:off + tb, :] * cw_ref[j:j + 1, :]
    conv_scr[0:ML_TAIL, :] = conv_scr[tb:tb + ML_TAIL, :]
    qk = acc * _sigmoid(acc)

    pair = 2 * ML_CHUNK
    zs = zs_ref[...]
    b_col_all = _seg_cumsum_rows(_log_sigmoid(zs), ML_CHUNK)
    gt = gt_ref[...]
    b_row_all = _seg_cumsum_lanes(_log_sigmoid(gt), ML_CHUNK)
    s_idx = lax.broadcasted_iota(jnp.int32, (pair, pair), 0)
    t_idx = lax.broadcasted_iota(jnp.int32, (pair, pair), 1)
    visible = (s_idx // ML_CHUNK == t_idx // ML_CHUNK) & (s_idx <= t_idx)
    lane = lax.broadcasted_iota(jnp.int32, (1, pair), 1)
    in_chunk = [lane < ML_CHUNK, lane >= ML_CHUNK]
    ones_rows = (lax.broadcasted_iota(jnp.int32, (ML_AUG - d, pair), 0) == 0).astype(F32)

    def rows(p):
        return slice(p * pair, (p + 1) * pair)

    def stage_scores(p):
        out = []
        for h in range(heads):
            qt = (qk[rows(p), h * d:(h + 1) * d] * (d ** -0.5)).T
            out.append((qt, _dot(qk[rows(p), w + h * d:w + (h + 1) * d], qt)))
        return out

    def stage_free(p, sc):
        rs = rows(p)
        out = []
        for h in range(heads):
            qt, s_raw = sc[h]
            b_row = b_row_all[heads + h:heads + h + 1, rs]
            li_row = gt[h:h + 1, rs]
            u_col = b_col_all[rs, ZS_ML_F + h:ZS_ML_F + h + 1] - zs[rs, ZS_ML_I + h:ZS_ML_I + h + 1]
            d_log = jnp.where(visible, b_row - u_col, NEG_INF)
            dmax = jnp.max(d_log, axis=0, keepdims=True)
            s_t = s_raw * jnp.exp(d_log - dmax)
            vt_aug = jnp.concatenate([v_ref[rs, h * d:(h + 1) * d].T, ones_rows], axis=0)
            kc = qk[rs, w + h * d:w + (h + 1) * d]
            per_chunk = []
            for c in range(2):
                b_end = b_row[:, (c + 1) * ML_CHUNK - 1:(c + 1) * ML_CHUNK]
                w_state = jnp.where(in_chunk[c], b_end - b_row + li_row, NEG_INF)
                wmax = jnp.max(w_state, axis=1, keepdims=True)
                w_row = jnp.exp(w_state - wmax)
                per_chunk.append((b_end, wmax, _dot(vt_aug * w_row, kc)))
            out.append(dict(qt=qt, b_row=b_row, dmax=dmax, intra=_dot(vt_aug, s_t), per_chunk=per_chunk))
        return out

    n_pairs = tb // pair
    sc = {0: stage_scores(0)}
    if n_pairs > 1:
        sc[1] = stage_scores(1)
    free = {0: stage_free(0, sc.pop(0))}
    for p in range(n_pairs):
        rs = rows(p)
        if p + 2 < n_pairs:
            sc[p + 2] = stage_scores(p + 2)
        if p + 1 < n_pairs:
            free[p + 1] = stage_free(p + 1, sc.pop(p + 1))
        cur = free.pop(p)
        for h in range(heads):
            cs = slice(h * d, (h + 1) * d)
            f = cur[h]
            num, m_tok = None, None
            for c in range(2):
                b_end, wmax, upd = f["per_chunk"][c]
                state, m_prev = st_scr[h], m_scr[h]
                inter = _dot(state, f["qt"])
                inter_log = f["b_row"] + m_prev
                m_t = jnp.maximum(inter_log, f["dmax"])
                num_c = jnp.exp(inter_log - m_t) * inter + jnp.exp(f["dmax"] - m_t) * f["intra"]
                num = num_c if c == 0 else jnp.where(in_chunk[0], num, num_c)
                m_tok = m_t if c == 0 else jnp.where(in_chunk[0], m_tok, m_t)
                m_new = jnp.maximum(b_end + m_prev, wmax)
                st_scr[h] = jnp.exp(b_end + m_prev - m_new) * state + jnp.exp(wmax - m_new) * upd
                m_scr[h] = m_new
            qn = num[d:d + 1]
            hcell = num[0:d] / jnp.maximum(jnp.abs(qn), jnp.exp(-m_tok))
            hn = hcell * lax.rsqrt(jnp.mean(hcell * hcell, axis=0, keepdims=True) + EPS)
            y = hn.T * ng_ref[:, cs] * _sigmoid(o_ref[rs, cs])
            y_ref[rs, cs] = y.astype(y_ref.dtype)


def mlstm(z, zs, gt, conv_w, norm_gain, *, heads, tb, col0, zs_block):
    t = z.shape[0]
    w = heads * HEAD_DIM
    cb = col0 // w
    return pl.pallas_call(
        functools.partial(_mlstm_kernel, heads=heads, tb=tb),
        grid=(t // tb,),
        in_specs=[pl.BlockSpec((tb, w), functools.partial(lambda i, c: (i, c), c=cb + c)) for c in range(4)] + [
            pl.BlockSpec((tb, 128), lambda i: (i, zs_block)),
            pl.BlockSpec((8, tb), lambda i: (0, i)),
            pl.BlockSpec((ML_CONV, 2 * w), lambda i: (0, 0)),
            pl.BlockSpec((1, w), lambda i: (0, 0)),
        ],
        out_specs=pl.BlockSpec((tb, w), lambda i: (i, 0)),
        out_shape=jax.ShapeDtypeStruct((t, w), MXU_DTYPE),
        scratch_shapes=[
            pltpu.VMEM((tb + ML_TAIL, 2 * w), F32),
            pltpu.VMEM((heads, ML_AUG, HEAD_DIM), F32),
            pltpu.VMEM((heads, 1, 1), F32),
        ],
        compiler_params=_params("arbitrary"),
        name="mlstm",
    )(z, z, z, z, zs, gt, conv_w, norm_gain.reshape(1, w))


NSA_GROUP = 4
CMP_LEN = 32
CMP_STRIDE = 16
SEL_BLOCK = 64
N_SELECT = 16
WINDOW = 512
FORCE_BONUS = 1e4
SEL_LANES = 128


def _rope(x, cos_t, sin_t):
    return x * cos_t + pltpu.roll(x, HEAD_DIM // 2, axis=1) * sin_t


def _nsa_prep_kernel(q_ref, ks_ref, vs_ref, kw_ref, vw_ref, cos_ref, sin_ref, et_ref, qg_ref, kg_ref,
                     qh_ref, ksa_ref, vso_ref, kwr_ref, vwo_ref, *, heads, kv_heads):
    d = HEAD_DIM
    cos_t, sin_t = cos_ref[...], sin_ref[...]
    for h in range(heads):
        cs = slice(h * d, (h + 1) * d)
        q = _rope(_rms(q_ref[:, cs], qg_ref[...]), cos_t, sin_t) * (d ** -0.5 * LOG2_E)
        qh_ref[cs, :] = q.T.astype(qh_ref.dtype)
    for g in range(kv_heads):
        cs = slice(g * d, (g + 1) * d)
        ksa_ref[g, :, 0:d] = _rope(_rms(ks_ref[:, cs], kg_ref[1:2]), cos_t, sin_t).astype(ksa_ref.dtype)
        ksa_ref[g, :, d:2 * d] = et_ref[...]
        kwr_ref[g] = _rope(_rms(kw_ref[:, cs], kg_ref[2:3]), cos_t, sin_t).astype(kwr_ref.dtype)
        vso_ref[g] = vs_ref[:, cs].T.astype(vso_ref.dtype)
        vwo_ref[g] = vw_ref[:, cs].T.astype(vwo_ref.dtype)


def nsa_prep(z, cos_t, sin_t, et, q_gain, k_gains, *, heads, kv_heads, tq, col_q):
    t = z.shape[0]
    d = HEAD_DIM
    kvw = kv_heads * d
    qb = col_q // (heads * d)
    kb = (col_q + heads * d) // kvw

    def zcol(width, c):
        return pl.BlockSpec((tq, width), functools.partial(lambda i, c: (i, c), c=c))

    k_out = pl.BlockSpec((kv_heads, tq, d), lambda i: (0, i, 0))
    k_shape = jax.ShapeDtypeStruct((kv_heads, t, d), MXU_DTYPE)
    vt_out = pl.BlockSpec((kv_heads, d, tq), lambda i: (0, 0, i))
    vt_shape = jax.ShapeDtypeStruct((kv_heads, d, t), MXU_DTYPE)
    return pl.pallas_call(
        functools.partial(_nsa_prep_kernel, heads=heads, kv_heads=kv_heads),
        grid=(t // tq,),
        in_specs=[zcol(heads * d, qb), zcol(kvw, kb + 2), zcol(kvw, kb + 3), zcol(kvw, kb + 4), zcol(kvw, kb + 5),
                  pl.BlockSpec((tq, d), lambda i: (i, 0)), pl.BlockSpec((tq, d), lambda i: (i, 0)),
                  pl.BlockSpec((tq, SEL_LANES), lambda i: (i, 0)),
                  pl.BlockSpec((1, d), lambda i: (0, 0)), pl.BlockSpec((3, d), lambda i: (0, 0))],
        out_specs=[pl.BlockSpec((heads * d, tq), lambda i: (0, i)),
                   pl.BlockSpec((kv_heads, tq, 2 * d), lambda i: (0, i, 0)), vt_out, k_out, vt_out],
        out_shape=[jax.ShapeDtypeStruct((heads * d, t), MXU_DTYPE),
                   jax.ShapeDtypeStruct((kv_heads, t, 2 * d), MXU_DTYPE), vt_shape, k_shape, vt_shape],
        compiler_params=_params("parallel"),
        name="nsa_prep",
    )(z, z, z, z, z, cos_t, sin_t, et, q_gain.reshape(1, d), k_gains)


def _nsa_compress_kernel(zk_ref, zv_ref, pe_ref, w_ref, kg_ref, cos_ref, sin_ref, kc_ref, vc_ref, *, ncp):
    d = HEAD_DIM
    half = CMP_LEN // 2
    row = lax.broadcasted_iota(jnp.int32, (ncp, d), 0)

    def compress(z_ref, which):
        lo = jnp.zeros((ncp, d), F32)
        hi = jnp.zeros((ncp, d), F32)
        for l in range(half):
            xl = z_ref[pl.ds(l, ncp, stride=CMP_STRIDE), :]
            lo = lo + _dot(xl + pe_ref[which, l:l + 1, :], w_ref[which, l])
            hi = hi + _dot(xl + pe_ref[which, half + l:half + l + 1, :], w_ref[which, half + l])
        out = lo + pltpu.roll(hi, ncp - 1, axis=0)
        return jnp.where(row < ncp - 1, out, 0.0)

    kc_ref[0] = _rope(_rms(compress(zk_ref, 0), kg_ref[0:1]), cos_ref[...], sin_ref[...]).astype(kc_ref.dtype)
    vc_ref[0] = compress(zv_ref, 1).T.astype(vc_ref.dtype)


def nsa_compress(z, cmp_pos, cmp_w, k_gains, cos_c, sin_c, *, kv_heads, col_kc):
    t = z.shape[0]
    d = HEAD_DIM
    ncp = t // CMP_STRIDE
    kb = col_kc // d
    out_spec = pl.BlockSpec((1, ncp, d), lambda g: (g, 0, 0))
    out_shape = jax.ShapeDtypeStruct((kv_heads, ncp, d), MXU_DTYPE)
    return pl.pallas_call(
        functools.partial(_nsa_compress_kernel, ncp=ncp),
        grid=(kv_heads,),
        in_specs=[pl.BlockSpec((t, d), lambda g: (0, kb + g)),
                  pl.BlockSpec((t, d), lambda g: (0, kb + kv_heads + g)),
                  pl.BlockSpec((2, CMP_LEN, d), lambda g: (0, 0, 0)),
                  pl.BlockSpec((2, CMP_LEN, d, d), lambda g: (0, 0, 0, 0)),
                  pl.BlockSpec((3, d), lambda g: (0, 0)),
                  pl.BlockSpec((ncp, d), lambda g: (0, 0)), pl.BlockSpec((ncp, d), lambda g: (0, 0))],
        out_specs=[out_spec, pl.BlockSpec((1, d, ncp), lambda g: (g, 0, 0))],
        out_shape=[out_shape, jax.ShapeDtypeStruct((kv_heads, d, ncp), MXU_DTYPE)],
        compiler_params=_params("parallel"),
        name="nsa_compress",
    )(z, z, cmp_pos, cmp_w, k_gains, cos_c, sin_c)


def _nsa_cmp_kernel(qt_ref, kc_ref, vct_ref, ovt_ref, ocmp_ref, mb_ref, *, tq, grp, ncp):
    d = HEAD_DIM
    qi = pl.program_id(1)
    cmp_end = lax.broadcasted_iota(jnp.int32, (ncp, tq), 0) * CMP_STRIDE + (CMP_LEN - 1)
    vis = cmp_end <= qi * tq + lax.broadcasted_iota(jnp.int32, (ncp, tq), 1)
    psum = None

    def scores(j):
        return jnp.dot(kc_ref[0], qt_ref[j * d:(j + 1) * d, :], preferred_element_type=F32)

    s_next = scores(0)
    for j in range(grp):
        s = s_next
        if j + 1 < grp:
            s_next = scores(j + 1)
        sm = jnp.where(vis, s, NEG_INF)
        p = jnp.where(vis, jnp.exp2(sm - jnp.max(sm, axis=0, keepdims=True)), 0.0)
        l = jnp.sum(p, axis=0, keepdims=True)
        p = p * (1.0 / jnp.where(l > 0.0, l, 1.0))
        ocmp_ref[j * d:(j + 1) * d, :] = _dot(vct_ref[0], p)
        psum = p if psum is None else psum + p
    hi = psum.astype(MXU_DTYPE)
    lo = (psum - hi.astype(F32)).astype(MXU_DTYPE)
    imp = (jnp.dot(ovt_ref[...], hi, preferred_element_type=F32)
           + jnp.dot(ovt_ref[...], lo, preferred_element_type=F32))
    blk = lax.broadcasted_iota(jnp.int32, (SEL_LANES, tq), 0)
    cur = (qi * tq + lax.broadcasted_iota(jnp.int32, (SEL_LANES, tq), 1)) // SEL_BLOCK
    forced = (blk == 0) | (blk == cur) | (blk == cur - 1)
    score = jnp.where(blk <= cur, imp + jnp.where(forced, FORCE_BONUS, 0.0), NEG_INF)
    blk_f = blk.astype(F32)
    bias = jnp.full((SEL_LANES, tq), NEG_INF, F32)
    for _ in range(N_SELECT):
        mx = jnp.max(score, axis=0, keepdims=True)
        first = jnp.min(jnp.where(score == mx, blk_f, float(SEL_LANES)), axis=0, keepdims=True)
        pick = blk_f == first
        bias = jnp.where(pick, jnp.where(mx > 0.5 * NEG_INF, 0.0, NEG_INF), bias)
        score = jnp.where(pick, -jnp.inf, score)
    mb_ref[0] = bias.astype(mb_ref.dtype)


def nsa_cmp_select(qt, kc, vct, ovt, *, kv_heads, tq):
    t = qt.shape[1]
    d = HEAD_DIM
    ncp = kc.shape[1]
    gw = NSA_GROUP * d
    return pl.pallas_call(
        functools.partial(_nsa_cmp_kernel, tq=tq, grp=NSA_GROUP, ncp=ncp),
        grid=(kv_heads, t // tq),
        in_specs=[pl.BlockSpec((gw, tq), lambda g, i: (g, i)),
                  pl.BlockSpec((1, ncp, d), lambda g, i: (g, 0, 0)),
                  pl.BlockSpec((1, d, ncp), lambda g, i: (g, 0, 0)),
                  pl.BlockSpec((SEL_LANES, ncp), lambda g, i: (0, 0))],
        out_specs=[pl.BlockSpec((gw, tq), lambda g, i: (g, i)),
                   pl.BlockSpec((1, SEL_LANES, tq), lambda g, i: (g, 0, i))],
        out_shape=[jax.ShapeDtypeStruct((kv_heads * gw, t), F32),
                   jax.ShapeDtypeStruct((kv_heads, SEL_LANES, t), MXU_DTYPE)],
        compiler_params=_params("parallel", "parallel"),
        name="nsa_cmp_select",
    )(qt, kc, vct, ovt)


def _nsa_sel_kernel(qi_tab, kj_tab, qt_ref, mb_ref, ka_ref, vt_ref, o_ref, qa_scr, m_scr, l_scr, acc_scr,
                    *, tq, tk, grp, hpu):
    d = HEAD_DIM
    step = pl.program_id(1)
    qi, kj = qi_tab[step], kj_tab[step]
    last = (qi * tq + tq - 1) // tk
    units = grp // hpu
    uw = hpu * tq

    @pl.when(kj == 0)
    def _():
        for j in range(grp):
            u, ls = j // hpu, slice((j % hpu) * tq, (j % hpu + 1) * tq)
            qa_scr[u, 0:d, ls] = qt_ref[j * d:(j + 1) * d, :]
            qa_scr[u, d:2 * d, ls] = mb_ref[0]
        m_scr[...] = jnp.full_like(m_scr, NEG_INF)
        l_scr[...] = jnp.zeros_like(l_scr)
        acc_scr[...] = jnp.zeros_like(acc_scr)

    def update(diagonal):
        if diagonal:
            causal = (kj * tk + lax.broadcasted_iota(jnp.int32, (tk, uw), 0)
                      <= qi * tq + lax.broadcasted_iota(jnp.int32, (tk, uw), 1) % tq)

        def scores(u):
            return jnp.dot(ka_ref[0], qa_scr[u], preferred_element_type=F32)

        s_next = scores(0)
        for u in range(units):
            s = s_next
            if u + 1 < units:
                s_next = scores(u + 1)
            if diagonal:
                s = jnp.where(causal, s, NEG_INF)
            m_prev = m_scr[u]
            m_new = jnp.maximum(m_prev, jnp.max(s, axis=0, keepdims=True))
            alpha = jnp.exp2(m_prev - m_new)
            p = jnp.exp2(s - m_new)
            l_scr[u] = alpha * l_scr[u] + jnp.sum(p, axis=0, keepdims=True)
            acc_scr[u] = alpha * acc_scr[u] + _dot(vt_ref[0], p)
            m_scr[u] = m_new

    @pl.when(kj < last)
    def _():
        update(False)

    @pl.when(kj == last)
    def _():
        update(True)
        for j in range(grp):
            u, ls = j // hpu, slice((j % hpu) * tq, (j % hpu + 1) * tq)
            o_ref[j * d:(j + 1) * d, :] = acc_scr[u, :, ls] * (1.0 / l_scr[u, :, ls])


def nsa_sel_attn(qt, mb, ks_aug, vst, *, kv_heads, tq, tk, hpu):
    t = qt.shape[1]
    d = HEAD_DIM
    gw = NSA_GROUP * d
    pairs = [(qi, kj) for qi in range(t // tq) for kj in range((qi * tq + tq - 1) // tk + 1)]
    qi_tab = jnp.asarray(np.array([p[0] for p in pairs], np.int32))
    kj_tab = jnp.asarray(np.array([p[1] for p in pairs], np.int32))
    units, uw = NSA_GROUP // hpu, hpu * tq
    grid_spec = pltpu.PrefetchScalarGridSpec(
        num_scalar_prefetch=2,
        grid=(kv_heads, len(pairs)),
        in_specs=[pl.BlockSpec((gw, tq), lambda g, s, qt, kt: (g, qt[s])),
                  pl.BlockSpec((1, SEL_LANES, tq), lambda g, s, qt, kt: (g, 0, qt[s])),
                  pl.BlockSpec((1, tk, 2 * d), lambda g, s, qt, kt: (g, kt[s], 0)),
                  pl.BlockSpec((1, d, tk), lambda g, s, qt, kt: (g, 0, kt[s]))],
        out_specs=pl.BlockSpec((gw, tq), lambda g, s, qt, kt: (g, qt[s])),
        scratch_shapes=[pltpu.VMEM((units, 2 * d, uw), MXU_DTYPE), pltpu.VMEM((units, 1, uw), F32),
                        pltpu.VMEM((units, 1, uw), F32), pltpu.VMEM((units, d, uw), F32)],
    )
    return pl.pallas_call(
        functools.partial(_nsa_sel_kernel, tq=tq, tk=tk, grp=NSA_GROUP, hpu=hpu),
        grid_spec=grid_spec,
        out_shape=jax.ShapeDtypeStruct((kv_heads * gw, t), F32),
        compiler_params=_params("parallel", "arbitrary"),
        name="nsa_sel_attn",
    )(qi_tab, kj_tab, qt, mb, ks_aug, vst)


def _nsa_win_kernel(qt_ref, k0_ref, k1_ref, k2_ref, v0_ref, v1_ref, v2_ref, ocmp_ref, osel_ref, zg_ref, y_ref,
                    *, tq, grp):
    d = HEAD_DIM
    qi = pl.program_id(1)
    k = jnp.concatenate([k0_ref[0], k1_ref[0], k2_ref[0]], axis=0)
    vt = jnp.concatenate([v0_ref[0], v1_ref[0], v2_ref[0]], axis=1)
    kpos = (qi - 2) * tq + lax.broadcasted_iota(jnp.int32, (3 * tq, tq), 0)
    rel = qi * tq + lax.broadcasted_iota(jnp.int32, (3 * tq, tq), 1) - kpos
    ok = (rel >= 0) & (rel < WINDOW) & (kpos >= 0)
    gates_t = _sigmoid(zg_ref[...]).T

    def scores(j):
        return jnp.dot(k, qt_ref[j * d:(j + 1) * d, :], preferred_element_type=F32)

    s_next = scores(0)
    for j in range(grp):
        rs = slice(j * d, (j + 1) * d)
        s = s_next
        if j + 1 < grp:
            s_next = scores(j + 1)
        sm = jnp.where(ok, s, NEG_INF)
        p = jnp.where(ok, jnp.exp2(sm - jnp.max(sm, axis=0, keepdims=True)), 0.0)
        o = _dot(vt, p) * (1.0 / jnp.sum(p, axis=0, keepdims=True))
        y = (gates_t[3 * j:3 * j + 1] * ocmp_ref[rs, :] + gates_t[3 * j + 1:3 * j + 2] * osel_ref[rs, :]
             + gates_t[3 * j + 2:3 * j + 3] * o)
        y_ref[:, rs] = y.T.astype(y_ref.dtype)


def nsa_win_combine(qt, kw, vwt, o_cmp, o_sel, zs, *, kv_heads, tq, zs_block0):
    t = qt.shape[1]
    d = HEAD_DIM
    gw = NSA_GROUP * d
    k_specs = [pl.BlockSpec((1, tq, d), functools.partial(lambda g, i, b: (g, jnp.maximum(i - b, 0), 0), b=b))
               for b in (2, 1, 0)]
    v_specs = [pl.BlockSpec((1, d, tq), functools.partial(lambda g, i, b: (g, 0, jnp.maximum(i - b, 0)), b=b))
               for b in (2, 1, 0)]
    tspec = pl.BlockSpec((gw, tq), lambda g, i: (g, i))
    return pl.pallas_call(
        functools.partial(_nsa_win_kernel, tq=tq, grp=NSA_GROUP),
        grid=(kv_heads, t // tq),
        in_specs=[tspec] + k_specs + v_specs + [tspec, tspec,
                                                pl.BlockSpec((tq, 128), lambda g, i: (i, zs_block0 + g))],
        out_specs=pl.BlockSpec((tq, gw), lambda g, i: (i, g)),
        out_shape=jax.ShapeDtypeStruct((t, kv_heads * gw), MXU_DTYPE),
        compiler_params=_params("parallel", "parallel"),
        name="nsa_win_combine",
    )(qt, kw, kw, kw, vwt, vwt, vwt, o_cmp, o_sel, zs)


def _split_in_proj(w_in, b_in, sizes):
    starts = np.concatenate([[0], np.cumsum(sizes)])
    i_gate, i_mli, i_mlf = 11, 16, 17
    g0, g1 = starts[i_gate], starts[i_gate + 1]
    half = (g1 - g0) // 2

    def pad(a, width):
        return jnp.pad(a, [(0, 0)] * (a.ndim - 1) + [(0, width - a.shape[-1])])

    def part_b(a):
        return jnp.concatenate([a[..., g1:starts[i_mli]], pad(a[..., g0:g0 + half], 128),
                                pad(a[..., g0 + half:g1], 128),
                                pad(a[..., starts[i_mli]:starts[i_mlf + 1]], 128)], axis=-1)

    return g0, part_b(w_in), part_b(b_in)


def kernel(x, mem, norm_mix, w_in, b_in, hgrn_lb_logits, hgrn_norm, nsa_q_norm, nsa_k_norm, nsa_cmp_pos, nsa_cmp_w, mlstm_conv, mlstm_norm, w_out, norm_xattn, norm_mem, xa_wq, xa_wk, xa_wv, xa_wo, xa_q_norm, xa_k_norm, norm_mlp, mlp_w1, mlp_w2):
    _, t, d_model = x.shape
    depth = w_in.shape[0]
    d = HEAD_DIM
    hg_heads = ml_heads = d_model // (4 * d)
    nsa_heads = d_model // (2 * d)
    kv_heads = nsa_heads // NSA_GROUP
    hw, nw, kvw = hg_heads * d, nsa_heads * d, kv_heads * d
    sizes = (hw,) * 4 + (nw,) + (kvw,) * 6 + (3 * nsa_heads,) + (hw,) * 4 + (ml_heads, ml_heads)
    col_nsa_q = 4 * hw
    col_nsa_kc = col_nsa_q + nw
    xa_heads = 4
    bf = MXU_DTYPE

    half = d // 2
    inv_freq = ROPE_THETA ** (-jnp.arange(half, dtype=F32) / half)

    def rope_tables(pos):
        ang = pos[:, None] * inv_freq[None, :]
        cos, sin = jnp.cos(ang), jnp.sin(ang)
        return jnp.concatenate([cos, cos], axis=-1), jnp.concatenate([-sin, sin], axis=-1)

    cos_t, sin_t = rope_tables(jnp.arange(t, dtype=F32))
    ncp = t // CMP_STRIDE
    n_cmp, n_sel = ncp - 1, t // SEL_BLOCK
    cos_c, sin_c = rope_tables(jnp.arange(ncp, dtype=F32) * CMP_STRIDE + (CMP_LEN - 1))
    et = jnp.asarray(np.arange(t)[:, None] // SEL_BLOCK == np.arange(SEL_LANES)[None, :], bf)
    c_start = np.arange(ncp)[:, None] * CMP_STRIDE
    s_start = np.arange(SEL_LANES)[None, :] * SEL_BLOCK
    overlap = ((c_start < s_start + SEL_BLOCK) & (c_start + CMP_LEN > s_start)
               & (np.arange(ncp)[:, None] < n_cmp) & (np.arange(SEL_LANES)[None, :] < n_sel))
    ovt = jnp.asarray(overlap.T, bf)

    w_in_c = w_in.astype(bf)
    n_a, w_b, b_b = _split_in_proj(w_in_c, b_in, sizes)
    w_out_c, wq_c, wk_c, wv_c, wo_c = (a.astype(bf) for a in (w_out, xa_wq, xa_wk, xa_wv, xa_wo))
    w1_c, w2_c, cmp_w_c = mlp_w1.astype(bf), mlp_w2.astype(bf), nsa_cmp_w.astype(bf)

    h = x.reshape(t, d_model)
    mem2 = mem.reshape(mem.shape[1], d_model)
    hn = rms_cast(h, norm_mix[0], tm=512)
    for layer in range(depth):
        z = matmul_cols(hn, w_in_c, b_in, layer=layer, n_tiles=3, tm=512, tn=n_a // 3)
        zb = matmul_cols(hn, w_b, b_b, layer=layer, n_tiles=1, tm=512, tn=w_b.shape[2])
        gate_block = 4 * hw // 128
        gt = zb[:, (gate_block + 2) * 128:(gate_block + 2) * 128 + 8].T

        y_hg = hgrn2(z, hgrn_lb_logits, hgrn_norm[layer], layer=layer, heads=hg_heads, tb=512)
        y_ml = mlstm(zb, zb, gt, mlstm_conv[layer], mlstm_norm[layer], heads=ml_heads, tb=512, col0=0,
                     zs_block=gate_block + 2)

        qt, ks_aug, vst, kw, vwt = nsa_prep(z, cos_t, sin_t, et, nsa_q_norm[layer], nsa_k_norm[layer],
                                            heads=nsa_heads, kv_heads=kv_heads, tq=512, col_q=col_nsa_q)
        kc, vct = nsa_compress(z, nsa_cmp_pos[layer], cmp_w_c[layer], nsa_k_norm[layer],
                               cos_c, sin_c, kv_heads=kv_heads, col_kc=col_nsa_kc)
        o_cmp, mb = nsa_cmp_select(qt, kc, vct, ovt, kv_heads=kv_heads, tq=256)
        o_sel = nsa_sel_attn(qt, mb, ks_aug, vst, kv_heads=kv_heads, tq=512, tk=512, hpu=1)
        y_ns = nsa_win_combine(qt, kw, vwt, o_cmp, o_sel, zb, kv_heads=kv_heads, tq=256, zs_block0=gate_block)

        kb = hw
        terms = [(y_hg, 0, 0)] + [(y_ns, c, 1 + c) for c in range(nw // kb)] + [(y_ml, 0, 1 + nw // kb)]
        h, hn = matmul_residual(terms, w_out_c, h, norm_xattn[layer], layer=layer, kb=kb, tm=512)

        k_mem, v_mem = xa_kv(mem2, norm_mem[layer], wk_c, wv_c, xa_k_norm[layer], layer=layer, heads=xa_heads)
        q_x = matmul_cols(hn, wq_c, None, layer=layer, n_tiles=1, tm=512, tn=d_model)
        o_x = xa_attn(q_x, xa_q_norm[layer], k_mem, v_mem, heads=xa_heads, tm=512)
        h = matmul_residual([(o_x, c, c) for c in range(d_model // kb)], wo_c, h, None, layer=layer, kb=kb, tm=512)

        if layer + 1 < depth:
            h, hn = mlp(h, norm_mlp[layer], w1_c, w2_c, norm_mix[layer + 1], layer=layer, tm=512, tf=1024)
        else:
            h = mlp(h, norm_mlp[layer], w1_c, w2_c, None, layer=layer, tm=512, tf=1024)
    return h.reshape(x.shape)
```

```python
import functools

import jax
import jax.numpy as jnp
import numpy as np
from jax import lax
from jax.experimental import pallas as pl
from jax.experimental.pallas import tpu as pltpu

F32 = jnp.float32
MXU_DTYPE = jnp.bfloat16

EPS = 1e-6
NEG_INF = -1e30
LOG2_E = 1.4426950408889634
HEAD_DIM = 128
ROPE_THETA = 10000.0

V7X_VMEM_LIMIT_BYTES = 56 * 1024 * 1024


def _params(*semantics):
    return pltpu.CompilerParams(dimension_semantics=semantics, vmem_limit_bytes=V7X_VMEM_LIMIT_BYTES)


def _rms(x, gain):
    return x * lax.rsqrt(jnp.mean(x * x, axis=-1, keepdims=True) + EPS) * gain


def _dot(a, b):
    return jnp.dot(a.astype(MXU_DTYPE), b.astype(MXU_DTYPE), preferred_element_type=F32)


def _dot_nt(a, b):
    return lax.dot_general(a.astype(MXU_DTYPE), b.astype(MXU_DTYPE), (((1,), (1,)), ((), ())),
                           preferred_element_type=F32)


def _dot_tn(a, b):
    return lax.dot_general(a.astype(MXU_DTYPE), b.astype(MXU_DTYPE), (((0,), (0,)), ((), ())),
                           preferred_element_type=F32)


def _rms_cast_kernel(x_ref, g_ref, o_ref):
    o_ref[...] = _rms(x_ref[...], g_ref[...]).astype(o_ref.dtype)


def rms_cast(x, gain, *, tm):
    m, d = x.shape
    return pl.pallas_call(
        _rms_cast_kernel,
        grid=(m // tm,),
        in_specs=[pl.BlockSpec((tm, d), lambda i: (i, 0)), pl.BlockSpec((1, d), lambda i: (0, 0))],
        out_specs=pl.BlockSpec((tm, d), lambda i: (i, 0)),
        out_shape=jax.ShapeDtypeStruct((m, d), MXU_DTYPE),
        compiler_params=_params("parallel"),
        name="rms_cast",
    )(x, gain.reshape(1, d))


def _matmul_bias_kernel(h_ref, w_ref, b_ref, o_ref):
    o_ref[...] = jnp.dot(h_ref[...], w_ref[...], preferred_element_type=F32) + b_ref[...]


def _matmul_kernel(h_ref, w_ref, o_ref):
    o_ref[...] = jnp.dot(h_ref[...], w_ref[...], preferred_element_type=F32)


def matmul_cols(h, w, bias, *, layer, n_tiles, tm, tn):
    m, d = h.shape
    in_specs = [pl.BlockSpec((tm, d), lambda j, i: (i, 0)),
                pl.BlockSpec((None, d, tn), lambda j, i: (layer, 0, j))]
    args = [h, w]
    if bias is not None:
        in_specs.append(pl.BlockSpec((None, 1, tn), lambda j, i: (layer, 0, j)))
        args.append(bias.reshape(bias.shape[0], 1, bias.shape[1]))
    return pl.pallas_call(
        _matmul_kernel if bias is None else _matmul_bias_kernel,
        grid=(n_tiles, m // tm),
        in_specs=in_specs,
        out_specs=pl.BlockSpec((tm, tn), lambda j, i: (i, j)),
        out_shape=jax.ShapeDtypeStruct((m, n_tiles * tn), F32),
        compiler_params=_params("parallel", "parallel"),
        name="matmul_cols",
    )(*args)


def _matmul_res_kernel(*refs, n_terms, with_norm):
    ys, ws = refs[:n_terms], refs[n_terms:2 * n_terms]
    res_ref = refs[2 * n_terms]
    acc = res_ref[...]
    for y_ref, w_ref in zip(ys, ws):
        acc = acc + jnp.dot(y_ref[...].astype(MXU_DTYPE), w_ref[...], preferred_element_type=F32)
    if with_norm:
        g_ref, o_ref, hn_ref = refs[2 * n_terms + 1:]
        hn_ref[...] = _rms(acc, g_ref[...]).astype(hn_ref.dtype)
    else:
        o_ref = refs[2 * n_terms + 1]
    o_ref[...] = acc


def matmul_residual(terms, w, res, next_gain, *, layer, kb, tm):
    m, n = res.shape
    y_specs = [pl.BlockSpec((tm, kb), functools.partial(lambda i, c: (i, c), c=cb)) for _, cb, _ in terms]
    w_specs = [pl.BlockSpec((None, kb, n), functools.partial(lambda i, r: (layer, r, 0), r=rb)) for _, _, rb in terms]
    row_spec = pl.BlockSpec((tm, n), lambda i: (i, 0))
    with_norm = next_gain is not None
    return pl.pallas_call(
        functools.partial(_matmul_res_kernel, n_terms=len(terms), with_norm=with_norm),
        grid=(m // tm,),
        in_specs=y_specs + w_specs + [row_spec] + ([pl.BlockSpec((1, n), lambda i: (0, 0))] if with_norm else []),
        out_specs=[row_spec, row_spec] if with_norm else row_spec,
        out_shape=([jax.ShapeDtypeStruct((m, n), F32), jax.ShapeDtypeStruct((m, n), MXU_DTYPE)] if with_norm
                   else jax.ShapeDtypeStruct((m, n), F32)),
        compiler_params=_params("parallel"),
        name="matmul_residual",
    )(*[t[0] for t in terms], *([w] * len(terms)), res, *([next_gain.reshape(1, n)] if with_norm else []))


def _mlp_kernel(*refs, with_norm):
    if with_norm:
        x_ref, g_ref, w1_ref, w2_ref, gn_ref, o_ref, hn_ref, h_scr = refs
    else:
        x_ref, g_ref, w1_ref, w2_ref, o_ref, h_scr = refs

    @pl.when(pl.program_id(1) == 0)
    def _():
        x = x_ref[...]
        h_scr[...] = _rms(x, g_ref[...]).astype(h_scr.dtype)
        o_ref[...] = x

    u = jnp.dot(h_scr[...], w1_ref[...], preferred_element_type=F32)
    a = jnp.square(jnp.maximum(u, 0.0)).astype(MXU_DTYPE)
    o_ref[...] += jnp.dot(a, w2_ref[...], preferred_element_type=F32)

    if with_norm:
        @pl.when(pl.program_id(1) == pl.num_programs(1) - 1)
        def _():
            hn_ref[...] = _rms(o_ref[...], gn_ref[...]).astype(hn_ref.dtype)


def mlp(x, gain, w1, w2, next_gain, *, layer, tm, tf):
    m, d = x.shape
    ff = w1.shape[2]
    row_spec = pl.BlockSpec((tm, d), lambda i, f: (i, 0))
    gain_spec = pl.BlockSpec((1, d), lambda i, f: (0, 0))
    with_norm = next_gain is not None
    return pl.pallas_call(
        functools.partial(_mlp_kernel, with_norm=with_norm),
        grid=(m // tm, ff // tf),
        in_specs=[row_spec, gain_spec,
                  pl.BlockSpec((None, d, tf), lambda i, f: (layer, 0, f)),
                  pl.BlockSpec((None, tf, d), lambda i, f: (layer, f, 0))] + ([gain_spec] if with_norm else []),
        out_specs=[row_spec, row_spec] if with_norm else row_spec,
        out_shape=([jax.ShapeDtypeStruct((m, d), F32), jax.ShapeDtypeStruct((m, d), MXU_DTYPE)] if with_norm
                   else jax.ShapeDtypeStruct((m, d), F32)),
        scratch_shapes=[pltpu.VMEM((tm, d), MXU_DTYPE)],
        compiler_params=_params("parallel", "arbitrary"),
        name="mlp",
    )(x, gain.reshape(1, d), w1, w2, *([next_gain.reshape(1, d)] if with_norm else []))


def _xa_kv_kernel(mem_ref, g_ref, wk_ref, wv_ref, kg_ref, k_ref, v_ref):
    mn = _rms(mem_ref[...], g_ref[...]).astype(MXU_DTYPE)
    k = jnp.dot(mn, wk_ref[...], preferred_element_type=F32)
    k_ref[...] = _rms(k, kg_ref[...]).astype(k_ref.dtype)
    v_ref[...] = jnp.dot(mn, wv_ref[...], preferred_element_type=F32).astype(v_ref.dtype)


def xa_kv(mem, gain, wk, wv, k_gain, *, layer, heads):
    ml, d = mem.shape
    hd = d // heads
    return pl.pallas_call(
        _xa_kv_kernel,
        grid=(heads,),
        in_specs=[
            pl.BlockSpec((ml, d), lambda h: (0, 0)),
            pl.BlockSpec((1, d), lambda h: (0, 0)),
            pl.BlockSpec((None, d, hd), lambda h: (layer, 0, h)),
            pl.BlockSpec((None, d, hd), lambda h: (layer, 0, h)),
            pl.BlockSpec((1, hd), lambda h: (0, 0)),
        ],
        out_specs=[pl.BlockSpec((ml, hd), lambda h: (0, h))] * 2,
        out_shape=[jax.ShapeDtypeStruct((ml, d), MXU_DTYPE)] * 2,
        compiler_params=_params("parallel"),
        name="xa_kv",
    )(mem, gain.reshape(1, d), wk, wv, k_gain.reshape(1, hd))


def _xa_attn_kernel(q_ref, qg_ref, k_ref, v_ref, o_ref, *, heads, hd):
    scale = hd ** -0.5
    for h in range(heads):
        cs = slice(h * hd, (h + 1) * hd)
        q = _rms(q_ref[:, cs], qg_ref[...]) * scale
        s = _dot_nt(q, k_ref[:, cs])
        p = jnp.exp(s - jnp.max(s, axis=-1, keepdims=True))
        p = p / jnp.sum(p, axis=-1, keepdims=True)
        o_ref[:, cs] = _dot(p, v_ref[:, cs]).astype(o_ref.dtype)


def xa_attn(q, q_gain, k, v, *, heads, tm):
    m, d = q.shape
    ml = k.shape[0]
    hd = d // heads
    return pl.pallas_call(
        functools.partial(_xa_attn_kernel, heads=heads, hd=hd),
        grid=(m // tm,),
        in_specs=[
            pl.BlockSpec((tm, d), lambda i: (i, 0)),
            pl.BlockSpec((1, hd), lambda i: (0, 0)),
            pl.BlockSpec((ml, d), lambda i: (0, 0)),
            pl.BlockSpec((ml, d), lambda i: (0, 0)),
        ],
        out_specs=pl.BlockSpec((tm, d), lambda i: (i, 0)),
        out_shape=jax.ShapeDtypeStruct((m, d), MXU_DTYPE),
        compiler_params=_params("parallel"),
        name="xa_attn",
    )(q, q_gain.reshape(1, hd), k, v)


def _seg_cumsum_rows(x, seg):
    pos = lax.broadcasted_iota(jnp.int32, x.shape, 0) % seg
    k = 1
    while k < seg:
        x = x + jnp.where(pos >= k, pltpu.roll(x, k, axis=0), 0.0)
        k *= 2
    return x


def _seg_cumsum_lanes(x, seg):
    pos = lax.broadcasted_iota(jnp.int32, x.shape, 1) % seg
    k = 1
    while k < seg:
        x = x + jnp.where(pos >= k, pltpu.roll(x, k, axis=1), 0.0)
        k *= 2
    return x


def _sigmoid(x):
    return 1.0 / (1.0 + jnp.exp(-x))


def _log_sigmoid(x):
    return jnp.minimum(x, 0.0) - jnp.log(1.0 + jnp.exp(-jnp.abs(x)))


HG_SUB = 16
HG_MIN_FORGET = 1e-20


def _hgrn2_kernel(q_ref, f_ref, i_ref, g_ref, lbl_ref, ng_ref, y_ref, st_ref, *, layer, heads, tb):
    dk = HEAD_DIM

    @pl.when(pl.program_id(0) == 0)
    def _():
        st_ref[...] = jnp.zeros_like(st_ref)

    logits = lbl_ref[...]
    e = jnp.exp(logits - jnp.max(logits, axis=0, keepdims=True))
    probs = e / jnp.sum(e, axis=0, keepdims=True)
    lb_all = jnp.zeros_like(probs[0:1])
    for l in range(1, layer + 1):
        lb_all = lb_all + probs[l:l + 1]
    half = HG_SUB // 2
    row_half = lax.broadcasted_iota(jnp.int32, (half, dk), 0)

    def body(c, carry):
        r = pl.ds(pl.multiple_of(c * HG_SUB, HG_SUB), HG_SUB)
        pre = []
        for h in range(heads):
            cs = slice(h * dk, (h + 1) * dk)
            lb = lb_all[:, cs]
            q, zf, v = q_ref[r, cs], f_ref[r, cs], i_ref[r, cs]
            forget = lb + (1.0 - lb) * _sigmoid(zf)
            log_f = jnp.log(jnp.maximum(forget, HG_MIN_FORGET))
            key = (1.0 - lb) * _sigmoid(-zf)
            b = _seg_cumsum_rows(log_f, HG_SUB) * LOG2_E
            pre.append((q, v, key, b, b[HG_SUB - 1:HG_SUB]))
        inter = [_dot_nt(q * jnp.exp2(b), st_ref[h]) for h, (q, v, key, b, b_end) in enumerate(pre)]
        upd = [_dot_tn(v, key * jnp.exp2(b_end - b)) for q, v, key, b, b_end in pre]
        for h in range(heads):
            cs = slice(h * dk, (h + 1) * dk)
            q, v, key, b, b_end = pre[h]
            q_lo, q_hi, b_lo, b_hi = q[0:half], q[half:], b[0:half], b[half:]
            o_lo = jnp.zeros((half, dk), F32)
            o_hi = jnp.zeros((half, dk), F32)
            for s in range(HG_SUB):
                bs, ks, vs = b[s:s + 1], key[s:s + 1], v[s:s + 1]
                if s < half:
                    decay = jnp.exp2(jnp.where(row_half >= s, b_lo - bs, NEG_INF))
                    o_lo = o_lo + jnp.sum(q_lo * decay * ks, axis=-1, keepdims=True) * vs
                    decay = jnp.exp2(b_hi - bs)
                else:
                    decay = jnp.exp2(jnp.where(row_half >= s - half, b_hi - bs, NEG_INF))
                o_hi = o_hi + jnp.sum(q_hi * decay * ks, axis=-1, keepdims=True) * vs
            o = jnp.concatenate([o_lo, o_hi], axis=0) + inter[h]
            st_ref[h] = jnp.exp2(b_end) * st_ref[h] + upd[h]
            g = g_ref[r, cs]
            y = _rms(o, ng_ref[:, cs]) * (g * _sigmoid(g))
            y_ref[r, cs] = y.astype(y_ref.dtype)
        return carry

    lax.fori_loop(0, tb // HG_SUB, body, 0)


def hgrn2(z, lb_logits, norm_gain, *, layer, heads, tb):
    t = z.shape[0]
    w = heads * HEAD_DIM
    depth = lb_logits.shape[0]
    return pl.pallas_call(
        functools.partial(_hgrn2_kernel, layer=layer, heads=heads, tb=tb),
        grid=(t // tb,),
        in_specs=[pl.BlockSpec((tb, w), functools.partial(lambda i, c: (i, c), c=c)) for c in range(4)] + [
            pl.BlockSpec((depth, w), lambda i: (0, 0)),
            pl.BlockSpec((1, w), lambda i: (0, 0)),
        ],
        out_specs=pl.BlockSpec((tb, w), lambda i: (i, 0)),
        out_shape=jax.ShapeDtypeStruct((t, w), MXU_DTYPE),
        scratch_shapes=[pltpu.VMEM((heads, HEAD_DIM, HEAD_DIM), F32)],
        compiler_params=_params("arbitrary"),
        name="hgrn2",
    )(z, z, z, z, lb_logits, norm_gain.reshape(1, w))


ML_CHUNK = 64
ML_CONV = 4
ML_TAIL = 8
ML_AUG = HEAD_DIM + 16
ZS_ML_I = 0
ZS_ML_F = 4


def _mlstm_kernel(q_ref, k_ref, v_ref, o_ref, zs_ref, gt_ref, cw_ref, ng_ref, y_ref,
                  conv_scr, st_scr, m_scr, *, heads, tb):
    d = HEAD_DIM
    w = heads * d

    @pl.when(pl.program_id(0) == 0)
    def _():
        conv_scr[0:ML_TAIL, :] = jnp.zeros((ML_TAIL, 2 * w), F32)
        st_scr[...] = jnp.zeros_like(st_scr)
        m_scr[...] = jnp.zeros_like(m_scr)

    conv_scr[ML_TAIL:ML_TAIL + tb, 0:w] = q_ref[...]
    conv_scr[ML_TAIL:ML_TAIL + tb, w:2 * w] = k_ref[...]
    acc = jnp.zeros((tb, 2 * w), F32)
    for j in range(ML_CONV):
        off = ML_TAIL - (ML_CONV - 1) + j
        acc = acc + conv_scr[off:off + tb, :] * cw_ref[j:j + 1, :]
    conv_scr[0:ML_TAIL, :] = conv_scr[tb:tb + ML_TAIL, :]
    qk = acc * _sigmoid(acc)

    pair = 2 * ML_CHUNK
    zs = zs_ref[...]
    b_col_all = _seg_cumsum_rows(_log_sigmoid(zs), ML_CHUNK)
    gt = gt_ref[...]
    b_row_all = _seg_cumsum_lanes(_log_sigmoid(gt), ML_CHUNK)
    s_idx = lax.broadcasted_iota(jnp.int32, (pair, pair), 0)
    t_idx = lax.broadcasted_iota(jnp.int32, (pair, pair), 1)
    visible = (s_idx // ML_CHUNK == t_idx // ML_CHUNK) & (s_idx <= t_idx)
    lane = lax.broadcasted_iota(jnp.int32, (1, pair), 1)
    in_chunk = [lane < ML_CHUNK, lane >= ML_CHUNK]
    ones_rows = (lax.broadcasted_iota(jnp.int32, (ML_AUG - d, pair), 0) == 0).astype(F32)

    def rows(p):
        return slice(p * pair, (p + 1) * pair)

    def stage_scores(p):
        out = []
        for h in range(heads):
            qt = (qk[rows(p), h * d:(h + 1) * d] * (d ** -0.5)).T
            out.append((qt, _dot(qk[rows(p), w + h * d:w + (h + 1) * d], qt)))
        return out

    def stage_free(p, sc):
        rs = rows(p)
        out = []
        for h in range(heads):
            qt, s_raw = sc[h]
            b_row = b_row_all[heads + h:heads + h + 1, rs]
            li_row = gt[h:h + 1, rs]
            u_col = b_col_all[rs, ZS_ML_F + h:ZS_ML_F + h + 1] - zs[rs, ZS_ML_I + h:ZS_ML_I + h + 1]
            d_log = jnp.where(visible, b_row - u_col, NEG_INF)
            dmax = jnp.max(d_log, axis=0, keepdims=True)
            s_t = s_raw * jnp.exp(d_log - dmax)
            vt_aug = jnp.concatenate([v_ref[rs, h * d:(h + 1) * d].T, ones_rows], axis=0)
            kc = qk[rs, w + h * d:w + (h + 1) * d]
            per_chunk = []
            for c in range(2):
                b_end = b_row[:, (c + 1) * ML_CHUNK - 1:(c + 1) * ML_CHUNK]
                w_state = jnp.where(in_chunk[c], b_end - b_row + li_row, NEG_INF)
                wmax = jnp.max(w_state, axis=1, keepdims=True)
                w_row = jnp.exp(w_state - wmax)
                per_chunk.append((b_end, wmax, _dot(vt_aug * w_row, kc)))
            out.append(dict(qt=qt, b_row=b_row, dmax=dmax, intra=_dot(vt_aug, s_t), per_chunk=per_chunk))
        return out

    n_pairs = tb // pair
    sc = {0: stage_scores(0)}
    if n_pairs > 1:
        sc[1] = stage_scores(1)
    free = {0: stage_free(0, sc.pop(0))}
    for p in range(n_pairs):
        rs = rows(p)
        if p + 2 < n_pairs:
            sc[p + 2] = stage_scores(p + 2)
        if p + 1 < n_pairs:
            free[p + 1] = stage_free(p + 1, sc.pop(p + 1))
        cur = free.pop(p)
        for h in range(heads):
            cs = slice(h * d, (h + 1) * d)
            f = cur[h]
            num, m_tok = None, None
            for c in range(2):
                b_end, wmax, upd = f["per_chunk"][c]
                state, m_prev = st_scr[h], m_scr[h]
                inter = _dot(state, f["qt"])
                inter_log = f["b_row"] + m_prev
                m_t = jnp.maximum(inter_log, f["dmax"])
                num_c = jnp.exp(inter_log - m_t) * inter + jnp.exp(f["dmax"] - m_t) * f["intra"]
                num = num_c if c == 0 else jnp.where(in_chunk[0], num, num_c)
                m_tok = m_t if c == 0 else jnp.where(in_chunk[0], m_tok, m_t)
                m_new = jnp.maximum(b_end + m_prev, wmax)
                st_scr[h] = jnp.exp(b_end + m_prev - m_new) * state + jnp.exp(wmax - m_new) * upd
                m_scr[h] = m_new
            qn = num[d:d + 1]
            hcell = num[0:d] / jnp.maximum(jnp.abs(qn), jnp.exp(-m_tok))
            hn = hcell * lax.rsqrt(jnp.mean(hcell * hcell, axis=0, keepdims=True) + EPS)
            y = hn.T * ng_ref[:, cs] * _sigmoid(o_ref[rs, cs])
            y_ref[rs, cs] = y.astype(y_ref.dtype)


def mlstm(z, zs, gt, conv_w, norm_gain, *, heads, tb, col0, zs_block):
    t = z.shape[0]
    w = heads * HEAD_DIM
    cb = col0 // w
    return pl.pallas_call(
        functools.partial(_mlstm_kernel, heads=heads, tb=tb),
        grid=(t // tb,),
        in_specs=[pl.BlockSpec((tb, w), functools.partial(lambda i, c: (i, c), c=cb + c)) for c in range(4)] + [
            pl.BlockSpec((tb, 128), lambda i: (i, zs_block)),
            pl.BlockSpec((8, tb), lambda i: (0, i)),
            pl.BlockSpec((ML_CONV, 2 * w), lambda i: (0, 0)),
            pl.BlockSpec((1, w), lambda i: (0, 0)),
        ],
        out_specs=pl.BlockSpec((tb, w), lambda i: (i, 0)),
        out_shape=jax.ShapeDtypeStruct((t, w), MXU_DTYPE),
        scratch_shapes=[
            pltpu.VMEM((tb + ML_TAIL, 2 * w), F32),
            pltpu.VMEM((heads, ML_AUG, HEAD_DIM), F32),
            pltpu.VMEM((heads, 1, 1), F32),
        ],
        compiler_params=_params("arbitrary"),
        name="mlstm",
    )(z, z, z, z, zs, gt, conv_w, norm_gain.reshape(1, w))


NSA_GROUP = 4
CMP_LEN = 32
CMP_STRIDE = 16
SEL_BLOCK = 64
N_SELECT = 16
WINDOW = 512
FORCE_BONUS = 1e4
SEL_LANES = 128
SEL_LAG_MAX_GAP = 64.0


def _rope(x, cos_t, sin_t):
    return x * cos_t + pltpu.roll(x, HEAD_DIM // 2, axis=1) * sin_t


def _nsa_prep_kernel(q_ref, ks_ref, vs_ref, kw_ref, vw_ref, cos_ref, sin_ref, et_ref, qg_ref, kg_ref,
                     qh_ref, ksa_ref, vso_ref, kwr_ref, vwo_ref, *, heads, kv_heads):
    d = HEAD_DIM
    cos_t, sin_t = cos_ref[...], sin_ref[...]
    for h in range(heads):
        cs = slice(h * d, (h + 1) * d)
        q = _rope(_rms(q_ref[:, cs], qg_ref[...]), cos_t, sin_t) * (d ** -0.5 * LOG2_E)
        qh_ref[cs, :] = q.T.astype(qh_ref.dtype)
    for g in range(kv_heads):
        cs = slice(g * d, (g + 1) * d)
        ksa_ref[g, :, 0:d] = _rope(_rms(ks_ref[:, cs], kg_ref[1:2]), cos_t, sin_t).astype(ksa_ref.dtype)
        ksa_ref[g, :, d:2 * d] = et_ref[...]
        kwr_ref[g] = _rope(_rms(kw_ref[:, cs], kg_ref[2:3]), cos_t, sin_t).astype(kwr_ref.dtype)
        vso_ref[g] = vs_ref[:, cs].T.astype(vso_ref.dtype)
        vwo_ref[g] = vw_ref[:, cs].T.astype(vwo_ref.dtype)


def nsa_prep(z, cos_t, sin_t, et, q_gain, k_gains, *, heads, kv_heads, tq, col_q):
    t = z.shape[0]
    d = HEAD_DIM
    kvw = kv_heads * d
    qb = col_q // (heads * d)
    kb = (col_q + heads * d) // kvw

    def zcol(width, c):
        return pl.BlockSpec((tq, width), functools.partial(lambda i, c: (i, c), c=c))

    k_out = pl.BlockSpec((kv_heads, tq, d), lambda i: (0, i, 0))
    k_shape = jax.ShapeDtypeStruct((kv_heads, t, d), MXU_DTYPE)
    vt_out = pl.BlockSpec((kv_heads, d, tq), lambda i: (0, 0, i))
    vt_shape = jax.ShapeDtypeStruct((kv_heads, d, t), MXU_DTYPE)
    return pl.pallas_call(
        functools.partial(_nsa_prep_kernel, heads=heads, kv_heads=kv_heads),
        grid=(t // tq,),
        in_specs=[zcol(heads * d, qb), zcol(kvw, kb + 2), zcol(kvw, kb + 3), zcol(kvw, kb + 4), zcol(kvw, kb + 5),
                  pl.BlockSpec((tq, d), lambda i: (i, 0)), pl.BlockSpec((tq, d), lambda i: (i, 0)),
                  pl.BlockSpec((tq, SEL_LANES), lambda i: (i, 0)),
                  pl.BlockSpec((1, d), lambda i: (0, 0)), pl.BlockSpec((3, d), lambda i: (0, 0))],
        out_specs=[pl.BlockSpec((heads * d, tq), lambda i: (0, i)),
                   pl.BlockSpec((kv_heads, tq, 2 * d), lambda i: (0, i, 0)), vt_out, k_out, vt_out],
        out_shape=[jax.ShapeDtypeStruct((heads * d, t), MXU_DTYPE),
                   jax.ShapeDtypeStruct((kv_heads, t, 2 * d), MXU_DTYPE), vt_shape, k_shape, vt_shape],
        compiler_params=_params("parallel"),
        name="nsa_prep",
    )(z, z, z, z, z, cos_t, sin_t, et, q_gain.reshape(1, d), k_gains)


def _nsa_compress_kernel(zk_ref, zv_ref, pe_ref, w_ref, kg_ref, cos_ref, sin_ref, kc_ref, vc_ref, *, ncp):
    d = HEAD_DIM
    half = CMP_LEN // 2
    row = lax.broadcasted_iota(jnp.int32, (ncp, d), 0)

    def compress(z_ref, which):
        lo = jnp.zeros((ncp, d), F32)
        hi = jnp.zeros((ncp, d), F32)
        for l in range(half):
            xl = z_ref[pl.ds(l, ncp, stride=CMP_STRIDE), :]
            lo = lo + _dot(xl + pe_ref[which, l:l + 1, :], w_ref[which, l])
            hi = hi + _dot(xl + pe_ref[which, half + l:half + l + 1, :], w_ref[which, half + l])
        out = lo + pltpu.roll(hi, ncp - 1, axis=0)
        return jnp.where(row < ncp - 1, out, 0.0)

    kc_ref[0] = _rope(_rms(compress(zk_ref, 0), kg_ref[0:1]), cos_ref[...], sin_ref[...]).astype(kc_ref.dtype)
    vc_ref[0] = compress(zv_ref, 1).T.astype(vc_ref.dtype)


def nsa_compress(z, cmp_pos, cmp_w, k_gains, cos_c, sin_c, *, kv_heads, col_kc):
    t = z.shape[0]
    d = HEAD_DIM
    ncp = t // CMP_STRIDE
    kb = col_kc // d
    out_spec = pl.BlockSpec((1, ncp, d), lambda g: (g, 0, 0))
    out_shape = jax.ShapeDtypeStruct((kv_heads, ncp, d), MXU_DTYPE)
    return pl.pallas_call(
        functools.partial(_nsa_compress_kernel, ncp=ncp),
        grid=(kv_heads,),
        in_specs=[pl.BlockSpec((t, d), lambda g: (0, kb + g)),
                  pl.BlockSpec((t, d), lambda g: (0, kb + kv_heads + g)),
                  pl.BlockSpec((2, CMP_LEN, d), lambda g: (0, 0, 0)),
                  pl.BlockSpec((2, CMP_LEN, d, d), lambda g: (0, 0, 0, 0)),
                  pl.BlockSpec((3, d), lambda g: (0, 0)),
                  pl.BlockSpec((ncp, d), lambda g: (0, 0)), pl.BlockSpec((ncp, d), lambda g: (0, 0))],
        out_specs=[out_spec, pl.BlockSpec((1, d, ncp), lambda g: (g, 0, 0))],
        out_shape=[out_shape, jax.ShapeDtypeStruct((kv_heads, d, ncp), MXU_DTYPE)],
        compiler_params=_params("parallel"),
        name="nsa_compress",
    )(z, z, cmp_pos, cmp_w, k_gains, cos_c, sin_c)


def _nsa_cmp_kernel(qt_ref, kc_ref, vct_ref, ovt_ref, ocmp_ref, mb_ref, *, tq, grp, ncp):
    d = HEAD_DIM
    qi = pl.program_id(1)
    cmp_end = lax.broadcasted_iota(jnp.int32, (ncp, tq), 0) * CMP_STRIDE + (CMP_LEN - 1)
    vis = cmp_end <= qi * tq + lax.broadcasted_iota(jnp.int32, (ncp, tq), 1)
    psum = None

    def scores(j):
        return jnp.dot(kc_ref[0], qt_ref[j * d:(j + 1) * d, :], preferred_element_type=F32)

    s_next = scores(0)
    for j in range(grp):
        s = s_next
        if j + 1 < grp:
            s_next = scores(j + 1)
        sm = jnp.where(vis, s, NEG_INF)
        p = jnp.where(vis, jnp.exp2(sm - jnp.max(sm, axis=0, keepdims=True)), 0.0)
        l = jnp.sum(p, axis=0, keepdims=True)
        p = p * (1.0 / jnp.where(l > 0.0, l, 1.0))
        ocmp_ref[j * d:(j + 1) * d, :] = _dot(vct_ref[0], p)
        psum = p if psum is None else psum + p
    hi = psum.astype(MXU_DTYPE)
    lo = (psum - hi.astype(F32)).astype(MXU_DTYPE)
    imp = (jnp.dot(ovt_ref[...], hi, preferred_element_type=F32)
           + jnp.dot(ovt_ref[...], lo, preferred_element_type=F32))
    blk = lax.broadcasted_iota(jnp.int32, (SEL_LANES, tq), 0)
    cur = (qi * tq + lax.broadcasted_iota(jnp.int32, (SEL_LANES, tq), 1)) // SEL_BLOCK
    forced = (blk == 0) | (blk == cur) | (blk == cur - 1)
    score = jnp.where(blk <= cur, imp + jnp.where(forced, FORCE_BONUS, 0.0), NEG_INF)
    blk_f = blk.astype(F32)
    bias = jnp.full((SEL_LANES, tq), NEG_INF, F32)
    for _ in range(N_SELECT):
        mx = jnp.max(score, axis=0, keepdims=True)
        first = jnp.min(jnp.where(score == mx, blk_f, float(SEL_LANES)), axis=0, keepdims=True)
        pick = blk_f == first
        bias = jnp.where(pick, jnp.where(mx > 0.5 * NEG_INF, 0.0, NEG_INF), bias)
        score = jnp.where(pick, -jnp.inf, score)
    mb_ref[0] = bias.astype(mb_ref.dtype)


def nsa_cmp_select(qt, kc, vct, ovt, *, kv_heads, tq):
    t = qt.shape[1]
    d = HEAD_DIM
    ncp = kc.shape[1]
    gw = NSA_GROUP * d
    return pl.pallas_call(
        functools.partial(_nsa_cmp_kernel, tq=tq, grp=NSA_GROUP, ncp=ncp),
        grid=(kv_heads, t // tq),
        in_specs=[pl.BlockSpec((gw, tq), lambda g, i: (g, i)),
                  pl.BlockSpec((1, ncp, d), lambda g, i: (g, 0, 0)),
                  pl.BlockSpec((1, d, ncp), lambda g, i: (g, 0, 0)),
                  pl.BlockSpec((SEL_LANES, ncp), lambda g, i: (0, 0))],
        out_specs=[pl.BlockSpec((gw, tq), lambda g, i: (g, i)),
                   pl.BlockSpec((1, SEL_LANES, tq), lambda g, i: (g, 0, i))],
        out_shape=[jax.ShapeDtypeStruct((kv_heads * gw, t), F32),
                   jax.ShapeDtypeStruct((kv_heads, SEL_LANES, t), MXU_DTYPE)],
        compiler_params=_params("parallel", "parallel"),
        name="nsa_cmp_select",
    )(qt, kc, vct, ovt)


def _nsa_sel_kernel(qi_tab, kj_tab, qt_ref, mb_ref, ka_ref, vt_ref, o_ref, qa_scr, m_scr, l_scr, acc_scr,
                    *, tq, tk, grp, hpu):
    d = HEAD_DIM
    step = pl.program_id(1)
    qi, kj = qi_tab[step], kj_tab[step]
    last = (qi * tq + tq - 1) // tk
    units = grp // hpu
    uw = hpu * tq

    @pl.when(kj == 0)
    def _():
        for j in range(grp):
            u, ls = j // hpu, slice((j % hpu) * tq, (j % hpu + 1) * tq)
            qa_scr[u, 0:d, ls] = qt_ref[j * d:(j + 1) * d, :]
            qa_scr[u, d:2 * d, ls] = mb_ref[0]
        m_scr[...] = jnp.full_like(m_scr, NEG_INF)
        l_scr[...] = jnp.zeros_like(l_scr)
        acc_scr[...] = jnp.zeros_like(acc_scr)

    def masked_scores(diagonal):
        if diagonal:
            causal = (kj * tk + lax.broadcasted_iota(jnp.int32, (tk, uw), 0)
                      <= qi * tq + lax.broadcasted_iota(jnp.int32, (tk, uw), 1) % tq)

        def scores(u):
            return jnp.dot(ka_ref[0], qa_scr[u], preferred_element_type=F32)

        s_next = scores(0)
        for u in range(units):
            s = s_next
            if u + 1 < units:
                s_next = scores(u + 1)
            yield u, (jnp.where(causal, s, NEG_INF) if diagonal else s)

    def update(diagonal):
        for u, s in masked_scores(diagonal):
            m_prev = m_scr[u]
            m_new = jnp.maximum(m_prev, jnp.max(s, axis=0, keepdims=True))
            alpha = jnp.exp2(m_prev - m_new)
            p = jnp.exp2(s - m_new)
            l_scr[u] = alpha * l_scr[u] + jnp.sum(p, axis=0, keepdims=True)
            acc_scr[u] = alpha * acc_scr[u] + _dot(vt_ref[0], p)
            m_scr[u] = m_new

    def update_lagged(diagonal):
        pend, gap = [], None
        for u, s in masked_scores(diagonal):
            m_prev = m_scr[u]
            p = jnp.exp2(s - m_prev)
            tile_max = jnp.max(s, axis=0, keepdims=True)
            pend.append((tile_max, jnp.sum(p, axis=0, keepdims=True), _dot(vt_ref[0], p)))
            gap = tile_max - m_prev if gap is None else jnp.maximum(gap, tile_max - m_prev)
        safe = jnp.max(gap) <= SEL_LAG_MAX_GAP

        @pl.when(safe)
        def _():
            for u, (tile_max, p_sum, pv) in enumerate(pend):
                m_prev = m_scr[u]
                m_new = jnp.maximum(m_prev, tile_max)
                alpha = jnp.exp2(m_prev - m_new)
                l_scr[u] = (l_scr[u] + p_sum) * alpha
                acc_scr[u] = (acc_scr[u] + pv) * alpha
                m_scr[u] = m_new

        @pl.when(jnp.logical_not(safe))
        def _():
            update(diagonal)

    def finalize():
        for j in range(grp):
            u, ls = j // hpu, slice((j % hpu) * tq, (j % hpu + 1) * tq)
            o_ref[j * d:(j + 1) * d, :] = acc_scr[u, :, ls] * (1.0 / l_scr[u, :, ls])

    @pl.when((kj == 0) & (kj < last))
    def _():
        update(False)

    @pl.when((kj == 0) & (kj == last))
    def _():
        update(True)
        finalize()

    @pl.when((kj > 0) & (kj < last))
    def _():
        update_lagged(False)

    @pl.when((kj > 0) & (kj == last))
    def _():
        update_lagged(True)
        finalize()


def nsa_sel_attn(qt, mb, ks_aug, vst, *, kv_heads, tq, tk, hpu):
    t = qt.shape[1]
    d = HEAD_DIM
    gw = NSA_GROUP * d
    pairs = [(qi, kj) for qi in range(t // tq) for kj in range((qi * tq + tq - 1) // tk + 1)]
    qi_tab = jnp.asarray(np.array([p[0] for p in pairs], np.int32))
    kj_tab = jnp.asarray(np.array([p[1] for p in pairs], np.int32))
    units, uw = NSA_GROUP // hpu, hpu * tq
    grid_spec = pltpu.PrefetchScalarGridSpec(
        num_scalar_prefetch=2,
        grid=(kv_heads, len(pairs)),
        in_specs=[pl.BlockSpec((gw, tq), lambda g, s, qt, kt: (g, qt[s])),
                  pl.BlockSpec((1, SEL_LANES, tq), lambda g, s, qt, kt: (g, 0, qt[s])),
                  pl.BlockSpec((1, tk, 2 * d), lambda g, s, qt, kt: (g, kt[s], 0)),
                  pl.BlockSpec((1, d, tk), lambda g, s, qt, kt: (g, 0, kt[s]))],
        out_specs=pl.BlockSpec((gw, tq), lambda g, s, qt, kt: (g, qt[s])),
        scratch_shapes=[pltpu.VMEM((units, 2 * d, uw), MXU_DTYPE), pltpu.VMEM((units, 1, uw), F32),
                        pltpu.VMEM((units, 1, uw), F32), pltpu.VMEM((units, d, uw), F32)],
    )
    return pl.pallas_call(
        functools.partial(_nsa_sel_kernel, tq=tq, tk=tk, grp=NSA_GROUP, hpu=hpu),
        grid_spec=grid_spec,
        out_shape=jax.ShapeDtypeStruct((kv_heads * gw, t), F32),
        compiler_params=_params("parallel", "arbitrary"),
        name="nsa_sel_attn",
    )(qi_tab, kj_tab, qt, mb, ks_aug, vst)


def _nsa_win_kernel(qt_ref, k0_ref, k1_ref, k2_ref, v0_ref, v1_ref, v2_ref, ocmp_ref, osel_ref, zg_ref, y_ref,
                    *, tq, grp):
    d = HEAD_DIM
    qi = pl.program_id(1)
    k = jnp.concatenate([k0_ref[0], k1_ref[0], k2_ref[0]], axis=0)
    vt = jnp.concatenate([v0_ref[0], v1_ref[0], v2_ref[0]], axis=1)
    kpos = (qi - 2) * tq + lax.broadcasted_iota(jnp.int32, (3 * tq, tq), 0)
    rel = qi * tq + lax.broadcasted_iota(jnp.int32, (3 * tq, tq), 1) - kpos
    ok = (rel >= 0) & (rel < WINDOW) & (kpos >= 0)
    gates_t = _sigmoid(zg_ref[...]).T

    def scores(j):
        return jnp.dot(k, qt_ref[j * d:(j + 1) * d, :], preferred_element_type=F32)

    s_next = scores(0)
    for j in range(grp):
        rs = slice(j * d, (j + 1) * d)
        s = s_next
        if j + 1 < grp:
            s_next = scores(j + 1)
        sm = jnp.where(ok, s, NEG_INF)
        p = jnp.where(ok, jnp.exp2(sm - jnp.max(sm, axis=0, keepdims=True)), 0.0)
        o = _dot(vt, p) * (1.0 / jnp.sum(p, axis=0, keepdims=True))
        y = (gates_t[3 * j:3 * j + 1] * ocmp_ref[rs, :] + gates_t[3 * j + 1:3 * j + 2] * osel_ref[rs, :]
             + gates_t[3 * j + 2:3 * j + 3] * o)
        y_ref[:, rs] = y.T.astype(y_ref.dtype)


def nsa_win_combine(qt, kw, vwt, o_cmp, o_sel, zs, *, kv_heads, tq, zs_block0):
    t = qt.shape[1]
    d = HEAD_DIM
    gw = NSA_GROUP * d
    k_specs = [pl.BlockSpec((1, tq, d), functools.partial(lambda g, i, b: (g, jnp.maximum(i - b, 0), 0), b=b))
               for b in (2, 1, 0)]
    v_specs = [pl.BlockSpec((1, d, tq), functools.partial(lambda g, i, b: (g, 0, jnp.maximum(i - b, 0)), b=b))
               for b in (2, 1, 0)]
    tspec = pl.BlockSpec((gw, tq), lambda g, i: (g, i))
    return pl.pallas_call(
        functools.partial(_nsa_win_kernel, tq=tq, grp=NSA_GROUP),
        grid=(kv_heads, t // tq),
        in_specs=[tspec] + k_specs + v_specs + [tspec, tspec,
                                                pl.BlockSpec((tq, 128), lambda g, i: (i, zs_block0 + g))],
        out_specs=pl.BlockSpec((tq, gw), lambda g, i: (i, g)),
        out_shape=jax.ShapeDtypeStruct((t, kv_heads * gw), MXU_DTYPE),
        compiler_params=_params("parallel", "parallel"),
        name="nsa_win_combine",
    )(qt, kw, kw, kw, vwt, vwt, vwt, o_cmp, o_sel, zs)


def _split_in_proj(w_in, b_in, sizes):
    starts = np.concatenate([[0], np.cumsum(sizes)])
    i_gate, i_mli, i_mlf = 11, 16, 17
    g0, g1 = starts[i_gate], starts[i_gate + 1]
    half = (g1 - g0) // 2

    def pad(a, width):
        return jnp.pad(a, [(0, 0)] * (a.ndim - 1) + [(0, width - a.shape[-1])])

    def part_b(a):
        return jnp.concatenate([a[..., g1:starts[i_mli]], pad(a[..., g0:g0 + half], 128),
                                pad(a[..., g0 + half:g1], 128),
                                pad(a[..., starts[i_mli]:starts[i_mlf + 1]], 128)], axis=-1)

    return g0, part_b(w_in), part_b(b_in)


def kernel(x, mem, norm_mix, w_in, b_in, hgrn_lb_logits, hgrn_norm, nsa_q_norm, nsa_k_norm, nsa_cmp_pos, nsa_cmp_w, mlstm_conv, mlstm_norm, w_out, norm_xattn, norm_mem, xa_wq, xa_wk, xa_wv, xa_wo, xa_q_norm, xa_k_norm, norm_mlp, mlp_w1, mlp_w2):
    _, t, d_model = x.shape
    depth = w_in.shape[0]
    d = HEAD_DIM
    hg_heads = ml_heads = d_model // (4 * d)
    nsa_heads = d_model // (2 * d)
    kv_heads = nsa_heads // NSA_GROUP
    hw, nw, kvw = hg_heads * d, nsa_heads * d, kv_heads * d
    sizes = (hw,) * 4 + (nw,) + (kvw,) * 6 + (3 * nsa_heads,) + (hw,) * 4 + (ml_heads, ml_heads)
    col_nsa_q = 4 * hw
    col_nsa_kc = col_nsa_q + nw
    xa_heads = 4
    bf = MXU_DTYPE

    half = d // 2
    inv_freq = ROPE_THETA ** (-jnp.arange(half, dtype=F32) / half)

    def rope_tables(pos):
        ang = pos[:, None] * inv_freq[None, :]
        cos, sin = jnp.cos(ang), jnp.sin(ang)
        return jnp.concatenate([cos, cos], axis=-1), jnp.concatenate([-sin, sin], axis=-1)

    cos_t, sin_t = rope_tables(jnp.arange(t, dtype=F32))
    ncp = t // CMP_STRIDE
    n_cmp, n_sel = ncp - 1, t // SEL_BLOCK
    cos_c, sin_c = rope_tables(jnp.arange(ncp, dtype=F32) * CMP_STRIDE + (CMP_LEN - 1))
    et = jnp.asarray(np.arange(t)[:, None] // SEL_BLOCK == np.arange(SEL_LANES)[None, :], bf)
    c_start = np.arange(ncp)[:, None] * CMP_STRIDE
    s_start = np.arange(SEL_LANES)[None, :] * SEL_BLOCK
    overlap = ((c_start < s_start + SEL_BLOCK) & (c_start + CMP_LEN > s_start)
               & (np.arange(ncp)[:, None] < n_cmp) & (np.arange(SEL_LANES)[None, :] < n_sel))
    ovt = jnp.asarray(overlap.T, bf)

    w_in_c = w_in.astype(bf)
    n_a, w_b, b_b = _split_in_proj(w_in_c, b_in, sizes)
    w_out_c, wq_c, wk_c, wv_c, wo_c = (a.astype(bf) for a in (w_out, xa_wq, xa_wk, xa_wv, xa_wo))
    w1_c, w2_c, cmp_w_c = mlp_w1.astype(bf), mlp_w2.astype(bf), nsa_cmp_w.astype(bf)

    h = x.reshape(t, d_model)
    mem2 = mem.reshape(mem.shape[1], d_model)
    hn = rms_cast(h, norm_mix[0], tm=512)
    for layer in range(depth):
        z = matmul_cols(hn, w_in_c, b_in, layer=layer, n_tiles=3, tm=512, tn=n_a // 3)
        zb = matmul_cols(hn, w_b, b_b, layer=layer, n_tiles=1, tm=512, tn=w_b.shape[2])
        gate_block = 4 * hw // 128
        gt = zb[:, (gate_block + 2) * 128:(gate_block + 2) * 128 + 8].T

        y_hg = hgrn2(z, hgrn_lb_logits, hgrn_norm[layer], layer=layer, heads=hg_heads, tb=512)
        y_ml = mlstm(zb, zb, gt, mlstm_conv[layer], mlstm_norm[layer], heads=ml_heads, tb=512, col0=0,
                     zs_block=gate_block + 2)

        qt, ks_aug, vst, kw, vwt = nsa_prep(z, cos_t, sin_t, et, nsa_q_norm[layer], nsa_k_norm[layer],
                                            heads=nsa_heads, kv_heads=kv_heads, tq=512, col_q=col_nsa_q)
        kc, vct = nsa_compress(z, nsa_cmp_pos[layer], cmp_w_c[layer], nsa_k_norm[layer],
                               cos_c, sin_c, kv_heads=kv_heads, col_kc=col_nsa_kc)
        o_cmp, mb = nsa_cmp_select(qt, kc, vct, ovt, kv_heads=kv_heads, tq=256)
        o_sel = nsa_sel_attn(qt, mb, ks_aug, vst, kv_heads=kv_heads, tq=512, tk=512, hpu=1)
        y_ns = nsa_win_combine(qt, kw, vwt, o_cmp, o_sel, zb, kv_heads=kv_heads, tq=256, zs_block0=gate_block)

        kb = hw
        terms = [(y_hg, 0, 0)] + [(y_ns, c, 1 + c) for c in range(nw // kb)] + [(y_ml, 0, 1 + nw // kb)]
        h, hn = matmul_residual(terms, w_out_c, h, norm_xattn[layer], layer=layer, kb=kb, tm=512)

        k_mem, v_mem = xa_kv(mem2, norm_mem[layer], wk_c, wv_c, xa_k_norm[layer], layer=layer, heads=xa_heads)
        q_x = matmul_cols(hn, wq_c, None, layer=layer, n_tiles=1, tm=512, tn=d_model)
        o_x = xa_attn(q_x, xa_q_norm[layer], k_mem, v_mem, heads=xa_heads, tm=512)
        h = matmul_residual([(o_x, c, c) for c in range(d_model // kb)], wo_c, h, None, layer=layer, kb=kb, tm=512)

        if layer + 1 < depth:
            h, hn = mlp(h, norm_mlp[layer], w1_c, w2_c, norm_mix[layer + 1], layer=layer, tm=512, tf=1024)
        else:
            h = mlp(h, norm_mlp[layer], w1_c, w2_c, None, layer=layer, tm=512, tf=1024)
    return h.reshape(x.shape)
```

```python
import functools

import jax
import jax.numpy as jnp
import numpy as np
from jax import lax
from jax.experimental import pallas as pl
from jax.experimental.pallas import tpu as pltpu

F32 = jnp.float32
MXU_DTYPE = jnp.bfloat16

EPS = 1e-6
NEG_INF = -1e30
LOG2_E = 1.4426950408889634
HEAD_DIM = 128
ROPE_THETA = 10000.0

V7X_VMEM_LIMIT_BYTES = 56 * 1024 * 1024


def _params(*semantics):
    return pltpu.CompilerParams(dimension_semantics=semantics, vmem_limit_bytes=V7X_VMEM_LIMIT_BYTES)


def _rms(x, gain):
    return x * lax.rsqrt(jnp.mean(x * x, axis=-1, keepdims=True) + EPS) * gain


def _dot(a, b):
    return jnp.dot(a.astype(MXU_DTYPE), b.astype(MXU_DTYPE), preferred_element_type=F32)


def _dot_nt(a, b):
    return lax.dot_general(a.astype(MXU_DTYPE), b.astype(MXU_DTYPE), (((1,), (1,)), ((), ())),
                           preferred_element_type=F32)


def _dot_tn(a, b):
    return lax.dot_general(a.astype(MXU_DTYPE), b.astype(MXU_DTYPE), (((0,), (0,)), ((), ())),
                           preferred_element_type=F32)


def _rms_cast_kernel(x_ref, g_ref, o_ref):
    o_ref[...] = _rms(x_ref[...], g_ref[...]).astype(o_ref.dtype)


def rms_cast(x, gain, *, tm):
    m, d = x.shape
    return pl.pallas_call(
        _rms_cast_kernel,
        grid=(m // tm,),
        in_specs=[pl.BlockSpec((tm, d), lambda i: (i, 0)), pl.BlockSpec((1, d), lambda i: (0, 0))],
        out_specs=pl.BlockSpec((tm, d), lambda i: (i, 0)),
        out_shape=jax.ShapeDtypeStruct((m, d), MXU_DTYPE),
        compiler_params=_params("parallel"),
        name="rms_cast",
    )(x, gain.reshape(1, d))


def _matmul_bias_kernel(h_ref, w_ref, b_ref, o_ref):
    o_ref[...] = jnp.dot(h_ref[...], w_ref[...], preferred_element_type=F32) + b_ref[...]


def _matmul_kernel(h_ref, w_ref, o_ref):
    o_ref[...] = jnp.dot(h_ref[...], w_ref[...], preferred_element_type=F32)


def matmul_cols(h, w, bias, *, layer, n_tiles, tm, tn):
    m, d = h.shape
    in_specs = [pl.BlockSpec((tm, d), lambda j, i: (i, 0)),
                pl.BlockSpec((None, d, tn), lambda j, i: (layer, 0, j))]
    args = [h, w]
    if bias is not None:
        in_specs.append(pl.BlockSpec((None, 1, tn), lambda j, i: (layer, 0, j)))
        args.append(bias.reshape(bias.shape[0], 1, bias.shape[1]))
    return pl.pallas_call(
        _matmul_kernel if bias is None else _matmul_bias_kernel,
        grid=(n_tiles, m // tm),
        in_specs=in_specs,
        out_specs=pl.BlockSpec((tm, tn), lambda j, i: (i, j)),
        out_shape=jax.ShapeDtypeStruct((m, n_tiles * tn), F32),
        compiler_params=_params("parallel", "parallel"),
        name="matmul_cols",
    )(*args)


def _matmul_res_kernel(*refs, n_terms, with_norm):
    ys, ws = refs[:n_terms], refs[n_terms:2 * n_terms]
    res_ref = refs[2 * n_terms]
    acc = res_ref[...]
    for y_ref, w_ref in zip(ys, ws):
        acc = acc + jnp.dot(y_ref[...].astype(MXU_DTYPE), w_ref[...], preferred_element_type=F32)
    if with_norm:
        g_ref, o_ref, hn_ref = refs[2 * n_terms + 1:]
        hn_ref[...] = _rms(acc, g_ref[...]).astype(hn_ref.dtype)
    else:
        o_ref = refs[2 * n_terms + 1]
    o_ref[...] = acc


def matmul_residual(terms, w, res, next_gain, *, layer, kb, tm):
    m, n = res.shape
    y_specs = [pl.BlockSpec((tm, kb), functools.partial(lambda i, c: (i, c), c=cb)) for _, cb, _ in terms]
    w_specs = [pl.BlockSpec((None, kb, n), functools.partial(lambda i, r: (layer, r, 0), r=rb)) for _, _, rb in terms]
    row_spec = pl.BlockSpec((tm, n), lambda i: (i, 0))
    with_norm = next_gain is not None
    return pl.pallas_call(
        functools.partial(_matmul_res_kernel, n_terms=len(terms), with_norm=with_norm),
        grid=(m // tm,),
        in_specs=y_specs + w_specs + [row_spec] + ([pl.BlockSpec((1, n), lambda i: (0, 0))] if with_norm else []),
        out_specs=[row_spec, row_spec] if with_norm else row_spec,
        out_shape=([jax.ShapeDtypeStruct((m, n), F32), jax.ShapeDtypeStruct((m, n), MXU_DTYPE)] if with_norm
                   else jax.ShapeDtypeStruct((m, n), F32)),
        compiler_params=_params("parallel"),
        name="matmul_residual",
    )(*[t[0] for t in terms], *([w] * len(terms)), res, *([next_gain.reshape(1, n)] if with_norm else []))


def _mlp_kernel(*refs, with_norm):
    if with_norm:
        x_ref, g_ref, w1_ref, w2_ref, gn_ref, o_ref, hn_ref, h_scr = refs
    else:
        x_ref, g_ref, w1_ref, w2_ref, o_ref, h_scr = refs

    @pl.when(pl.program_id(1) == 0)
    def _():
        x = x_ref[...]
        h_scr[...] = _rms(x, g_ref[...]).astype(h_scr.dtype)
        o_ref[...] = x

    u = jnp.dot(h_scr[...], w1_ref[...], preferred_element_type=F32)
    a = jnp.square(jnp.maximum(u, 0.0)).astype(MXU_DTYPE)
    o_ref[...] += jnp.dot(a, w2_ref[...], preferred_element_type=F32)

    if with_norm:
        @pl.when(pl.program_id(1) == pl.num_programs(1) - 1)
        def _():
            hn_ref[...] = _rms(o_ref[...], gn_ref[...]).astype(hn_ref.dtype)


def mlp(x, gain, w1, w2, next_gain, *, layer, tm, tf):
    m, d = x.shape
    ff = w1.shape[2]
    row_spec = pl.BlockSpec((tm, d), lambda i, f: (i, 0))
    gain_spec = pl.BlockSpec((1, d), lambda i, f: (0, 0))
    with_norm = next_gain is not None
    return pl.pallas_call(
        functools.partial(_mlp_kernel, with_norm=with_norm),
        grid=(m // tm, ff // tf),
        in_specs=[row_spec, gain_spec,
                  pl.BlockSpec((None, d, tf), lambda i, f: (layer, 0, f)),
                  pl.BlockSpec((None, tf, d), lambda i, f: (layer, f, 0))] + ([gain_spec] if with_norm else []),
        out_specs=[row_spec, row_spec] if with_norm else row_spec,
        out_shape=([jax.ShapeDtypeStruct((m, d), F32), jax.ShapeDtypeStruct((m, d), MXU_DTYPE)] if with_norm
                   else jax.ShapeDtypeStruct((m, d), F32)),
        scratch_shapes=[pltpu.VMEM((tm, d), MXU_DTYPE)],
        compiler_params=_params("parallel", "arbitrary"),
        name="mlp",
    )(x, gain.reshape(1, d), w1, w2, *([next_gain.reshape(1, d)] if with_norm else []))


def _xa_kv_kernel(mem_ref, g_ref, wk_ref, wv_ref, kg_ref, k_ref, v_ref):
    mn = _rms(mem_ref[...], g_ref[...]).astype(MXU_DTYPE)
    k = jnp.dot(mn, wk_ref[...], preferred_element_type=F32)
    k_ref[...] = _rms(k, kg_ref[...]).astype(k_ref.dtype)
    v_ref[...] = jnp.dot(mn, wv_ref[...], preferred_element_type=F32).astype(v_ref.dtype)


def xa_kv(mem, gain, wk, wv, k_gain, *, layer, heads):
    ml, d = mem.shape
    hd = d // heads
    return pl.pallas_call(
        _xa_kv_kernel,
        grid=(heads,),
        in_specs=[
            pl.BlockSpec((ml, d), lambda h: (0, 0)),
            pl.BlockSpec((1, d), lambda h: (0, 0)),
            pl.BlockSpec((None, d, hd), lambda h: (layer, 0, h)),
            pl.BlockSpec((None, d, hd), lambda h: (layer, 0, h)),
            pl.BlockSpec((1, hd), lambda h: (0, 0)),
        ],
        out_specs=[pl.BlockSpec((ml, hd), lambda h: (0, h))] * 2,
        out_shape=[jax.ShapeDtypeStruct((ml, d), MXU_DTYPE)] * 2,
        compiler_params=_params("parallel"),
        name="xa_kv",
    )(mem, gain.reshape(1, d), wk, wv, k_gain.reshape(1, hd))


def _xa_attn_kernel(q_ref, qg_ref, k_ref, v_ref, o_ref, *, heads, hd):
    scale = hd ** -0.5
    for h in range(heads):
        cs = slice(h * hd, (h + 1) * hd)
        q = _rms(q_ref[:, cs], qg_ref[...]) * scale
        s = _dot_nt(q, k_ref[:, cs])
        p = jnp.exp(s - jnp.max(s, axis=-1, keepdims=True))
        p = p / jnp.sum(p, axis=-1, keepdims=True)
        o_ref[:, cs] = _dot(p, v_ref[:, cs]).astype(o_ref.dtype)


def xa_attn(q, q_gain, k, v, *, heads, tm):
    m, d = q.shape
    ml = k.shape[0]
    hd = d // heads
    return pl.pallas_call(
        functools.partial(_xa_attn_kernel, heads=heads, hd=hd),
        grid=(m // tm,),
        in_specs=[
            pl.BlockSpec((tm, d), lambda i: (i, 0)),
            pl.BlockSpec((1, hd), lambda i: (0, 0)),
            pl.BlockSpec((ml, d), lambda i: (0, 0)),
            pl.BlockSpec((ml, d), lambda i: (0, 0)),
        ],
        out_specs=pl.BlockSpec((tm, d), lambda i: (i, 0)),
        out_shape=jax.ShapeDtypeStruct((m, d), MXU_DTYPE),
        compiler_params=_params("parallel"),
        name="xa_attn",
    )(q, q_gain.reshape(1, hd), k, v)


def _seg_cumsum_rows(x, seg):
    pos = lax.broadcasted_iota(jnp.int32, x.shape, 0) % seg
    k = 1
    while k < seg:
        x = x + jnp.where(pos >= k, pltpu.roll(x, k, axis=0), 0.0)
        k *= 2
    return x


def _seg_cumsum_lanes(x, seg):
    pos = lax.broadcasted_iota(jnp.int32, x.shape, 1) % seg
    k = 1
    while k < seg:
        x = x + jnp.where(pos >= k, pltpu.roll(x, k, axis=1), 0.0)
        k *= 2
    return x


def _sigmoid(x):
    return 1.0 / (1.0 + jnp.exp(-x))


def _log_sigmoid(x):
    return jnp.minimum(x, 0.0) - jnp.log(1.0 + jnp.exp(-jnp.abs(x)))


HG_SUB = 16
HG_MIN_FORGET = 1e-20


def _hgrn2_kernel(q_ref, f_ref, i_ref, g_ref, lbl_ref, ng_ref, y_ref, st_ref, *, layer, heads, tb):
    dk = HEAD_DIM

    @pl.when(pl.program_id(0) == 0)
    def _():
        st_ref[...] = jnp.zeros_like(st_ref)

    logits = lbl_ref[...]
    e = jnp.exp(logits - jnp.max(logits, axis=0, keepdims=True))
    probs = e / jnp.sum(e, axis=0, keepdims=True)
    lb_all = jnp.zeros_like(probs[0:1])
    for l in range(1, layer + 1):
        lb_all = lb_all + probs[l:l + 1]
    half = HG_SUB // 2
    row_half = lax.broadcasted_iota(jnp.int32, (half, dk), 0)

    def body(c, carry):
        r = pl.ds(pl.multiple_of(c * HG_SUB, HG_SUB), HG_SUB)
        pre = []
        for h in range(heads):
            cs = slice(h * dk, (h + 1) * dk)
            lb = lb_all[:, cs]
            q, zf, v = q_ref[r, cs], f_ref[r, cs], i_ref[r, cs]
            forget = lb + (1.0 - lb) * _sigmoid(zf)
            log_f = jnp.log(jnp.maximum(forget, HG_MIN_FORGET))
            key = (1.0 - lb) * _sigmoid(-zf)
            b = _seg_cumsum_rows(log_f, HG_SUB) * LOG2_E
            pre.append((q, v, key, b, b[HG_SUB - 1:HG_SUB]))
        inter = [_dot_nt(q * jnp.exp2(b), st_ref[h]) for h, (q, v, key, b, b_end) in enumerate(pre)]
        upd = [_dot_tn(v, key * jnp.exp2(b_end - b)) for q, v, key, b, b_end in pre]
        for h in range(heads):
            cs = slice(h * dk, (h + 1) * dk)
            q, v, key, b, b_end = pre[h]
            q_lo, q_hi, b_lo, b_hi = q[0:half], q[half:], b[0:half], b[half:]
            o_lo = jnp.zeros((half, dk), F32)
            o_hi = jnp.zeros((half, dk), F32)
            for s in range(HG_SUB):
                bs, ks, vs = b[s:s + 1], key[s:s + 1], v[s:s + 1]
                if s < half:
                    decay = jnp.exp2(jnp.where(row_half >= s, b_lo - bs, NEG_INF))
                    o_lo = o_lo + jnp.sum(q_lo * decay * ks, axis=-1, keepdims=True) * vs
                    decay = jnp.exp2(b_hi - bs)
                else:
                    decay = jnp.exp2(jnp.where(row_half >= s - half, b_hi - bs, NEG_INF))
                o_hi = o_hi + jnp.sum(q_hi * decay * ks, axis=-1, keepdims=True) * vs
            o = jnp.concatenate([o_lo, o_hi], axis=0) + inter[h]
            st_ref[h] = jnp.exp2(b_end) * st_ref[h] + upd[h]
            g = g_ref[r, cs]
            y = _rms(o, ng_ref[:, cs]) * (g * _sigmoid(g))
            y_ref[r, cs] = y.astype(y_ref.dtype)
        return carry

    lax.fori_loop(0, tb // HG_SUB, body, 0)


def hgrn2(z, lb_logits, norm_gain, *, layer, heads, tb):
    t = z.shape[0]
    w = heads * HEAD_DIM
    depth = lb_logits.shape[0]
    return pl.pallas_call(
        functools.partial(_hgrn2_kernel, layer=layer, heads=heads, tb=tb),
        grid=(t // tb,),
        in_specs=[pl.BlockSpec((tb, w), functools.partial(lambda i, c: (i, c), c=c)) for c in range(4)] + [
            pl.BlockSpec((depth, w), lambda i: (0, 0)),
            pl.BlockSpec((1, w), lambda i: (0, 0)),
        ],
        out_specs=pl.BlockSpec((tb, w), lambda i: (i, 0)),
        out_shape=jax.ShapeDtypeStruct((t, w), MXU_DTYPE),
        scratch_shapes=[pltpu.VMEM((heads, HEAD_DIM, HEAD_DIM), F32)],
        compiler_params=_params("arbitrary"),
        name="hgrn2",
    )(z, z, z, z, lb_logits, norm_gain.reshape(1, w))


ML_CHUNK = 64
ML_CONV = 4
ML_TAIL = 8
ML_AUG = HEAD_DIM + 16
ZS_ML_I = 0
ZS_ML_F = 4


def _mlstm_kernel(q_ref, k_ref, v_ref, o_ref, zs_ref, gt_ref, cw_ref, ng_ref, y_ref,
                  conv_scr, st_scr, m_scr, *, heads, tb):
    d = HEAD_DIM
    w = heads * d

    @pl.when(pl.program_id(0) == 0)
    def _():
        conv_scr[0:ML_TAIL, :] = jnp.zeros((ML_TAIL, 2 * w), F32)
        st_scr[...] = jnp.zeros_like(st_scr)
        m_scr[...] = jnp.zeros_like(m_scr)

    conv_scr[ML_TAIL:ML_TAIL + tb, 0:w] = q_ref[...]
    conv_scr[ML_TAIL:ML_TAIL + tb, w:2 * w] = k_ref[...]
    acc = jnp.zeros((tb, 2 * w), F32)
    for j in range(ML_CONV):
        off = ML_TAIL - (ML_CONV - 1) + j
        acc = acc + conv_scr[off:off + tb, :] * cw_ref[j:j + 1, :]
    conv_scr[0:ML_TAIL, :] = conv_scr[tb:tb + ML_TAIL, :]
    qk = acc * _sigmoid(acc)

    pair = 2 * ML_CHUNK
    zs = zs_ref[...]
    b_col_all = _seg_cumsum_rows(_log_sigmoid(zs), ML_CHUNK)
    gt = gt_ref[...]
    b_row_all = _seg_cumsum_lanes(_log_sigmoid(gt), ML_CHUNK)
    s_idx = lax.broadcasted_iota(jnp.int32, (pair, pair), 0)
    t_idx = lax.broadcasted_iota(jnp.int32, (pair, pair), 1)
    visible = (s_idx // ML_CHUNK == t_idx // ML_CHUNK) & (s_idx <= t_idx)
    lane = lax.broadcasted_iota(jnp.int32, (1, pair), 1)
    in_chunk = [lane < ML_CHUNK, lane >= ML_CHUNK]
    ones_rows = (lax.broadcasted_iota(jnp.int32, (ML_AUG - d, pair), 0) == 0).astype(F32)

    def rows(p):
        return slice(p * pair, (p + 1) * pair)

    def stage_scores(p):
        out = []
        for h in range(heads):
            qt = (qk[rows(p), h * d:(h + 1) * d] * (d ** -0.5)).T
            out.append((qt, _dot(qk[rows(p), w + h * d:w + (h + 1) * d], qt)))
        return out

    def stage_free(p, sc):
        rs = rows(p)
        out = []
        for h in range(heads):
            qt, s_raw = sc[h]
            b_row = b_row_all[heads + h:heads + h + 1, rs]
            li_row = gt[h:h + 1, rs]
            u_col = b_col_all[rs, ZS_ML_F + h:ZS_ML_F + h + 1] - zs[rs, ZS_ML_I + h:ZS_ML_I + h + 1]
            d_log = jnp.where(visible, b_row - u_col, NEG_INF)
            dmax = jnp.max(d_log, axis=0, keepdims=True)
            s_t = s_raw * jnp.exp(d_log - dmax)
            vt_aug = jnp.concatenate([v_ref[rs, h * d:(h + 1) * d].T, ones_rows], axis=0)
            kc = qk[rs, w + h * d:w + (h + 1) * d]
            per_chunk = []
            for c in range(2):
                b_end = b_row[:, (c + 1) * ML_CHUNK - 1:(c + 1) * ML_CHUNK]
                w_state = jnp.where(in_chunk[c], b_end - b_row + li_row, NEG_INF)
                wmax = jnp.max(w_state, axis=1, keepdims=True)
                w_row = jnp.exp(w_state - wmax)
                per_chunk.append((b_end, wmax, _dot(vt_aug * w_row, kc)))
            out.append(dict(qt=qt, b_row=b_row, dmax=dmax, intra=_dot(vt_aug, s_t), per_chunk=per_chunk))
        return out

    n_pairs = tb // pair
    sc = {0: stage_scores(0)}
    if n_pairs > 1:
        sc[1] = stage_scores(1)
    free = {0: stage_free(0, sc.pop(0))}
    for p in range(n_pairs):
        rs = rows(p)
        if p + 2 < n_pairs:
            sc[p + 2] = stage_scores(p + 2)
        if p + 1 < n_pairs:
            free[p + 1] = stage_free(p + 1, sc.pop(p + 1))
        cur = free.pop(p)
        for h in range(heads):
            cs = slice(h * d, (h + 1) * d)
            f = cur[h]
            num, m_tok = None, None
            for c in range(2):
                b_end, wmax, upd = f["per_chunk"][c]
                state, m_prev = st_scr[h], m_scr[h]
                inter = _dot(state, f["qt"])
                inter_log = f["b_row"] + m_prev
                m_t = jnp.maximum(inter_log, f["dmax"])
                num_c = jnp.exp(inter_log - m_t) * inter + jnp.exp(f["dmax"] - m_t) * f["intra"]
                num = num_c if c == 0 else jnp.where(in_chunk[0], num, num_c)
                m_tok = m_t if c == 0 else jnp.where(in_chunk[0], m_tok, m_t)
                m_new = jnp.maximum(b_end + m_prev, wmax)
                st_scr[h] = jnp.exp(b_end + m_prev - m_new) * state + jnp.exp(wmax - m_new) * upd
                m_scr[h] = m_new
            qn = num[d:d + 1]
            hcell = num[0:d] / jnp.maximum(jnp.abs(qn), jnp.exp(-m_tok))
            hn = hcell * lax.rsqrt(jnp.mean(hcell * hcell, axis=0, keepdims=True) + EPS)
            y = hn.T * ng_ref[:, cs] * _sigmoid(o_ref[rs, cs])
            y_ref[rs, cs] = y.astype(y_ref.dtype)


def mlstm(z, zs, gt, conv_w, norm_gain, *, heads, tb, col0, zs_block):
    t = z.shape[0]
    w = heads * HEAD_DIM
    cb = col0 // w
    return pl.pallas_call(
        functools.partial(_mlstm_kernel, heads=heads, tb=tb),
        grid=(t // tb,),
        in_specs=[pl.BlockSpec((tb, w), functools.partial(lambda i, c: (i, c), c=cb + c)) for c in range(4)] + [
            pl.BlockSpec((tb, 128), lambda i: (i, zs_block)),
            pl.BlockSpec((8, tb), lambda i: (0, i)),
            pl.BlockSpec((ML_CONV, 2 * w), lambda i: (0, 0)),
            pl.BlockSpec((1, w), lambda i: (0, 0)),
        ],
        out_specs=pl.BlockSpec((tb, w), lambda i: (i, 0)),
        out_shape=jax.ShapeDtypeStruct((t, w), MXU_DTYPE),
        scratch_shapes=[
            pltpu.VMEM((tb + ML_TAIL, 2 * w), F32),
            pltpu.VMEM((heads, ML_AUG, HEAD_DIM), F32),
            pltpu.VMEM((heads, 1, 1), F32),
        ],
        compiler_params=_params("arbitrary"),
        name="mlstm",
    )(z, z, z, z, zs, gt, conv_w, norm_gain.reshape(1, w))


NSA_GROUP = 4
CMP_LEN = 32
CMP_STRIDE = 16
SEL_BLOCK = 64
N_SELECT = 16
WINDOW = 512
FORCE_BONUS = 1e4
SEL_LANES = 128
SEL_LAG_MAX_GAP = 64.0
SEL_BOUND_MARGIN = 1.02


def _rope(x, cos_t, sin_t):
    return x * cos_t + pltpu.roll(x, HEAD_DIM // 2, axis=1) * sin_t


def _nsa_prep_kernel(q_ref, ks_ref, vs_ref, kw_ref, vw_ref, cos_ref, sin_ref, et_ref, qg_ref, kg_ref,
                     qh_ref, ksa_ref, vso_ref, kwr_ref, vwo_ref, qn_ref, kn_ref, *, heads, kv_heads):
    d = HEAD_DIM
    cos_t, sin_t = cos_ref[...], sin_ref[...]
    for h in range(heads):
        cs = slice(h * d, (h + 1) * d)
        qt = (_rope(_rms(q_ref[:, cs], qg_ref[...]), cos_t, sin_t) * (d ** -0.5 * LOG2_E)).T
        qh_ref[cs, :] = qt.astype(qh_ref.dtype)
        qn_ref[h:h + 1, :] = jnp.sqrt(jnp.sum(qt * qt, axis=0, keepdims=True))
    for g in range(kv_heads):
        cs = slice(g * d, (g + 1) * d)
        ks = _rope(_rms(ks_ref[:, cs], kg_ref[1:2]), cos_t, sin_t)
        ksa_ref[g, :, 0:d] = ks.astype(ksa_ref.dtype)
        k_norm2 = jnp.max(jnp.sum(ks * ks, axis=-1, keepdims=True), axis=0, keepdims=True)
        kn_ref[g, 0] = jnp.broadcast_to(jnp.sqrt(k_norm2), kn_ref.shape[2:])
        ksa_ref[g, :, d:2 * d] = et_ref[...]
        kwr_ref[g] = _rope(_rms(kw_ref[:, cs], kg_ref[2:3]), cos_t, sin_t).astype(kwr_ref.dtype)
        vso_ref[g] = vs_ref[:, cs].T.astype(vso_ref.dtype)
        vwo_ref[g] = vw_ref[:, cs].T.astype(vwo_ref.dtype)


def nsa_prep(z, cos_t, sin_t, et, q_gain, k_gains, *, heads, kv_heads, tq, col_q):
    t = z.shape[0]
    d = HEAD_DIM
    kvw = kv_heads * d
    qb = col_q // (heads * d)
    kb = (col_q + heads * d) // kvw

    def zcol(width, c):
        return pl.BlockSpec((tq, width), functools.partial(lambda i, c: (i, c), c=c))

    k_out = pl.BlockSpec((kv_heads, tq, d), lambda i: (0, i, 0))
    k_shape = jax.ShapeDtypeStruct((kv_heads, t, d), MXU_DTYPE)
    vt_out = pl.BlockSpec((kv_heads, d, tq), lambda i: (0, 0, i))
    vt_shape = jax.ShapeDtypeStruct((kv_heads, d, t), MXU_DTYPE)
    return pl.pallas_call(
        functools.partial(_nsa_prep_kernel, heads=heads, kv_heads=kv_heads),
        grid=(t // tq,),
        in_specs=[zcol(heads * d, qb), zcol(kvw, kb + 2), zcol(kvw, kb + 3), zcol(kvw, kb + 4), zcol(kvw, kb + 5),
                  pl.BlockSpec((tq, d), lambda i: (i, 0)), pl.BlockSpec((tq, d), lambda i: (i, 0)),
                  pl.BlockSpec((tq, SEL_LANES), lambda i: (i, 0)),
                  pl.BlockSpec((1, d), lambda i: (0, 0)), pl.BlockSpec((3, d), lambda i: (0, 0))],
        out_specs=[pl.BlockSpec((heads * d, tq), lambda i: (0, i)),
                   pl.BlockSpec((kv_heads, tq, 2 * d), lambda i: (0, i, 0)), vt_out, k_out, vt_out,
                   pl.BlockSpec((heads, tq), lambda i: (0, i)),
                   pl.BlockSpec((kv_heads, 1, 8, d), lambda i: (0, i, 0, 0))],
        out_shape=[jax.ShapeDtypeStruct((heads * d, t), MXU_DTYPE),
                   jax.ShapeDtypeStruct((kv_heads, t, 2 * d), MXU_DTYPE), vt_shape, k_shape, vt_shape,
                   jax.ShapeDtypeStruct((heads, t), F32),
                   jax.ShapeDtypeStruct((kv_heads, t // tq, 8, d), F32)],
        compiler_params=_params("parallel"),
        name="nsa_prep",
    )(z, z, z, z, z, cos_t, sin_t, et, q_gain.reshape(1, d), k_gains)


def _nsa_compress_kernel(zk_ref, zv_ref, pe_ref, w_ref, kg_ref, cos_ref, sin_ref, kc_ref, vc_ref, *, ncp):
    d = HEAD_DIM
    half = CMP_LEN // 2
    row = lax.broadcasted_iota(jnp.int32, (ncp, d), 0)

    def compress(z_ref, which):
        lo = jnp.zeros((ncp, d), F32)
        hi = jnp.zeros((ncp, d), F32)
        for l in range(half):
            xl = z_ref[pl.ds(l, ncp, stride=CMP_STRIDE), :]
            lo = lo + _dot(xl + pe_ref[which, l:l + 1, :], w_ref[which, l])
            hi = hi + _dot(xl + pe_ref[which, half + l:half + l + 1, :], w_ref[which, half + l])
        out = lo + pltpu.roll(hi, ncp - 1, axis=0)
        return jnp.where(row < ncp - 1, out, 0.0)

    kc_ref[0] = _rope(_rms(compress(zk_ref, 0), kg_ref[0:1]), cos_ref[...], sin_ref[...]).astype(kc_ref.dtype)
    vc_ref[0] = compress(zv_ref, 1).T.astype(vc_ref.dtype)


def nsa_compress(z, cmp_pos, cmp_w, k_gains, cos_c, sin_c, *, kv_heads, col_kc):
    t = z.shape[0]
    d = HEAD_DIM
    ncp = t // CMP_STRIDE
    kb = col_kc // d
    out_spec = pl.BlockSpec((1, ncp, d), lambda g: (g, 0, 0))
    out_shape = jax.ShapeDtypeStruct((kv_heads, ncp, d), MXU_DTYPE)
    return pl.pallas_call(
        functools.partial(_nsa_compress_kernel, ncp=ncp),
        grid=(kv_heads,),
        in_specs=[pl.BlockSpec((t, d), lambda g: (0, kb + g)),
                  pl.BlockSpec((t, d), lambda g: (0, kb + kv_heads + g)),
                  pl.BlockSpec((2, CMP_LEN, d), lambda g: (0, 0, 0)),
                  pl.BlockSpec((2, CMP_LEN, d, d), lambda g: (0, 0, 0, 0)),
                  pl.BlockSpec((3, d), lambda g: (0, 0)),
                  pl.BlockSpec((ncp, d), lambda g: (0, 0)), pl.BlockSpec((ncp, d), lambda g: (0, 0))],
        out_specs=[out_spec, pl.BlockSpec((1, d, ncp), lambda g: (g, 0, 0))],
        out_shape=[out_shape, jax.ShapeDtypeStruct((kv_heads, d, ncp), MXU_DTYPE)],
        compiler_params=_params("parallel"),
        name="nsa_compress",
    )(z, z, cmp_pos, cmp_w, k_gains, cos_c, sin_c)


def _nsa_cmp_kernel(qt_ref, kc_ref, vct_ref, ovt_ref, ocmp_ref, mb_ref, *, tq, grp, ncp):
    d = HEAD_DIM
    qi = pl.program_id(1)
    cmp_end = lax.broadcasted_iota(jnp.int32, (ncp, tq), 0) * CMP_STRIDE + (CMP_LEN - 1)
    vis = cmp_end <= qi * tq + lax.broadcasted_iota(jnp.int32, (ncp, tq), 1)
    psum = None

    def scores(j):
        return jnp.dot(kc_ref[0], qt_ref[j * d:(j + 1) * d, :], preferred_element_type=F32)

    s_next = scores(0)
    for j in range(grp):
        s = s_next
        if j + 1 < grp:
            s_next = scores(j + 1)
        sm = jnp.where(vis, s, NEG_INF)
        p = jnp.where(vis, jnp.exp2(sm - jnp.max(sm, axis=0, keepdims=True)), 0.0)
        l = jnp.sum(p, axis=0, keepdims=True)
        p = p * (1.0 / jnp.where(l > 0.0, l, 1.0))
        ocmp_ref[j * d:(j + 1) * d, :] = _dot(vct_ref[0], p)
        psum = p if psum is None else psum + p
    hi = psum.astype(MXU_DTYPE)
    lo = (psum - hi.astype(F32)).astype(MXU_DTYPE)
    imp = (jnp.dot(ovt_ref[...], hi, preferred_element_type=F32)
           + jnp.dot(ovt_ref[...], lo, preferred_element_type=F32))
    blk = lax.broadcasted_iota(jnp.int32, (SEL_LANES, tq), 0)
    cur = (qi * tq + lax.broadcasted_iota(jnp.int32, (SEL_LANES, tq), 1)) // SEL_BLOCK
    forced = (blk == 0) | (blk == cur) | (blk == cur - 1)
    score = jnp.where(blk <= cur, imp + jnp.where(forced, FORCE_BONUS, 0.0), NEG_INF)
    blk_f = blk.astype(F32)
    bias = jnp.full((SEL_LANES, tq), NEG_INF, F32)
    for _ in range(N_SELECT):
        mx = jnp.max(score, axis=0, keepdims=True)
        first = jnp.min(jnp.where(score == mx, blk_f, float(SEL_LANES)), axis=0, keepdims=True)
        pick = blk_f == first
        bias = jnp.where(pick, jnp.where(mx > 0.5 * NEG_INF, 0.0, NEG_INF), bias)
        score = jnp.where(pick, -jnp.inf, score)
    mb_ref[0] = bias.astype(mb_ref.dtype)


def nsa_cmp_select(qt, kc, vct, ovt, *, kv_heads, tq):
    t = qt.shape[1]
    d = HEAD_DIM
    ncp = kc.shape[1]
    gw = NSA_GROUP * d
    return pl.pallas_call(
        functools.partial(_nsa_cmp_kernel, tq=tq, grp=NSA_GROUP, ncp=ncp),
        grid=(kv_heads, t // tq),
        in_specs=[pl.BlockSpec((gw, tq), lambda g, i: (g, i)),
                  pl.BlockSpec((1, ncp, d), lambda g, i: (g, 0, 0)),
                  pl.BlockSpec((1, d, ncp), lambda g, i: (g, 0, 0)),
                  pl.BlockSpec((SEL_LANES, ncp), lambda g, i: (0, 0))],
        out_specs=[pl.BlockSpec((gw, tq), lambda g, i: (g, i)),
                   pl.BlockSpec((1, SEL_LANES, tq), lambda g, i: (g, 0, i))],
        out_shape=[jax.ShapeDtypeStruct((kv_heads * gw, t), F32),
                   jax.ShapeDtypeStruct((kv_heads, SEL_LANES, t), MXU_DTYPE)],
        compiler_params=_params("parallel", "parallel"),
        name="nsa_cmp_select",
    )(qt, kc, vct, ovt)


def _nsa_sel_kernel(qi_tab, kj_tab, qt_ref, mb_ref, ka_ref, vt_ref, qn_ref, kn_ref, o_ref,
                    qa_scr, m_scr, l_scr, acc_scr,
                    *, tq, tk, kv_heads, grp, hpu):
    d = HEAD_DIM
    step = pl.program_id(0)
    qi, kj = qi_tab[step], kj_tab[step]
    last = (qi * tq + tq - 1) // tk
    heads = kv_heads * grp
    units = heads // hpu
    uw = hpu * tq

    def kv_of(u):
        return u * hpu // grp

    @pl.when(kj == 0)
    def _():
        for j in range(heads):
            u, ls = j // hpu, slice((j % hpu) * tq, (j % hpu + 1) * tq)
            qa_scr[u, 0:d, ls] = qt_ref[j * d:(j + 1) * d, :]
            qa_scr[u, d:2 * d, ls] = mb_ref[j // grp]
        m_scr[...] = jnp.full_like(m_scr, NEG_INF)
        l_scr[...] = jnp.zeros_like(l_scr)
        acc_scr[...] = jnp.zeros_like(acc_scr)

    def masked_scores(diagonal):
        if diagonal:
            causal = (kj * tk + lax.broadcasted_iota(jnp.int32, (tk, uw), 0)
                      <= qi * tq + lax.broadcasted_iota(jnp.int32, (tk, uw), 1) % tq)

        def scores(u):
            return jnp.dot(ka_ref[kv_of(u)], qa_scr[u], preferred_element_type=F32)

        s_next = scores(0)
        for u in range(units):
            s = s_next
            if u + 1 < units:
                s_next = scores(u + 1)
            yield u, (jnp.where(causal, s, NEG_INF) if diagonal else s)

    def update(diagonal):
        for u, s in masked_scores(diagonal):
            m_prev = m_scr[u]
            m_new = jnp.maximum(m_prev, jnp.max(s, axis=0, keepdims=True))
            alpha = jnp.exp2(m_prev - m_new)
            p = jnp.exp2(s - m_new)
            l_scr[u] = alpha * l_scr[u] + jnp.sum(p, axis=0, keepdims=True)
            acc_scr[u] = alpha * acc_scr[u] + _dot(vt_ref[kv_of(u)], p)
            m_scr[u] = m_new

    def update_lagged(diagonal):
        gap = None
        for u in range(units):
            heads_u = range(u * hpu, (u + 1) * hpu)
            q_norm = jnp.concatenate([qn_ref[j:j + 1, :] for j in heads_u], axis=1)
            bound = q_norm * (kn_ref[kv_of(u), 0, 0:1, 0:1] * SEL_BOUND_MARGIN)
            gap = bound - m_scr[u] if gap is None else jnp.maximum(gap, bound - m_scr[u])
        safe = jnp.max(gap) <= SEL_LAG_MAX_GAP

        @pl.when(safe)
        def _():
            for u, s in masked_scores(diagonal):
                m_prev = m_scr[u]
                p = jnp.exp2(s - m_prev)
                m_new = jnp.maximum(m_prev, jnp.max(s, axis=0, keepdims=True))
                alpha = jnp.exp2(m_prev - m_new)
                l_scr[u] = (l_scr[u] + jnp.sum(p, axis=0, keepdims=True)) * alpha
                acc_scr[u] = (acc_scr[u] + _dot(vt_ref[kv_of(u)], p)) * alpha
                m_scr[u] = m_new

        @pl.when(jnp.logical_not(safe))
        def _():
            update(diagonal)

    def finalize():
        for j in range(heads):
            u, ls = j // hpu, slice((j % hpu) * tq, (j % hpu + 1) * tq)
            o_ref[j * d:(j + 1) * d, :] = acc_scr[u, :, ls] * (1.0 / l_scr[u, :, ls])

    @pl.when((kj == 0) & (kj < last))
    def _():
        update(False)

    @pl.when((kj == 0) & (kj == last))
    def _():
        update(True)
        finalize()

    @pl.when((kj > 0) & (kj < last))
    def _():
        update_lagged(False)

    @pl.when((kj > 0) & (kj == last))
    def _():
        update_lagged(True)
        finalize()


def nsa_sel_attn(qt, mb, ks_aug, vst, q_norm, k_norm_max, *, kv_heads, tq, tk, hpu):
    t = qt.shape[1]
    assert k_norm_max.shape[1] * tk == t
    d = HEAD_DIM
    gw = NSA_GROUP * d
    pairs = [(qi, kj) for qi in range(t // tq) for kj in range((qi * tq + tq - 1) // tk + 1)]
    qi_tab = jnp.asarray(np.array([p[0] for p in pairs], np.int32))
    kj_tab = jnp.asarray(np.array([p[1] for p in pairs], np.int32))
    units, uw = kv_heads * NSA_GROUP // hpu, hpu * tq
    grid_spec = pltpu.PrefetchScalarGridSpec(
        num_scalar_prefetch=2,
        grid=(len(pairs),),
        in_specs=[pl.BlockSpec((kv_heads * gw, tq), lambda s, qt, kt: (0, qt[s])),
                  pl.BlockSpec((kv_heads, SEL_LANES, tq), lambda s, qt, kt: (0, 0, qt[s])),
                  pl.BlockSpec((kv_heads, tk, 2 * d), lambda s, qt, kt: (0, kt[s], 0)),
                  pl.BlockSpec((kv_heads, d, tk), lambda s, qt, kt: (0, 0, kt[s])),
                  pl.BlockSpec((kv_heads * NSA_GROUP, tq), lambda s, qt, kt: (0, qt[s])),
                  pl.BlockSpec((kv_heads, 1, 8, d), lambda s, qt, kt: (0, kt[s], 0, 0))],
        out_specs=pl.BlockSpec((kv_heads * gw, tq), lambda s, qt, kt: (0, qt[s])),
        scratch_shapes=[pltpu.VMEM((units, 2 * d, uw), MXU_DTYPE), pltpu.VMEM((units, 1, uw), F32),
                        pltpu.VMEM((units, 1, uw), F32), pltpu.VMEM((units, d, uw), F32)],
    )
    return pl.pallas_call(
        functools.partial(_nsa_sel_kernel, tq=tq, tk=tk, kv_heads=kv_heads, grp=NSA_GROUP, hpu=hpu),
        grid_spec=grid_spec,
        out_shape=jax.ShapeDtypeStruct((kv_heads * gw, t), F32),
        compiler_params=_params("arbitrary"),
        name="nsa_sel_attn",
    )(qi_tab, kj_tab, qt, mb, ks_aug, vst, q_norm, k_norm_max)


def _nsa_win_kernel(qt_ref, k0_ref, k1_ref, k2_ref, v0_ref, v1_ref, v2_ref, ocmp_ref, osel_ref, zg_ref, y_ref,
                    *, tq, grp):
    d = HEAD_DIM
    qi = pl.program_id(1)
    k = jnp.concatenate([k0_ref[0], k1_ref[0], k2_ref[0]], axis=0)
    vt = jnp.concatenate([v0_ref[0], v1_ref[0], v2_ref[0]], axis=1)
    kpos = (qi - 2) * tq + lax.broadcasted_iota(jnp.int32, (3 * tq, tq), 0)
    rel = qi * tq + lax.broadcasted_iota(jnp.int32, (3 * tq, tq), 1) - kpos
    ok = (rel >= 0) & (rel < WINDOW) & (kpos >= 0)
    gates_t = _sigmoid(zg_ref[...]).T

    def scores(j):
        return jnp.dot(k, qt_ref[j * d:(j + 1) * d, :], preferred_element_type=F32)

    s_next = scores(0)
    for j in range(grp):
        rs = slice(j * d, (j + 1) * d)
        s = s_next
        if j + 1 < grp:
            s_next = scores(j + 1)
        sm = jnp.where(ok, s, NEG_INF)
        p = jnp.where(ok, jnp.exp2(sm - jnp.max(sm, axis=0, keepdims=True)), 0.0)
        o = _dot(vt, p) * (1.0 / jnp.sum(p, axis=0, keepdims=True))
        y = (gates_t[3 * j:3 * j + 1] * ocmp_ref[rs, :] + gates_t[3 * j + 1:3 * j + 2] * osel_ref[rs, :]
             + gates_t[3 * j + 2:3 * j + 3] * o)
        y_ref[:, rs] = y.T.astype(y_ref.dtype)


def nsa_win_combine(qt, kw, vwt, o_cmp, o_sel, zs, *, kv_heads, tq, zs_block0):
    t = qt.shape[1]
    d = HEAD_DIM
    gw = NSA_GROUP * d
    k_specs = [pl.BlockSpec((1, tq, d), functools.partial(lambda g, i, b: (g, jnp.maximum(i - b, 0), 0), b=b))
               for b in (2, 1, 0)]
    v_specs = [pl.BlockSpec((1, d, tq), functools.partial(lambda g, i, b: (g, 0, jnp.maximum(i - b, 0)), b=b))
               for b in (2, 1, 0)]
    tspec = pl.BlockSpec((gw, tq), lambda g, i: (g, i))
    return pl.pallas_call(
        functools.partial(_nsa_win_kernel, tq=tq, grp=NSA_GROUP),
        grid=(kv_heads, t // tq),
        in_specs=[tspec] + k_specs + v_specs + [tspec, tspec,
                                                pl.BlockSpec((tq, 128), lambda g, i: (i, zs_block0 + g))],
        out_specs=pl.BlockSpec((tq, gw), lambda g, i: (i, g)),
        out_shape=jax.ShapeDtypeStruct((t, kv_heads * gw), MXU_DTYPE),
        compiler_params=_params("parallel", "parallel"),
        name="nsa_win_combine",
    )(qt, kw, kw, kw, vwt, vwt, vwt, o_cmp, o_sel, zs)


def _split_in_proj(w_in, b_in, sizes):
    starts = np.concatenate([[0], np.cumsum(sizes)])
    i_gate, i_mli, i_mlf = 11, 16, 17
    g0, g1 = starts[i_gate], starts[i_gate + 1]
    half = (g1 - g0) // 2

    def pad(a, width):
        return jnp.pad(a, [(0, 0)] * (a.ndim - 1) + [(0, width - a.shape[-1])])

    def part_b(a):
        return jnp.concatenate([a[..., g1:starts[i_mli]], pad(a[..., g0:g0 + half], 128),
                                pad(a[..., g0 + half:g1], 128),
                                pad(a[..., starts[i_mli]:starts[i_mlf + 1]], 128)], axis=-1)

    return g0, part_b(w_in), part_b(b_in)


def kernel(x, mem, norm_mix, w_in, b_in, hgrn_lb_logits, hgrn_norm, nsa_q_norm, nsa_k_norm, nsa_cmp_pos, nsa_cmp_w, mlstm_conv, mlstm_norm, w_out, norm_xattn, norm_mem, xa_wq, xa_wk, xa_wv, xa_wo, xa_q_norm, xa_k_norm, norm_mlp, mlp_w1, mlp_w2):
    _, t, d_model = x.shape
    depth = w_in.shape[0]
    d = HEAD_DIM
    hg_heads = ml_heads = d_model // (4 * d)
    nsa_heads = d_model // (2 * d)
    kv_heads = nsa_heads // NSA_GROUP
    hw, nw, kvw = hg_heads * d, nsa_heads * d, kv_heads * d
    sizes = (hw,) * 4 + (nw,) + (kvw,) * 6 + (3 * nsa_heads,) + (hw,) * 4 + (ml_heads, ml_heads)
    col_nsa_q = 4 * hw
    col_nsa_kc = col_nsa_q + nw
    xa_heads = 4
    bf = MXU_DTYPE

    half = d // 2
    inv_freq = ROPE_THETA ** (-jnp.arange(half, dtype=F32) / half)

    def rope_tables(pos):
        ang = pos[:, None] * inv_freq[None, :]
        cos, sin = jnp.cos(ang), jnp.sin(ang)
        return jnp.concatenate([cos, cos], axis=-1), jnp.concatenate([-sin, sin], axis=-1)

    cos_t, sin_t = rope_tables(jnp.arange(t, dtype=F32))
    ncp = t // CMP_STRIDE
    n_cmp, n_sel = ncp - 1, t // SEL_BLOCK
    cos_c, sin_c = rope_tables(jnp.arange(ncp, dtype=F32) * CMP_STRIDE + (CMP_LEN - 1))
    et = jnp.asarray(np.arange(t)[:, None] // SEL_BLOCK == np.arange(SEL_LANES)[None, :], bf)
    c_start = np.arange(ncp)[:, None] * CMP_STRIDE
    s_start = np.arange(SEL_LANES)[None, :] * SEL_BLOCK
    overlap = ((c_start < s_start + SEL_BLOCK) & (c_start + CMP_LEN > s_start)
               & (np.arange(ncp)[:, None] < n_cmp) & (np.arange(SEL_LANES)[None, :] < n_sel))
    ovt = jnp.asarray(overlap.T, bf)

    n_a, w_b, b_b = _split_in_proj(w_in, b_in, sizes)
    w_a, w_b = w_in[:, :, :n_a].astype(bf), w_b.astype(bf)
    w_out_c, wq_c, wk_c, wv_c, wo_c = (a.astype(bf) for a in (w_out, xa_wq, xa_wk, xa_wv, xa_wo))
    w1_c, w2_c, cmp_w_c = mlp_w1.astype(bf), mlp_w2.astype(bf), nsa_cmp_w.astype(bf)

    h = x.reshape(t, d_model)
    mem2 = mem.reshape(mem.shape[1], d_model)
    hn = rms_cast(h, norm_mix[0], tm=512)
    for layer in range(depth):
        z = matmul_cols(hn, w_a, b_in, layer=layer, n_tiles=3, tm=512, tn=n_a // 3)
        zb = matmul_cols(hn, w_b, b_b, layer=layer, n_tiles=1, tm=512, tn=w_b.shape[2])
        gate_block = 4 * hw // 128
        gt = zb[:, (gate_block + 2) * 128:(gate_block + 2) * 128 + 8].T

        y_hg = hgrn2(z, hgrn_lb_logits, hgrn_norm[layer], layer=layer, heads=hg_heads, tb=512)
        y_ml = mlstm(zb, zb, gt, mlstm_conv[layer], mlstm_norm[layer], heads=ml_heads, tb=512, col0=0,
                     zs_block=gate_block + 2)

        qt, ks_aug, vst, kw, vwt, q_nrm, k_nrm = nsa_prep(z, cos_t, sin_t, et, nsa_q_norm[layer], nsa_k_norm[layer],
                                                          heads=nsa_heads, kv_heads=kv_heads, tq=512, col_q=col_nsa_q)
        kc, vct = nsa_compress(z, nsa_cmp_pos[layer], cmp_w_c[layer], nsa_k_norm[layer],
                               cos_c, sin_c, kv_heads=kv_heads, col_kc=col_nsa_kc)
        o_cmp, mb = nsa_cmp_select(qt, kc, vct, ovt, kv_heads=kv_heads, tq=256)
        o_sel = nsa_sel_attn(qt, mb, ks_aug, vst, q_nrm, k_nrm, kv_heads=kv_heads, tq=512, tk=512, hpu=1)
        y_ns = nsa_win_combine(qt, kw, vwt, o_cmp, o_sel, zb, kv_heads=kv_heads, tq=256, zs_block0=gate_block)

        kb = hw
        terms = [(y_hg, 0, 0)] + [(y_ns, c, 1 + c) for c in range(nw // kb)] + [(y_ml, 0, 1 + nw // kb)]
        h, hn = matmul_residual(terms, w_out_c, h, norm_xattn[layer], layer=layer, kb=kb, tm=512)

        k_mem, v_mem = xa_kv(mem2, norm_mem[layer], wk_c, wv_c, xa_k_norm[layer], layer=layer, heads=xa_heads)
        q_x = matmul_cols(hn, wq_c, None, layer=layer, n_tiles=1, tm=512, tn=d_model)
        o_x = xa_attn(q_x, xa_q_norm[layer], k_mem, v_mem, heads=xa_heads, tm=512)
        h = matmul_residual([(o_x, c, c) for c in range(d_model // kb)], wo_c, h, None, layer=layer, kb=kb, tm=512)

        if layer + 1 < depth:
            h, hn = mlp(h, norm_mlp[layer], w1_c, w2_c, norm_mix[layer + 1], layer=layer, tm=512, tf=1024)
        else:
            h = mlp(h, norm_mlp[layer], w1_c, w2_c, None, layer=layer, tm=512, tf=1024)
    return h.reshape(x.shape)
```

```python
import functools

import jax
import jax.numpy as jnp
import numpy as np
from jax import lax
from jax.experimental import pallas as pl
from jax.experimental.pallas import tpu as pltpu

F32 = jnp.float32
MXU_DTYPE = jnp.bfloat16

EPS = 1e-6
NEG_INF = -1e30
LOG2_E = 1.4426950408889634
HEAD_DIM = 128
ROPE_THETA = 10000.0

V7X_VMEM_LIMIT_BYTES = 56 * 1024 * 1024


def _params(*semantics):
    return pltpu.CompilerParams(dimension_semantics=semantics, vmem_limit_bytes=V7X_VMEM_LIMIT_BYTES)


def _rms(x, gain):
    return x * lax.rsqrt(jnp.mean(x * x, axis=-1, keepdims=True) + EPS) * gain


def _dot(a, b):
    return jnp.dot(a.astype(MXU_DTYPE), b.astype(MXU_DTYPE), preferred_element_type=F32)


def _dot_nt(a, b):
    return lax.dot_general(a.astype(MXU_DTYPE), b.astype(MXU_DTYPE), (((1,), (1,)), ((), ())),
                           preferred_element_type=F32)


def _dot_tn(a, b):
    return lax.dot_general(a.astype(MXU_DTYPE), b.astype(MXU_DTYPE), (((0,), (0,)), ((), ())),
                           preferred_element_type=F32)


def _rms_cast_kernel(x_ref, g_ref, o_ref):
    o_ref[...] = _rms(x_ref[...], g_ref[...]).astype(o_ref.dtype)


def rms_cast(x, gain, *, tm):
    m, d = x.shape
    return pl.pallas_call(
        _rms_cast_kernel,
        grid=(m // tm,),
        in_specs=[pl.BlockSpec((tm, d), lambda i: (i, 0)), pl.BlockSpec((1, d), lambda i: (0, 0))],
        out_specs=pl.BlockSpec((tm, d), lambda i: (i, 0)),
        out_shape=jax.ShapeDtypeStruct((m, d), MXU_DTYPE),
        compiler_params=_params("parallel"),
        name="rms_cast",
    )(x, gain.reshape(1, d))


def _cast_kernel(w_ref, o_ref):
    o_ref[...] = w_ref[...].astype(o_ref.dtype)


def cast_cols(w, *, n_cols, tk, tn):
    depth, k, _ = w.shape
    return pl.pallas_call(
        _cast_kernel,
        grid=(depth, k // tk, n_cols // tn),
        in_specs=[pl.BlockSpec((None, tk, tn), lambda l, i, j: (l, i, j))],
        out_specs=pl.BlockSpec((None, tk, tn), lambda l, i, j: (l, i, j)),
        out_shape=jax.ShapeDtypeStruct((depth, k, n_cols), MXU_DTYPE),
        compiler_params=_params("parallel", "parallel", "parallel"),
        name="cast_cols",
    )(w)


def _matmul_bias_kernel(h_ref, w_ref, b_ref, o_ref):
    o_ref[...] = jnp.dot(h_ref[...], w_ref[...], preferred_element_type=F32) + b_ref[...]


def _matmul_kernel(h_ref, w_ref, o_ref):
    o_ref[...] = jnp.dot(h_ref[...], w_ref[...], preferred_element_type=F32)


def matmul_cols(h, w, bias, *, layer, n_tiles, tm, tn):
    m, d = h.shape
    in_specs = [pl.BlockSpec((tm, d), lambda j, i: (i, 0)),
                pl.BlockSpec((None, d, tn), lambda j, i: (layer, 0, j))]
    args = [h, w]
    if bias is not None:
        in_specs.append(pl.BlockSpec((None, 1, tn), lambda j, i: (layer, 0, j)))
        args.append(bias.reshape(bias.shape[0], 1, bias.shape[1]))
    return pl.pallas_call(
        _matmul_kernel if bias is None else _matmul_bias_kernel,
        grid=(n_tiles, m // tm),
        in_specs=in_specs,
        out_specs=pl.BlockSpec((tm, tn), lambda j, i: (i, j)),
        out_shape=jax.ShapeDtypeStruct((m, n_tiles * tn), F32),
        compiler_params=_params("parallel", "parallel"),
        name="matmul_cols",
    )(*args)


def _matmul_res_kernel(*refs, n_terms, with_norm):
    ys, ws = refs[:n_terms], refs[n_terms:2 * n_terms]
    res_ref = refs[2 * n_terms]
    acc = res_ref[...]
    for y_ref, w_ref in zip(ys, ws):
        acc = acc + jnp.dot(y_ref[...].astype(MXU_DTYPE), w_ref[...], preferred_element_type=F32)
    if with_norm:
        g_ref, o_ref, hn_ref = refs[2 * n_terms + 1:]
        hn_ref[...] = _rms(acc, g_ref[...]).astype(hn_ref.dtype)
    else:
        o_ref = refs[2 * n_terms + 1]
    o_ref[...] = acc


def matmul_residual(terms, w, res, next_gain, *, layer, kb, tm):
    m, n = res.shape
    y_specs = [pl.BlockSpec((tm, kb), functools.partial(lambda i, c: (i, c), c=cb)) for _, cb, _ in terms]
    w_specs = [pl.BlockSpec((None, kb, n), functools.partial(lambda i, r: (layer, r, 0), r=rb)) for _, _, rb in terms]
    row_spec = pl.BlockSpec((tm, n), lambda i: (i, 0))
    with_norm = next_gain is not None
    return pl.pallas_call(
        functools.partial(_matmul_res_kernel, n_terms=len(terms), with_norm=with_norm),
        grid=(m // tm,),
        in_specs=y_specs + w_specs + [row_spec] + ([pl.BlockSpec((1, n), lambda i: (0, 0))] if with_norm else []),
        out_specs=[row_spec, row_spec] if with_norm else row_spec,
        out_shape=([jax.ShapeDtypeStruct((m, n), F32), jax.ShapeDtypeStruct((m, n), MXU_DTYPE)] if with_norm
                   else jax.ShapeDtypeStruct((m, n), F32)),
        compiler_params=_params("parallel"),
        name="matmul_residual",
    )(*[t[0] for t in terms], *([w] * len(terms)), res, *([next_gain.reshape(1, n)] if with_norm else []))


def _mlp_kernel(*refs, with_norm):
    if with_norm:
        x_ref, g_ref, w1_ref, w2_ref, gn_ref, o_ref, hn_ref, h_scr = refs
    else:
        x_ref, g_ref, w1_ref, w2_ref, o_ref, h_scr = refs

    @pl.when(pl.program_id(1) == 0)
    def _():
        x = x_ref[...]
        h_scr[...] = _rms(x, g_ref[...]).astype(h_scr.dtype)
        o_ref[...] = x

    u = jnp.dot(h_scr[...], w1_ref[...], preferred_element_type=F32)
    a = jnp.square(jnp.maximum(u, 0.0)).astype(MXU_DTYPE)
    o_ref[...] += jnp.dot(a, w2_ref[...], preferred_element_type=F32)

    if with_norm:
        @pl.when(pl.program_id(1) == pl.num_programs(1) - 1)
        def _():
            hn_ref[...] = _rms(o_ref[...], gn_ref[...]).astype(hn_ref.dtype)


def mlp(x, gain, w1, w2, next_gain, *, layer, tm, tf):
    m, d = x.shape
    ff = w1.shape[2]
    row_spec = pl.BlockSpec((tm, d), lambda i, f: (i, 0))
    gain_spec = pl.BlockSpec((1, d), lambda i, f: (0, 0))
    with_norm = next_gain is not None
    return pl.pallas_call(
        functools.partial(_mlp_kernel, with_norm=with_norm),
        grid=(m // tm, ff // tf),
        in_specs=[row_spec, gain_spec,
                  pl.BlockSpec((None, d, tf), lambda i, f: (layer, 0, f)),
                  pl.BlockSpec((None, tf, d), lambda i, f: (layer, f, 0))] + ([gain_spec] if with_norm else []),
        out_specs=[row_spec, row_spec] if with_norm else row_spec,
        out_shape=([jax.ShapeDtypeStruct((m, d), F32), jax.ShapeDtypeStruct((m, d), MXU_DTYPE)] if with_norm
                   else jax.ShapeDtypeStruct((m, d), F32)),
        scratch_shapes=[pltpu.VMEM((tm, d), MXU_DTYPE)],
        compiler_params=_params("parallel", "arbitrary"),
        name="mlp",
    )(x, gain.reshape(1, d), w1, w2, *([next_gain.reshape(1, d)] if with_norm else []))


def _xa_kv_kernel(mem_ref, g_ref, wk_ref, wv_ref, kg_ref, k_ref, v_ref):
    mn = _rms(mem_ref[...], g_ref[...])
    k_ref[...] = _rms(_dot(mn, wk_ref[...]), kg_ref[...]).astype(k_ref.dtype)
    v_ref[...] = _dot(mn, wv_ref[...]).astype(v_ref.dtype)


def xa_kv(mem, gain, wk, wv, k_gain, *, layer, heads):
    ml, d = mem.shape
    hd = d // heads
    return pl.pallas_call(
        _xa_kv_kernel,
        grid=(heads,),
        in_specs=[
            pl.BlockSpec((ml, d), lambda h: (0, 0)),
            pl.BlockSpec((1, d), lambda h: (0, 0)),
            pl.BlockSpec((None, d, hd), lambda h: (layer, 0, h)),
            pl.BlockSpec((None, d, hd), lambda h: (layer, 0, h)),
            pl.BlockSpec((1, hd), lambda h: (0, 0)),
        ],
        out_specs=[pl.BlockSpec((ml, hd), lambda h: (0, h))] * 2,
        out_shape=[jax.ShapeDtypeStruct((ml, d), MXU_DTYPE)] * 2,
        compiler_params=_params("parallel"),
        name="xa_kv",
    )(mem, gain.reshape(1, d), wk, wv, k_gain.reshape(1, hd))


def _xa_attn_kernel(q_ref, qg_ref, k_ref, v_ref, o_ref, *, heads, hd):
    scale = hd ** -0.5
    for h in range(heads):
        cs = slice(h * hd, (h + 1) * hd)
        q = _rms(q_ref[:, cs], qg_ref[...]) * scale
        s = _dot_nt(q, k_ref[:, cs])
        p = jnp.exp(s - jnp.max(s, axis=-1, keepdims=True))
        p = p / jnp.sum(p, axis=-1, keepdims=True)
        o_ref[:, cs] = _dot(p, v_ref[:, cs]).astype(o_ref.dtype)


def xa_attn(q, q_gain, k, v, *, heads, tm):
    m, d = q.shape
    ml = k.shape[0]
    hd = d // heads
    return pl.pallas_call(
        functools.partial(_xa_attn_kernel, heads=heads, hd=hd),
        grid=(m // tm,),
        in_specs=[
            pl.BlockSpec((tm, d), lambda i: (i, 0)),
            pl.BlockSpec((1, hd), lambda i: (0, 0)),
            pl.BlockSpec((ml, d), lambda i: (0, 0)),
            pl.BlockSpec((ml, d), lambda i: (0, 0)),
        ],
        out_specs=pl.BlockSpec((tm, d), lambda i: (i, 0)),
        out_shape=jax.ShapeDtypeStruct((m, d), MXU_DTYPE),
        compiler_params=_params("parallel"),
        name="xa_attn",
    )(q, q_gain.reshape(1, hd), k, v)


def _seg_cumsum_rows(x, seg):
    pos = lax.broadcasted_iota(jnp.int32, x.shape, 0) % seg
    k = 1
    while k < seg:
        x = x + jnp.where(pos >= k, pltpu.roll(x, k, axis=0), 0.0)
        k *= 2
    return x


def _seg_cumsum_lanes(x, seg):
    pos = lax.broadcasted_iota(jnp.int32, x.shape, 1) % seg
    k = 1
    while k < seg:
        x = x + jnp.where(pos >= k, pltpu.roll(x, k, axis=1), 0.0)
        k *= 2
    return x


def _sigmoid(x):
    return 1.0 / (1.0 + jnp.exp(-x))


def _log_sigmoid(x):
    return jnp.minimum(x, 0.0) - jnp.log(1.0 + jnp.exp(-jnp.abs(x)))


HG_SUB = 16
HG_MIN_FORGET = 1e-20


def _hgrn2_kernel(q_ref, f_ref, i_ref, g_ref, lbl_ref, ng_ref, y_ref, st_ref, *, layer, heads, tb):
    dk = HEAD_DIM

    @pl.when(pl.program_id(0) == 0)
    def _():
        st_ref[...] = jnp.zeros_like(st_ref)

    logits = lbl_ref[...]
    e = jnp.exp(logits - jnp.max(logits, axis=0, keepdims=True))
    probs = e / jnp.sum(e, axis=0, keepdims=True)
    lb_all = jnp.zeros_like(probs[0:1])
    for l in range(1, layer + 1):
        lb_all = lb_all + probs[l:l + 1]
    half = HG_SUB // 2
    row_half = lax.broadcasted_iota(jnp.int32, (half, dk), 0)

    n_sub = tb // HG_SUB

    def gates(c):
        r = pl.ds(pl.multiple_of(c * HG_SUB, HG_SUB), HG_SUB)
        pre = []
        for h in range(heads):
            cs = slice(h * dk, (h + 1) * dk)
            lb = lb_all[:, cs]
            q, zf, v = q_ref[r, cs], f_ref[r, cs], i_ref[r, cs]
            forget = lb + (1.0 - lb) * _sigmoid(zf)
            log_f = jnp.log(jnp.maximum(forget, HG_MIN_FORGET))
            key = (1.0 - lb) * _sigmoid(-zf)
            b = _seg_cumsum_rows(log_f, HG_SUB) * LOG2_E
            pre.append((q, v, key, b))
        return pre

    def body(c, pre):
        r = pl.ds(pl.multiple_of(c * HG_SUB, HG_SUB), HG_SUB)
        pre_next = gates(jnp.minimum(c + 1, n_sub - 1))
        pre = [(q, v, key, b, b[HG_SUB - 1:HG_SUB]) for q, v, key, b in pre]
        inter = [_dot_nt(q * jnp.exp2(b), st_ref[h]) for h, (q, v, key, b, b_end) in enumerate(pre)]
        upd = [_dot_tn(v, key * jnp.exp2(b_end - b)) for q, v, key, b, b_end in pre]
        for h in range(heads):
            cs = slice(h * dk, (h + 1) * dk)
            q, v, key, b, b_end = pre[h]
            q_lo, q_hi, b_lo, b_hi = q[0:half], q[half:], b[0:half], b[half:]
            o_lo = jnp.zeros((half, dk), F32)
            o_hi = jnp.zeros((half, dk), F32)
            for s in range(HG_SUB):
                bs, ks, vs = b[s:s + 1], key[s:s + 1], v[s:s + 1]
                if s < half:
                    decay = jnp.exp2(jnp.where(row_half >= s, b_lo - bs, NEG_INF))
                    o_lo = o_lo + jnp.sum(q_lo * decay * ks, axis=-1, keepdims=True) * vs
                    decay = jnp.exp2(b_hi - bs)
                else:
                    decay = jnp.exp2(jnp.where(row_half >= s - half, b_hi - bs, NEG_INF))
                o_hi = o_hi + jnp.sum(q_hi * decay * ks, axis=-1, keepdims=True) * vs
            o = jnp.concatenate([o_lo, o_hi], axis=0) + inter[h]
            st_ref[h] = jnp.exp2(b_end) * st_ref[h] + upd[h]
            g = g_ref[r, cs]
            y = _rms(o, ng_ref[:, cs]) * (g * _sigmoid(g))
            y_ref[r, cs] = y.astype(y_ref.dtype)
        return pre_next

    lax.fori_loop(0, n_sub, body, gates(jnp.int32(0)))


def hgrn2(z, lb_logits, norm_gain, *, layer, heads, tb):
    t = z.shape[0]
    w = heads * HEAD_DIM
    depth = lb_logits.shape[0]
    return pl.pallas_call(
        functools.partial(_hgrn2_kernel, layer=layer, heads=heads, tb=tb),
        grid=(t // tb,),
        in_specs=[pl.BlockSpec((tb, w), functools.partial(lambda i, c: (i, c), c=c)) for c in range(4)] + [
            pl.BlockSpec((depth, w), lambda i: (0, 0)),
            pl.BlockSpec((1, w), lambda i: (0, 0)),
        ],
        out_specs=pl.BlockSpec((tb, w), lambda i: (i, 0)),
        out_shape=jax.ShapeDtypeStruct((t, w), MXU_DTYPE),
        scratch_shapes=[pltpu.VMEM((heads, HEAD_DIM, HEAD_DIM), F32)],
        compiler_params=_params("arbitrary"),
        name="hgrn2",
    )(z, z, z, z, lb_logits, norm_gain.reshape(1, w))


ML_CHUNK = 64
ML_CONV = 4
ML_TAIL = 8
ML_AUG = HEAD_DIM + 16
ZS_ML_I = 0
ZS_ML_F = 4


def _mlstm_kernel(q_ref, k_ref, v_ref, o_ref, zs_ref, gt_ref, cw_ref, ng_ref, y_ref,
                  conv_scr, st_scr, m_scr, *, heads, tb):
    d = HEAD_DIM
    w = heads * d

    @pl.when(pl.program_id(0) == 0)
    def _():
        conv_scr[0:ML_TAIL, :] = jnp.zeros((ML_TAIL, 2 * w), F32)
        st_scr[...] = jnp.zeros_like(st_scr)
        m_scr[...] = jnp.zeros_like(m_scr)

    conv_scr[ML_TAIL:ML_TAIL + tb, 0:w] = q_ref[...]
    conv_scr[ML_TAIL:ML_TAIL + tb, w:2 * w] = k_ref[...]
    acc = jnp.zeros((tb, 2 * w), F32)
    for j in range(ML_CONV):
        off = ML_TAIL - (ML_CONV - 1) + j
        acc = acc + conv_scr[off:off + tb, :] * cw_ref[j:j + 1, :]
    conv_scr[0:ML_TAIL, :] = conv_scr[tb:tb + ML_TAIL, :]
    qk = acc * _sigmoid(acc)

    pair = 2 * ML_CHUNK
    zs = zs_ref[...]
    b_col_all = _seg_cumsum_rows(_log_sigmoid(zs), ML_CHUNK)
    gt = gt_ref[...]
    b_row_all = _seg_cumsum_lanes(_log_sigmoid(gt), ML_CHUNK)
    s_idx = lax.broadcasted_iota(jnp.int32, (pair, pair), 0)
    t_idx = lax.broadcasted_iota(jnp.int32, (pair, pair), 1)
    visible = (s_idx // ML_CHUNK == t_idx // ML_CHUNK) & (s_idx <= t_idx)
    lane = lax.broadcasted_iota(jnp.int32, (1, pair), 1)
    in_chunk = [lane < ML_CHUNK, lane >= ML_CHUNK]
    ones_rows = (lax.broadcasted_iota(jnp.int32, (ML_AUG - d, pair), 0) == 0).astype(F32)

    def rows(p):
        return slice(p * pair, (p + 1) * pair)

    def stage_scores(p):
        out = []
        for h in range(heads):
            qt = (qk[rows(p), h * d:(h + 1) * d] * (d ** -0.5)).T
            out.append((qt, _dot(qk[rows(p), w + h * d:w + (h + 1) * d], qt)))
        return out

    def stage_free(p, sc):
        rs = rows(p)
        out = []
        for h in range(heads):
            qt, s_raw = sc[h]
            b_row = b_row_all[heads + h:heads + h + 1, rs]
            li_row = gt[h:h + 1, rs]
            u_col = b_col_all[rs, ZS_ML_F + h:ZS_ML_F + h + 1] - zs[rs, ZS_ML_I + h:ZS_ML_I + h + 1]
            d_log = jnp.where(visible, b_row - u_col, NEG_INF)
            dmax = jnp.max(d_log, axis=0, keepdims=True)
            s_t = s_raw * jnp.exp(d_log - dmax)
            vt_aug = jnp.concatenate([v_ref[rs, h * d:(h + 1) * d].T, ones_rows], axis=0)
            kc = qk[rs, w + h * d:w + (h + 1) * d]
            per_chunk = []
            for c in range(2):
                b_end = b_row[:, (c + 1) * ML_CHUNK - 1:(c + 1) * ML_CHUNK]
                w_state = jnp.where(in_chunk[c], b_end - b_row + li_row, NEG_INF)
                wmax = jnp.max(w_state, axis=1, keepdims=True)
                w_row = jnp.exp(w_state - wmax)
                per_chunk.append((b_end, wmax, _dot(vt_aug * w_row, kc)))
            out.append(dict(qt=qt, b_row=b_row, dmax=dmax, intra=_dot(vt_aug, s_t), per_chunk=per_chunk))
        return out

    n_pairs = tb // pair
    sc = {0: stage_scores(0)}
    if n_pairs > 1:
        sc[1] = stage_scores(1)
    free = {0: stage_free(0, sc.pop(0))}
    for p in range(n_pairs):
        rs = rows(p)
        if p + 2 < n_pairs:
            sc[p + 2] = stage_scores(p + 2)
        if p + 1 < n_pairs:
            free[p + 1] = stage_free(p + 1, sc.pop(p + 1))
        cur = free.pop(p)
        for h in range(heads):
            cs = slice(h * d, (h + 1) * d)
            f = cur[h]
            num, m_tok = None, None
            for c in range(2):
                b_end, wmax, upd = f["per_chunk"][c]
                state, m_prev = st_scr[h], m_scr[h]
                inter = _dot(state, f["qt"])
                inter_log = f["b_row"] + m_prev
                m_t = jnp.maximum(inter_log, f["dmax"])
                num_c = jnp.exp(inter_log - m_t) * inter + jnp.exp(f["dmax"] - m_t) * f["intra"]
                num = num_c if c == 0 else jnp.where(in_chunk[0], num, num_c)
                m_tok = m_t if c == 0 else jnp.where(in_chunk[0], m_tok, m_t)
                m_new = jnp.maximum(b_end + m_prev, wmax)
                st_scr[h] = jnp.exp(b_end + m_prev - m_new) * state + jnp.exp(wmax - m_new) * upd
                m_scr[h] = m_new
            qn = num[d:d + 1]
            hcell = num[0:d] / jnp.maximum(jnp.abs(qn), jnp.exp(-m_tok))
            hn = hcell * lax.rsqrt(jnp.mean(hcell * hcell, axis=0, keepdims=True) + EPS)
            y = hn.T * ng_ref[:, cs] * _sigmoid(o_ref[rs, cs])
            y_ref[rs, cs] = y.astype(y_ref.dtype)


def mlstm(z, zs, gt, conv_w, norm_gain, *, heads, tb, col0, zs_block):
    t = z.shape[0]
    w = heads * HEAD_DIM
    cb = col0 // w
    return pl.pallas_call(
        functools.partial(_mlstm_kernel, heads=heads, tb=tb),
        grid=(t // tb,),
        in_specs=[pl.BlockSpec((tb, w), functools.partial(lambda i, c: (i, c), c=cb + c)) for c in range(4)] + [
            pl.BlockSpec((tb, 128), lambda i: (i, zs_block)),
            pl.BlockSpec((8, tb), lambda i: (0, i)),
            pl.BlockSpec((ML_CONV, 2 * w), lambda i: (0, 0)),
            pl.BlockSpec((1, w), lambda i: (0, 0)),
        ],
        out_specs=pl.BlockSpec((tb, w), lambda i: (i, 0)),
        out_shape=jax.ShapeDtypeStruct((t, w), MXU_DTYPE),
        scratch_shapes=[
            pltpu.VMEM((tb + ML_TAIL, 2 * w), F32),
            pltpu.VMEM((heads, ML_AUG, HEAD_DIM), F32),
            pltpu.VMEM((heads, 1, 1), F32),
        ],
        compiler_params=_params("arbitrary"),
        name="mlstm",
    )(z, z, z, z, zs, gt, conv_w, norm_gain.reshape(1, w))


NSA_GROUP = 4
CMP_LEN = 32
CMP_STRIDE = 16
SEL_BLOCK = 64
N_SELECT = 16
WINDOW = 512
FORCE_BONUS = 1e4
SEL_LANES = 128
SEL_LAG_MAX_GAP = 64.0
SEL_BOUND_MARGIN = 1.02


def _rope(x, cos_t, sin_t):
    return x * cos_t + pltpu.roll(x, HEAD_DIM // 2, axis=1) * sin_t


def _nsa_prep_kernel(q_ref, ks_ref, vs_ref, kw_ref, vw_ref, cos_ref, sin_ref, et_ref, qg_ref, kg_ref,
                     qh_ref, ksa_ref, vso_ref, kwr_ref, vwo_ref, qn_ref, kn_ref, *, heads, kv_heads):
    d = HEAD_DIM
    cos_t, sin_t = cos_ref[...], sin_ref[...]
    for h in range(heads):
        cs = slice(h * d, (h + 1) * d)
        qt = (_rope(_rms(q_ref[:, cs], qg_ref[...]), cos_t, sin_t) * (d ** -0.5 * LOG2_E)).T
        qh_ref[cs, :] = qt.astype(qh_ref.dtype)
        qn_ref[h:h + 1, :] = jnp.sqrt(jnp.sum(qt * qt, axis=0, keepdims=True))
    for g in range(kv_heads):
        cs = slice(g * d, (g + 1) * d)
        ks = _rope(_rms(ks_ref[:, cs], kg_ref[1:2]), cos_t, sin_t)
        ksa_ref[g, :, 0:d] = ks.astype(ksa_ref.dtype)
        k_norm2 = jnp.max(jnp.sum(ks * ks, axis=-1, keepdims=True), axis=0, keepdims=True)
        kn_ref[g, 0] = jnp.broadcast_to(jnp.sqrt(k_norm2), kn_ref.shape[2:])
        ksa_ref[g, :, d:2 * d] = et_ref[...]
        kwr_ref[g] = _rope(_rms(kw_ref[:, cs], kg_ref[2:3]), cos_t, sin_t).astype(kwr_ref.dtype)
        vso_ref[g] = vs_ref[:, cs].T.astype(vso_ref.dtype)
        vwo_ref[g] = vw_ref[:, cs].T.astype(vwo_ref.dtype)


def nsa_prep(z, cos_t, sin_t, et, q_gain, k_gains, *, heads, kv_heads, tq, col_q):
    t = z.shape[0]
    d = HEAD_DIM
    kvw = kv_heads * d
    qb = col_q // (heads * d)
    kb = (col_q + heads * d) // kvw

    def zcol(width, c):
        return pl.BlockSpec((tq, width), functools.partial(lambda i, c: (i, c), c=c))

    k_out = pl.BlockSpec((kv_heads, tq, d), lambda i: (0, i, 0))
    k_shape = jax.ShapeDtypeStruct((kv_heads, t, d), MXU_DTYPE)
    vt_out = pl.BlockSpec((kv_heads, d, tq), lambda i: (0, 0, i))
    vt_shape = jax.ShapeDtypeStruct((kv_heads, d, t), MXU_DTYPE)
    return pl.pallas_call(
        functools.partial(_nsa_prep_kernel, heads=heads, kv_heads=kv_heads),
        grid=(t // tq,),
        in_specs=[zcol(heads * d, qb), zcol(kvw, kb + 2), zcol(kvw, kb + 3), zcol(kvw, kb + 4), zcol(kvw, kb + 5),
                  pl.BlockSpec((tq, d), lambda i: (i, 0)), pl.BlockSpec((tq, d), lambda i: (i, 0)),
                  pl.BlockSpec((tq, SEL_LANES), lambda i: (i, 0)),
                  pl.BlockSpec((1, d), lambda i: (0, 0)), pl.BlockSpec((3, d), lambda i: (0, 0))],
        out_specs=[pl.BlockSpec((heads * d, tq), lambda i: (0, i)),
                   pl.BlockSpec((kv_heads, tq, 2 * d), lambda i: (0, i, 0)), vt_out, k_out, vt_out,
                   pl.BlockSpec((heads, tq), lambda i: (0, i)),
                   pl.BlockSpec((kv_heads, 1, 8, d), lambda i: (0, i, 0, 0))],
        out_shape=[jax.ShapeDtypeStruct((heads * d, t), MXU_DTYPE),
                   jax.ShapeDtypeStruct((kv_heads, t, 2 * d), MXU_DTYPE), vt_shape, k_shape, vt_shape,
                   jax.ShapeDtypeStruct((heads, t), F32),
                   jax.ShapeDtypeStruct((kv_heads, t // tq, 8, d), F32)],
        compiler_params=_params("parallel"),
        name="nsa_prep",
    )(z, z, z, z, z, cos_t, sin_t, et, q_gain.reshape(1, d), k_gains)


def _nsa_compress_kernel(zk_ref, zv_ref, pe_ref, w_ref, kg_ref, cos_ref, sin_ref, kc_ref, vc_ref, *, ncp):
    d = HEAD_DIM
    half = CMP_LEN // 2
    row = lax.broadcasted_iota(jnp.int32, (ncp, d), 0)

    def compress(z_ref, which):
        lo = jnp.zeros((ncp, d), F32)
        hi = jnp.zeros((ncp, d), F32)
        for l in range(half):
            xl = z_ref[pl.ds(l, ncp, stride=CMP_STRIDE), :]
            lo = lo + _dot(xl + pe_ref[which, l:l + 1, :], w_ref[which, l])
            hi = hi + _dot(xl + pe_ref[which, half + l:half + l + 1, :], w_ref[which, half + l])
        out = lo + pltpu.roll(hi, ncp - 1, axis=0)
        return jnp.where(row < ncp - 1, out, 0.0)

    kc_ref[0] = _rope(_rms(compress(zk_ref, 0), kg_ref[0:1]), cos_ref[...], sin_ref[...]).astype(kc_ref.dtype)
    vc_ref[0] = compress(zv_ref, 1).T.astype(vc_ref.dtype)


def nsa_compress(z, cmp_pos, cmp_w, k_gains, cos_c, sin_c, *, kv_heads, col_kc):
    t = z.shape[0]
    d = HEAD_DIM
    ncp = t // CMP_STRIDE
    kb = col_kc // d
    out_spec = pl.BlockSpec((1, ncp, d), lambda g: (g, 0, 0))
    out_shape = jax.ShapeDtypeStruct((kv_heads, ncp, d), MXU_DTYPE)
    return pl.pallas_call(
        functools.partial(_nsa_compress_kernel, ncp=ncp),
        grid=(kv_heads,),
        in_specs=[pl.BlockSpec((t, d), lambda g: (0, kb + g)),
                  pl.BlockSpec((t, d), lambda g: (0, kb + kv_heads + g)),
                  pl.BlockSpec((2, CMP_LEN, d), lambda g: (0, 0, 0)),
                  pl.BlockSpec((2, CMP_LEN, d, d), lambda g: (0, 0, 0, 0)),
                  pl.BlockSpec((3, d), lambda g: (0, 0)),
                  pl.BlockSpec((ncp, d), lambda g: (0, 0)), pl.BlockSpec((ncp, d), lambda g: (0, 0))],
        out_specs=[out_spec, pl.BlockSpec((1, d, ncp), lambda g: (g, 0, 0))],
        out_shape=[out_shape, jax.ShapeDtypeStruct((kv_heads, d, ncp), MXU_DTYPE)],
        compiler_params=_params("parallel"),
        name="nsa_compress",
    )(z, z, cmp_pos, cmp_w, k_gains, cos_c, sin_c)


def _nsa_cmp_kernel(qt_ref, kc_ref, vct_ref, ovt_ref, ocmp_ref, mb_ref, *, tq, grp, ncp):
    d = HEAD_DIM
    qi = pl.program_id(1)
    cmp_end = lax.broadcasted_iota(jnp.int32, (ncp, tq), 0) * CMP_STRIDE + (CMP_LEN - 1)
    vis = cmp_end <= qi * tq + lax.broadcasted_iota(jnp.int32, (ncp, tq), 1)
    psum = None

    def scores(j):
        return jnp.dot(kc_ref[0], qt_ref[j * d:(j + 1) * d, :], preferred_element_type=F32)

    s_next = scores(0)
    for j in range(grp):
        s = s_next
        if j + 1 < grp:
            s_next = scores(j + 1)
        sm = jnp.where(vis, s, NEG_INF)
        p = jnp.where(vis, jnp.exp2(sm - jnp.max(sm, axis=0, keepdims=True)), 0.0)
        l = jnp.sum(p, axis=0, keepdims=True)
        p = p * (1.0 / jnp.where(l > 0.0, l, 1.0))
        ocmp_ref[j * d:(j + 1) * d, :] = _dot(vct_ref[0], p)
        psum = p if psum is None else psum + p
    hi = psum.astype(MXU_DTYPE)
    lo = (psum - hi.astype(F32)).astype(MXU_DTYPE)
    imp = (jnp.dot(ovt_ref[...], hi, preferred_element_type=F32)
           + jnp.dot(ovt_ref[...], lo, preferred_element_type=F32))
    blk = lax.broadcasted_iota(jnp.int32, (SEL_LANES, tq), 0)
    cur = (qi * tq + lax.broadcasted_iota(jnp.int32, (SEL_LANES, tq), 1)) // SEL_BLOCK
    forced = (blk == 0) | (blk == cur) | (blk == cur - 1)
    score = jnp.where(blk <= cur, imp + jnp.where(forced, FORCE_BONUS, 0.0), NEG_INF)
    blk_f = blk.astype(F32)
    bias = jnp.full((SEL_LANES, tq), NEG_INF, F32)
    for _ in range(N_SELECT):
        mx = jnp.max(score, axis=0, keepdims=True)
        first = jnp.min(jnp.where(score == mx, blk_f, float(SEL_LANES)), axis=0, keepdims=True)
        pick = blk_f == first
        bias = jnp.where(pick, jnp.where(mx > 0.5 * NEG_INF, 0.0, NEG_INF), bias)
        score = jnp.where(pick, -jnp.inf, score)
    mb_ref[0] = bias.astype(mb_ref.dtype)


def nsa_cmp_select(qt, kc, vct, ovt, *, kv_heads, tq):
    t = qt.shape[1]
    d = HEAD_DIM
    ncp = kc.shape[1]
    gw = NSA_GROUP * d
    return pl.pallas_call(
        functools.partial(_nsa_cmp_kernel, tq=tq, grp=NSA_GROUP, ncp=ncp),
        grid=(kv_heads, t // tq),
        in_specs=[pl.BlockSpec((gw, tq), lambda g, i: (g, i)),
                  pl.BlockSpec((1, ncp, d), lambda g, i: (g, 0, 0)),
                  pl.BlockSpec((1, d, ncp), lambda g, i: (g, 0, 0)),
                  pl.BlockSpec((SEL_LANES, ncp), lambda g, i: (0, 0))],
        out_specs=[pl.BlockSpec((gw, tq), lambda g, i: (g, i)),
                   pl.BlockSpec((1, SEL_LANES, tq), lambda g, i: (g, 0, i))],
        out_shape=[jax.ShapeDtypeStruct((kv_heads * gw, t), F32),
                   jax.ShapeDtypeStruct((kv_heads, SEL_LANES, t), MXU_DTYPE)],
        compiler_params=_params("parallel", "parallel"),
        name="nsa_cmp_select",
    )(qt, kc, vct, ovt)


def _nsa_sel_kernel(qi_tab, kj_tab, qt_ref, mb_ref, ka_ref, vt_ref, qn_ref, kn_ref, o_ref,
                    qa_scr, m_scr, l_scr, acc_scr,
                    *, tq, tk, kv_heads, grp, hpu):
    d = HEAD_DIM
    step = pl.program_id(0)
    qi, kj = qi_tab[step], kj_tab[step]
    last = (qi * tq + tq - 1) // tk
    heads = kv_heads * grp
    units = heads // hpu
    uw = hpu * tq

    def kv_of(u):
        return u * hpu // grp

    @pl.when(kj == 0)
    def _():
        for j in range(heads):
            u, ls = j // hpu, slice((j % hpu) * tq, (j % hpu + 1) * tq)
            qa_scr[u, 0:d, ls] = qt_ref[j * d:(j + 1) * d, :]
            qa_scr[u, d:2 * d, ls] = mb_ref[j // grp]
        m_scr[...] = jnp.full_like(m_scr, NEG_INF)
        l_scr[...] = jnp.zeros_like(l_scr)
        acc_scr[...] = jnp.zeros_like(acc_scr)

    def masked_scores(diagonal):
        if diagonal:
            causal = (kj * tk + lax.broadcasted_iota(jnp.int32, (tk, uw), 0)
                      <= qi * tq + lax.broadcasted_iota(jnp.int32, (tk, uw), 1) % tq)

        def scores(u):
            return jnp.dot(ka_ref[kv_of(u)], qa_scr[u], preferred_element_type=F32)

        s_next = scores(0)
        for u in range(units):
            s = s_next
            if u + 1 < units:
                s_next = scores(u + 1)
            yield u, (jnp.where(causal, s, NEG_INF) if diagonal else s)

    def update(diagonal):
        for u, s in masked_scores(diagonal):
            m_prev = m_scr[u]
            m_new = jnp.maximum(m_prev, jnp.max(s, axis=0, keepdims=True))
            alpha = jnp.exp2(m_prev - m_new)
            p = jnp.exp2(s - m_new)
            l_scr[u] = alpha * l_scr[u] + jnp.sum(p, axis=0, keepdims=True)
            acc_scr[u] = alpha * acc_scr[u] + _dot(vt_ref[kv_of(u)], p)
            m_scr[u] = m_new

    def update_lagged(diagonal):
        gap = None
        for u in range(units):
            heads_u = range(u * hpu, (u + 1) * hpu)
            q_norm = jnp.concatenate([qn_ref[j:j + 1, :] for j in heads_u], axis=1)
            bound = q_norm * (kn_ref[kv_of(u), 0, 0:1, 0:1] * SEL_BOUND_MARGIN)
            gap = bound - m_scr[u] if gap is None else jnp.maximum(gap, bound - m_scr[u])
        safe = jnp.max(gap) <= SEL_LAG_MAX_GAP

        @pl.when(safe)
        def _():
            for u, s in masked_scores(diagonal):
                m_prev = m_scr[u]
                p = jnp.exp2(s - m_prev)
                m_new = jnp.maximum(m_prev, jnp.max(s, axis=0, keepdims=True))
                alpha = jnp.exp2(m_prev - m_new)
                l_scr[u] = (l_scr[u] + jnp.sum(p, axis=0, keepdims=True)) * alpha
                acc_scr[u] = (acc_scr[u] + _dot(vt_ref[kv_of(u)], p)) * alpha
                m_scr[u] = m_new

        @pl.when(jnp.logical_not(safe))
        def _():
            update(diagonal)

    def finalize():
        for j in range(heads):
            u, ls = j // hpu, slice((j % hpu) * tq, (j % hpu + 1) * tq)
            o_ref[j * d:(j + 1) * d, :] = acc_scr[u, :, ls] * (1.0 / l_scr[u, :, ls])

    @pl.when((kj == 0) & (kj < last))
    def _():
        update(False)

    @pl.when((kj == 0) & (kj == last))
    def _():
        update(True)
        finalize()

    @pl.when((kj > 0) & (kj < last))
    def _():
        update_lagged(False)

    @pl.when((kj > 0) & (kj == last))
    def _():
        update_lagged(True)
        finalize()


def nsa_sel_attn(qt, mb, ks_aug, vst, q_norm, k_norm_max, *, kv_heads, tq, tk, hpu):
    t = qt.shape[1]
    assert k_norm_max.shape[1] * tk == t
    d = HEAD_DIM
    gw = NSA_GROUP * d
    pairs = [(qi, kj) for qi in range(t // tq) for kj in range((qi * tq + tq - 1) // tk + 1)]
    qi_tab = jnp.asarray(np.array([p[0] for p in pairs], np.int32))
    kj_tab = jnp.asarray(np.array([p[1] for p in pairs], np.int32))
    units, uw = kv_heads * NSA_GROUP // hpu, hpu * tq
    grid_spec = pltpu.PrefetchScalarGridSpec(
        num_scalar_prefetch=2,
        grid=(len(pairs),),
        in_specs=[pl.BlockSpec((kv_heads * gw, tq), lambda s, qt, kt: (0, qt[s])),
                  pl.BlockSpec((kv_heads, SEL_LANES, tq), lambda s, qt, kt: (0, 0, qt[s])),
                  pl.BlockSpec((kv_heads, tk, 2 * d), lambda s, qt, kt: (0, kt[s], 0)),
                  pl.BlockSpec((kv_heads, d, tk), lambda s, qt, kt: (0, 0, kt[s])),
                  pl.BlockSpec((kv_heads * NSA_GROUP, tq), lambda s, qt, kt: (0, qt[s])),
                  pl.BlockSpec((kv_heads, 1, 8, d), lambda s, qt, kt: (0, kt[s], 0, 0))],
        out_specs=pl.BlockSpec((kv_heads * gw, tq), lambda s, qt, kt: (0, qt[s])),
        scratch_shapes=[pltpu.VMEM((units, 2 * d, uw), MXU_DTYPE), pltpu.VMEM((units, 1, uw), F32),
                        pltpu.VMEM((units, 1, uw), F32), pltpu.VMEM((units, d, uw), F32)],
    )
    return pl.pallas_call(
        functools.partial(_nsa_sel_kernel, tq=tq, tk=tk, kv_heads=kv_heads, grp=NSA_GROUP, hpu=hpu),
        grid_spec=grid_spec,
        out_shape=jax.ShapeDtypeStruct((kv_heads * gw, t), F32),
        compiler_params=_params("arbitrary"),
        name="nsa_sel_attn",
    )(qi_tab, kj_tab, qt, mb, ks_aug, vst, q_norm, k_norm_max)


def _nsa_win_kernel(qt_ref, k0_ref, k1_ref, k2_ref, v0_ref, v1_ref, v2_ref, ocmp_ref, osel_ref, zg_ref, y_ref,
                    *, tq, grp):
    d = HEAD_DIM
    qi = pl.program_id(1)
    k = jnp.concatenate([k0_ref[0], k1_ref[0], k2_ref[0]], axis=0)
    vt = jnp.concatenate([v0_ref[0], v1_ref[0], v2_ref[0]], axis=1)
    kpos = (qi - 2) * tq + lax.broadcasted_iota(jnp.int32, (3 * tq, tq), 0)
    rel = qi * tq + lax.broadcasted_iota(jnp.int32, (3 * tq, tq), 1) - kpos
    ok = (rel >= 0) & (rel < WINDOW) & (kpos >= 0)
    gates_t = _sigmoid(zg_ref[...]).T

    def scores(j):
        return jnp.dot(k, qt_ref[j * d:(j + 1) * d, :], preferred_element_type=F32)

    s_next = scores(0)
    for j in range(grp):
        rs = slice(j * d, (j + 1) * d)
        s = s_next
        if j + 1 < grp:
            s_next = scores(j + 1)
        sm = jnp.where(ok, s, NEG_INF)
        p = jnp.where(ok, jnp.exp2(sm - jnp.max(sm, axis=0, keepdims=True)), 0.0)
        o = _dot(vt, p) * (1.0 / jnp.sum(p, axis=0, keepdims=True))
        y = (gates_t[3 * j:3 * j + 1] * ocmp_ref[rs, :] + gates_t[3 * j + 1:3 * j + 2] * osel_ref[rs, :]
             + gates_t[3 * j + 2:3 * j + 3] * o)
        y_ref[:, rs] = y.T.astype(y_ref.dtype)


def nsa_win_combine(qt, kw, vwt, o_cmp, o_sel, zs, *, kv_heads, tq, zs_block0):
    t = qt.shape[1]
    d = HEAD_DIM
    gw = NSA_GROUP * d
    k_specs = [pl.BlockSpec((1, tq, d), functools.partial(lambda g, i, b: (g, jnp.maximum(i - b, 0), 0), b=b))
               for b in (2, 1, 0)]
    v_specs = [pl.BlockSpec((1, d, tq), functools.partial(lambda g, i, b: (g, 0, jnp.maximum(i - b, 0)), b=b))
               for b in (2, 1, 0)]
    tspec = pl.BlockSpec((gw, tq), lambda g, i: (g, i))
    return pl.pallas_call(
        functools.partial(_nsa_win_kernel, tq=tq, grp=NSA_GROUP),
        grid=(kv_heads, t // tq),
        in_specs=[tspec] + k_specs + v_specs + [tspec, tspec,
                                                pl.BlockSpec((tq, 128), lambda g, i: (i, zs_block0 + g))],
        out_specs=pl.BlockSpec((tq, gw), lambda g, i: (i, g)),
        out_shape=jax.ShapeDtypeStruct((t, kv_heads * gw), MXU_DTYPE),
        compiler_params=_params("parallel", "parallel"),
        name="nsa_win_combine",
    )(qt, kw, kw, kw, vwt, vwt, vwt, o_cmp, o_sel, zs)


def _split_in_proj(w_in, b_in, sizes):
    starts = np.concatenate([[0], np.cumsum(sizes)])
    i_gate, i_mli, i_mlf = 11, 16, 17
    g0, g1 = starts[i_gate], starts[i_gate + 1]
    half = (g1 - g0) // 2

    def pad(a, width):
        return jnp.pad(a, [(0, 0)] * (a.ndim - 1) + [(0, width - a.shape[-1])])

    def part_b(a):
        return jnp.concatenate([a[..., g1:starts[i_mli]], pad(a[..., g0:g0 + half], 128),
                                pad(a[..., g0 + half:g1], 128),
                                pad(a[..., starts[i_mli]:starts[i_mlf + 1]], 128)], axis=-1)

    return int(g0), part_b(w_in), part_b(b_in)


def kernel(x, mem, norm_mix, w_in, b_in, hgrn_lb_logits, hgrn_norm, nsa_q_norm, nsa_k_norm, nsa_cmp_pos, nsa_cmp_w, mlstm_conv, mlstm_norm, w_out, norm_xattn, norm_mem, xa_wq, xa_wk, xa_wv, xa_wo, xa_q_norm, xa_k_norm, norm_mlp, mlp_w1, mlp_w2):
    _, t, d_model = x.shape
    depth = w_in.shape[0]
    d = HEAD_DIM
    hg_heads = ml_heads = d_model // (4 * d)
    nsa_heads = d_model // (2 * d)
    kv_heads = nsa_heads // NSA_GROUP
    hw, nw, kvw = hg_heads * d, nsa_heads * d, kv_heads * d
    sizes = (hw,) * 4 + (nw,) + (kvw,) * 6 + (3 * nsa_heads,) + (hw,) * 4 + (ml_heads, ml_heads)
    col_nsa_q = 4 * hw
    col_nsa_kc = col_nsa_q + nw
    xa_heads = 4
    bf = MXU_DTYPE

    half = d // 2
    inv_freq = ROPE_THETA ** (-jnp.arange(half, dtype=F32) / half)

    def rope_tables(pos):
        ang = pos[:, None] * inv_freq[None, :]
        cos, sin = jnp.cos(ang), jnp.sin(ang)
        return jnp.concatenate([cos, cos], axis=-1), jnp.concatenate([-sin, sin], axis=-1)

    cos_t, sin_t = rope_tables(jnp.arange(t, dtype=F32))
    ncp = t // CMP_STRIDE
    n_cmp, n_sel = ncp - 1, t // SEL_BLOCK
    cos_c, sin_c = rope_tables(jnp.arange(ncp, dtype=F32) * CMP_STRIDE + (CMP_LEN - 1))
    et = jnp.asarray(np.arange(t)[:, None] // SEL_BLOCK == np.arange(SEL_LANES)[None, :], bf)
    c_start = np.arange(ncp)[:, None] * CMP_STRIDE
    s_start = np.arange(SEL_LANES)[None, :] * SEL_BLOCK
    overlap = ((c_start < s_start + SEL_BLOCK) & (c_start + CMP_LEN > s_start)
               & (np.arange(ncp)[:, None] < n_cmp) & (np.arange(SEL_LANES)[None, :] < n_sel))
    ovt = jnp.asarray(overlap.T, bf)

    n_a, w_b, b_b = _split_in_proj(w_in, b_in, sizes)
    w_a, w_b = cast_cols(w_in, n_cols=n_a, tk=512, tn=n_a // 3), w_b.astype(bf)
    w_out_c, wq_c, wo_c = (a.astype(bf) for a in (w_out, xa_wq, xa_wo))
    w1_c, w2_c, cmp_w_c = mlp_w1.astype(bf), mlp_w2.astype(bf), nsa_cmp_w.astype(bf)

    h = x.reshape(t, d_model)
    mem2 = mem.reshape(mem.shape[1], d_model)
    hn = rms_cast(h, norm_mix[0], tm=512)
    for layer in range(depth):
        z = matmul_cols(hn, w_a, b_in, layer=layer, n_tiles=3, tm=512, tn=n_a // 3)
        zb = matmul_cols(hn, w_b, b_b, layer=layer, n_tiles=1, tm=512, tn=w_b.shape[2])
        gate_block = 4 * hw // 128
        gt = zb[:, (gate_block + 2) * 128:(gate_block + 2) * 128 + 8].T

        y_hg = hgrn2(z, hgrn_lb_logits, hgrn_norm[layer], layer=layer, heads=hg_heads, tb=512)
        y_ml = mlstm(zb, zb, gt, mlstm_conv[layer], mlstm_norm[layer], heads=ml_heads, tb=512, col0=0,
                     zs_block=gate_block + 2)

        qt, ks_aug, vst, kw, vwt, q_nrm, k_nrm = nsa_prep(z, cos_t, sin_t, et, nsa_q_norm[layer], nsa_k_norm[layer],
                                                          heads=nsa_heads, kv_heads=kv_heads, tq=512, col_q=col_nsa_q)
        kc, vct = nsa_compress(z, nsa_cmp_pos[layer], cmp_w_c[layer], nsa_k_norm[layer],
                               cos_c, sin_c, kv_heads=kv_heads, col_kc=col_nsa_kc)
        o_cmp, mb = nsa_cmp_select(qt, kc, vct, ovt, kv_heads=kv_heads, tq=256)
        o_sel = nsa_sel_attn(qt, mb, ks_aug, vst, q_nrm, k_nrm, kv_heads=kv_heads, tq=512, tk=512, hpu=1)
        y_ns = nsa_win_combine(qt, kw, vwt, o_cmp, o_sel, zb, kv_heads=kv_heads, tq=256, zs_block0=gate_block)

        kb = hw
        terms = [(y_hg, 0, 0)] + [(y_ns, c, 1 + c) for c in range(nw // kb)] + [(y_ml, 0, 1 + nw // kb)]
        h, hn = matmul_residual(terms, w_out_c, h, norm_xattn[layer], layer=layer, kb=kb, tm=512)

        k_mem, v_mem = xa_kv(mem2, norm_mem[layer], xa_wk, xa_wv, xa_k_norm[layer], layer=layer, heads=xa_heads)
        q_x = matmul_cols(hn, wq_c, None, layer=layer, n_tiles=1, tm=512, tn=d_model)
        o_x = xa_attn(q_x, xa_q_norm[layer], k_mem, v_mem, heads=xa_heads, tm=512)
        h = matmul_residual([(o_x, c, c) for c in range(d_model // kb)], wo_c, h, None, layer=layer, kb=kb, tm=512)

        if layer + 1 < depth:
            h, hn = mlp(h, norm_mlp[layer], w1_c, w2_c, norm_mix[layer + 1], layer=layer, tm=512, tf=1024)
        else:
            h = mlp(h, norm_mlp[layer], w1_c, w2_c, None, layer=layer, tm=512, tf=1024)
    return h.reshape(x.shape)
```

```python
import functools

import jax
import jax.numpy as jnp
import numpy as np
from jax import lax
from jax.experimental import pallas as pl
from jax.experimental.pallas import tpu as pltpu

F32 = jnp.float32
MXU_DTYPE = jnp.bfloat16

EPS = 1e-6
NEG_INF = -1e30
LOG2_E = 1.4426950408889634
HEAD_DIM = 128
ROPE_THETA = 10000.0

V7X_VMEM_LIMIT_BYTES = 56 * 1024 * 1024


def _params(*semantics):
    return pltpu.CompilerParams(dimension_semantics=semantics, vmem_limit_bytes=V7X_VMEM_LIMIT_BYTES)


def _rms(x, gain):
    return x * lax.rsqrt(jnp.mean(x * x, axis=-1, keepdims=True) + EPS) * gain


def _dot(a, b):
    return jnp.dot(a.astype(MXU_DTYPE), b.astype(MXU_DTYPE), preferred_element_type=F32)


def _dot_nt(a, b):
    return lax.dot_general(a.astype(MXU_DTYPE), b.astype(MXU_DTYPE), (((1,), (1,)), ((), ())),
                           preferred_element_type=F32)


def _dot_tn(a, b):
    return lax.dot_general(a.astype(MXU_DTYPE), b.astype(MXU_DTYPE), (((0,), (0,)), ((), ())),
                           preferred_element_type=F32)


def _rms_cast_kernel(x_ref, g_ref, o_ref):
    o_ref[...] = _rms(x_ref[...], g_ref[...]).astype(o_ref.dtype)


def rms_cast(x, gain, *, tm):
    m, d = x.shape
    return pl.pallas_call(
        _rms_cast_kernel,
        grid=(m // tm,),
        in_specs=[pl.BlockSpec((tm, d), lambda i: (i, 0)), pl.BlockSpec((1, d), lambda i: (0, 0))],
        out_specs=pl.BlockSpec((tm, d), lambda i: (i, 0)),
        out_shape=jax.ShapeDtypeStruct((m, d), MXU_DTYPE),
        compiler_params=_params("parallel"),
        name="rms_cast",
    )(x, gain.reshape(1, d))


def _matmul_bias_kernel(h_ref, w_ref, b_ref, o_ref):
    o_ref[...] = jnp.dot(h_ref[...], w_ref[...], preferred_element_type=F32) + b_ref[...]


def _matmul_kernel(h_ref, w_ref, o_ref):
    o_ref[...] = jnp.dot(h_ref[...], w_ref[...], preferred_element_type=F32)


def matmul_cols(h, w, bias, *, layer, n_tiles, tm, tn):
    m, d = h.shape
    in_specs = [pl.BlockSpec((tm, d), lambda j, i: (i, 0)),
                pl.BlockSpec((None, d, tn), lambda j, i: (layer, 0, j))]
    args = [h, w]
    if bias is not None:
        in_specs.append(pl.BlockSpec((None, 1, tn), lambda j, i: (layer, 0, j)))
        args.append(bias.reshape(bias.shape[0], 1, bias.shape[1]))
    return pl.pallas_call(
        _matmul_kernel if bias is None else _matmul_bias_kernel,
        grid=(n_tiles, m // tm),
        in_specs=in_specs,
        out_specs=pl.BlockSpec((tm, tn), lambda j, i: (i, j)),
        out_shape=jax.ShapeDtypeStruct((m, n_tiles * tn), F32),
        compiler_params=_params("parallel", "parallel"),
        name="matmul_cols",
    )(*args)


def _matmul_res_kernel(*refs, n_terms, with_norm):
    ys, ws = refs[:n_terms], refs[n_terms:2 * n_terms]
    res_ref = refs[2 * n_terms]
    acc = res_ref[...]
    for y_ref, w_ref in zip(ys, ws):
        acc = acc + jnp.dot(y_ref[...].astype(MXU_DTYPE), w_ref[...], preferred_element_type=F32)
    if with_norm:
        g_ref, o_ref, hn_ref = refs[2 * n_terms + 1:]
        hn_ref[...] = _rms(acc, g_ref[...]).astype(hn_ref.dtype)
    else:
        o_ref = refs[2 * n_terms + 1]
    o_ref[...] = acc


def matmul_residual(terms, w, res, next_gain, *, layer, kb, tm):
    m, n = res.shape
    y_specs = [pl.BlockSpec((tm, kb), functools.partial(lambda i, c: (i, c), c=cb)) for _, cb, _ in terms]
    w_specs = [pl.BlockSpec((None, kb, n), functools.partial(lambda i, r: (layer, r, 0), r=rb)) for _, _, rb in terms]
    row_spec = pl.BlockSpec((tm, n), lambda i: (i, 0))
    with_norm = next_gain is not None
    return pl.pallas_call(
        functools.partial(_matmul_res_kernel, n_terms=len(terms), with_norm=with_norm),
        grid=(m // tm,),
        in_specs=y_specs + w_specs + [row_spec] + ([pl.BlockSpec((1, n), lambda i: (0, 0))] if with_norm else []),
        out_specs=[row_spec, row_spec] if with_norm else row_spec,
        out_shape=([jax.ShapeDtypeStruct((m, n), F32), jax.ShapeDtypeStruct((m, n), MXU_DTYPE)] if with_norm
                   else jax.ShapeDtypeStruct((m, n), F32)),
        compiler_params=_params("parallel"),
        name="matmul_residual",
    )(*[t[0] for t in terms], *([w] * len(terms)), res, *([next_gain.reshape(1, n)] if with_norm else []))


def _mlp_kernel(*refs, with_norm):
    if with_norm:
        x_ref, g_ref, w1_ref, w2_ref, gn_ref, o_ref, hn_ref, h_scr = refs
    else:
        x_ref, g_ref, w1_ref, w2_ref, o_ref, h_scr = refs

    @pl.when(pl.program_id(1) == 0)
    def _():
        x = x_ref[...]
        h_scr[...] = _rms(x, g_ref[...]).astype(h_scr.dtype)
        o_ref[...] = x

    u = jnp.dot(h_scr[...], w1_ref[...], preferred_element_type=F32)
    a = jnp.square(jnp.maximum(u, 0.0)).astype(MXU_DTYPE)
    o_ref[...] += jnp.dot(a, w2_ref[...], preferred_element_type=F32)

    if with_norm:
        @pl.when(pl.program_id(1) == pl.num_programs(1) - 1)
        def _():
            hn_ref[...] = _rms(o_ref[...], gn_ref[...]).astype(hn_ref.dtype)


def mlp(x, gain, w1, w2, next_gain, *, layer, tm, tf):
    m, d = x.shape
    ff = w1.shape[2]
    row_spec = pl.BlockSpec((tm, d), lambda i, f: (i, 0))
    gain_spec = pl.BlockSpec((1, d), lambda i, f: (0, 0))
    with_norm = next_gain is not None
    return pl.pallas_call(
        functools.partial(_mlp_kernel, with_norm=with_norm),
        grid=(m // tm, ff // tf),
        in_specs=[row_spec, gain_spec,
                  pl.BlockSpec((None, d, tf), lambda i, f: (layer, 0, f)),
                  pl.BlockSpec((None, tf, d), lambda i, f: (layer, f, 0))] + ([gain_spec] if with_norm else []),
        out_specs=[row_spec, row_spec] if with_norm else row_spec,
        out_shape=([jax.ShapeDtypeStruct((m, d), F32), jax.ShapeDtypeStruct((m, d), MXU_DTYPE)] if with_norm
                   else jax.ShapeDtypeStruct((m, d), F32)),
        scratch_shapes=[pltpu.VMEM((tm, d), MXU_DTYPE)],
        compiler_params=_params("parallel", "arbitrary"),
        name="mlp",
    )(x, gain.reshape(1, d), w1, w2, *([next_gain.reshape(1, d)] if with_norm else []))


def _xa_kv_kernel(mem_ref, g_ref, wk_ref, wv_ref, kg_ref, k_ref, v_ref):
    mn = _rms(mem_ref[...], g_ref[...])
    k_ref[...] = _rms(_dot(mn, wk_ref[...]), kg_ref[...]).astype(k_ref.dtype)
    v_ref[...] = _dot(mn, wv_ref[...]).astype(v_ref.dtype)


def xa_kv(mem, gain, wk, wv, k_gain, *, layer, heads):
    ml, d = mem.shape
    hd = d // heads
    return pl.pallas_call(
        _xa_kv_kernel,
        grid=(heads,),
        in_specs=[
            pl.BlockSpec((ml, d), lambda h: (0, 0)),
            pl.BlockSpec((1, d), lambda h: (0, 0)),
            pl.BlockSpec((None, d, hd), lambda h: (layer, 0, h)),
            pl.BlockSpec((None, d, hd), lambda h: (layer, 0, h)),
            pl.BlockSpec((1, hd), lambda h: (0, 0)),
        ],
        out_specs=[pl.BlockSpec((ml, hd), lambda h: (0, h))] * 2,
        out_shape=[jax.ShapeDtypeStruct((ml, d), MXU_DTYPE)] * 2,
        compiler_params=_params("parallel"),
        name="xa_kv",
    )(mem, gain.reshape(1, d), wk, wv, k_gain.reshape(1, hd))


def _xa_attn_kernel(q_ref, qg_ref, k_ref, v_ref, o_ref, *, heads, hd):
    scale = hd ** -0.5
    for h in range(heads):
        cs = slice(h * hd, (h + 1) * hd)
        q = _rms(q_ref[:, cs], qg_ref[...]) * scale
        s = _dot_nt(q, k_ref[:, cs])
        p = jnp.exp(s - jnp.max(s, axis=-1, keepdims=True))
        p = p / jnp.sum(p, axis=-1, keepdims=True)
        o_ref[:, cs] = _dot(p, v_ref[:, cs]).astype(o_ref.dtype)


def xa_attn(q, q_gain, k, v, *, heads, tm):
    m, d = q.shape
    ml = k.shape[0]
    hd = d // heads
    return pl.pallas_call(
        functools.partial(_xa_attn_kernel, heads=heads, hd=hd),
        grid=(m // tm,),
        in_specs=[
            pl.BlockSpec((tm, d), lambda i: (i, 0)),
            pl.BlockSpec((1, hd), lambda i: (0, 0)),
            pl.BlockSpec((ml, d), lambda i: (0, 0)),
            pl.BlockSpec((ml, d), lambda i: (0, 0)),
        ],
        out_specs=pl.BlockSpec((tm, d), lambda i: (i, 0)),
        out_shape=jax.ShapeDtypeStruct((m, d), MXU_DTYPE),
        compiler_params=_params("parallel"),
        name="xa_attn",
    )(q, q_gain.reshape(1, hd), k, v)


def _seg_cumsum_rows(x, seg):
    pos = lax.broadcasted_iota(jnp.int32, x.shape, 0) % seg
    k = 1
    while k < seg:
        x = x + jnp.where(pos >= k, pltpu.roll(x, k, axis=0), 0.0)
        k *= 2
    return x


def _seg_cumsum_lanes(x, seg):
    pos = lax.broadcasted_iota(jnp.int32, x.shape, 1) % seg
    k = 1
    while k < seg:
        x = x + jnp.where(pos >= k, pltpu.roll(x, k, axis=1), 0.0)
        k *= 2
    return x


def _sigmoid(x):
    return 1.0 / (1.0 + jnp.exp(-x))


def _log_sigmoid(x):
    return jnp.minimum(x, 0.0) - jnp.log(1.0 + jnp.exp(-jnp.abs(x)))


HG_SUB = 16
HG_MIN_FORGET = 1e-20


def _hgrn2_kernel(q_ref, f_ref, i_ref, g_ref, lbl_ref, ng_ref, y_ref, st_ref, *, layer, heads, tb):
    dk = HEAD_DIM

    @pl.when(pl.program_id(0) == 0)
    def _():
        st_ref[...] = jnp.zeros_like(st_ref)

    logits = lbl_ref[...]
    e = jnp.exp(logits - jnp.max(logits, axis=0, keepdims=True))
    probs = e / jnp.sum(e, axis=0, keepdims=True)
    lb_all = jnp.zeros_like(probs[0:1])
    for l in range(1, layer + 1):
        lb_all = lb_all + probs[l:l + 1]
    half = HG_SUB // 2
    row_half = lax.broadcasted_iota(jnp.int32, (half, dk), 0)

    n_sub = tb // HG_SUB

    def gates(c):
        r = pl.ds(pl.multiple_of(c * HG_SUB, HG_SUB), HG_SUB)
        pre = []
        for h in range(heads):
            cs = slice(h * dk, (h + 1) * dk)
            lb = lb_all[:, cs]
            q, zf, v = q_ref[r, cs], f_ref[r, cs], i_ref[r, cs]
            forget = lb + (1.0 - lb) * _sigmoid(zf)
            log_f = jnp.log(jnp.maximum(forget, HG_MIN_FORGET))
            key = (1.0 - lb) * _sigmoid(-zf)
            b = _seg_cumsum_rows(log_f, HG_SUB) * LOG2_E
            pre.append((q, v, key, b))
        return pre

    def body(c, pre):
        r = pl.ds(pl.multiple_of(c * HG_SUB, HG_SUB), HG_SUB)
        pre_next = gates(jnp.minimum(c + 1, n_sub - 1))
        pre = [(q, v, key, b, b[HG_SUB - 1:HG_SUB]) for q, v, key, b in pre]
        inter = [_dot_nt(q * jnp.exp2(b), st_ref[h]) for h, (q, v, key, b, b_end) in enumerate(pre)]
        upd = [_dot_tn(v, key * jnp.exp2(b_end - b)) for q, v, key, b, b_end in pre]
        for h in range(heads):
            cs = slice(h * dk, (h + 1) * dk)
            q, v, key, b, b_end = pre[h]
            q_lo, q_hi, b_lo, b_hi = q[0:half], q[half:], b[0:half], b[half:]
            o_lo = jnp.zeros((half, dk), F32)
            o_hi = jnp.zeros((half, dk), F32)
            for s in range(HG_SUB):
                bs, ks, vs = b[s:s + 1], key[s:s + 1], v[s:s + 1]
                if s < half:
                    decay = jnp.exp2(jnp.where(row_half >= s, b_lo - bs, NEG_INF))
                    o_lo = o_lo + jnp.sum(q_lo * decay * ks, axis=-1, keepdims=True) * vs
                    decay = jnp.exp2(b_hi - bs)
                else:
                    decay = jnp.exp2(jnp.where(row_half >= s - half, b_hi - bs, NEG_INF))
                o_hi = o_hi + jnp.sum(q_hi * decay * ks, axis=-1, keepdims=True) * vs
            o = jnp.concatenate([o_lo, o_hi], axis=0) + inter[h]
            st_ref[h] = jnp.exp2(b_end) * st_ref[h] + upd[h]
            g = g_ref[r, cs]
            y = _rms(o, ng_ref[:, cs]) * (g * _sigmoid(g))
            y_ref[r, cs] = y.astype(y_ref.dtype)
        return pre_next

    lax.fori_loop(0, n_sub, body, gates(jnp.int32(0)))


def hgrn2(z, lb_logits, norm_gain, *, layer, heads, tb):
    t = z.shape[0]
    w = heads * HEAD_DIM
    depth = lb_logits.shape[0]
    return pl.pallas_call(
        functools.partial(_hgrn2_kernel, layer=layer, heads=heads, tb=tb),
        grid=(t // tb,),
        in_specs=[pl.BlockSpec((tb, w), functools.partial(lambda i, c: (i, c), c=c)) for c in range(4)] + [
            pl.BlockSpec((depth, w), lambda i: (0, 0)),
            pl.BlockSpec((1, w), lambda i: (0, 0)),
        ],
        out_specs=pl.BlockSpec((tb, w), lambda i: (i, 0)),
        out_shape=jax.ShapeDtypeStruct((t, w), MXU_DTYPE),
        scratch_shapes=[pltpu.VMEM((heads, HEAD_DIM, HEAD_DIM), F32)],
        compiler_params=_params("arbitrary"),
        name="hgrn2",
    )(z, z, z, z, lb_logits, norm_gain.reshape(1, w))


ML_CHUNK = 64
ML_CONV = 4
ML_TAIL = 8
ML_AUG = HEAD_DIM + 16
ZS_ML_I = 0
ZS_ML_F = 4


def _mlstm_kernel(q_ref, k_ref, v_ref, o_ref, zs_ref, gt_ref, cw_ref, ng_ref, y_ref,
                  conv_scr, st_scr, m_scr, *, heads, tb):
    d = HEAD_DIM
    w = heads * d

    @pl.when(pl.program_id(0) == 0)
    def _():
        conv_scr[0:ML_TAIL, :] = jnp.zeros((ML_TAIL, 2 * w), F32)
        st_scr[...] = jnp.zeros_like(st_scr)
        m_scr[...] = jnp.zeros_like(m_scr)

    conv_scr[ML_TAIL:ML_TAIL + tb, 0:w] = q_ref[...]
    conv_scr[ML_TAIL:ML_TAIL + tb, w:2 * w] = k_ref[...]
    acc = jnp.zeros((tb, 2 * w), F32)
    for j in range(ML_CONV):
        off = ML_TAIL - (ML_CONV - 1) + j
        acc = acc + conv_scr[off:off + tb, :] * cw_ref[j:j + 1, :]
    conv_scr[0:ML_TAIL, :] = conv_scr[tb:tb + ML_TAIL, :]
    qk = acc * _sigmoid(acc)

    pair = 2 * ML_CHUNK
    zs = zs_ref[...]
    b_col_all = _seg_cumsum_rows(_log_sigmoid(zs), ML_CHUNK)
    gt = gt_ref[...]
    b_row_all = _seg_cumsum_lanes(_log_sigmoid(gt), ML_CHUNK)
    s_idx = lax.broadcasted_iota(jnp.int32, (pair, pair), 0)
    t_idx = lax.broadcasted_iota(jnp.int32, (pair, pair), 1)
    visible = (s_idx // ML_CHUNK == t_idx // ML_CHUNK) & (s_idx <= t_idx)
    lane = lax.broadcasted_iota(jnp.int32, (1, pair), 1)
    in_chunk = [lane < ML_CHUNK, lane >= ML_CHUNK]
    ones_rows = (lax.broadcasted_iota(jnp.int32, (ML_AUG - d, pair), 0) == 0).astype(F32)

    def rows(p):
        return slice(p * pair, (p + 1) * pair)

    def stage_scores(p):
        out = []
        for h in range(heads):
            qt = (qk[rows(p), h * d:(h + 1) * d] * (d ** -0.5)).T
            out.append((qt, _dot(qk[rows(p), w + h * d:w + (h + 1) * d], qt)))
        return out

    def stage_free(p, sc):
        rs = rows(p)
        out = []
        for h in range(heads):
            qt, s_raw = sc[h]
            b_row = b_row_all[heads + h:heads + h + 1, rs]
            li_row = gt[h:h + 1, rs]
            u_col = b_col_all[rs, ZS_ML_F + h:ZS_ML_F + h + 1] - zs[rs, ZS_ML_I + h:ZS_ML_I + h + 1]
            d_log = jnp.where(visible, b_row - u_col, NEG_INF)
            dmax = jnp.max(d_log, axis=0, keepdims=True)
            s_t = s_raw * jnp.exp(d_log - dmax)
            vt_aug = jnp.concatenate([v_ref[rs, h * d:(h + 1) * d].T, ones_rows], axis=0)
            kc = qk[rs, w + h * d:w + (h + 1) * d]
            per_chunk = []
            for c in range(2):
                b_end = b_row[:, (c + 1) * ML_CHUNK - 1:(c + 1) * ML_CHUNK]
                w_state = jnp.where(in_chunk[c], b_end - b_row + li_row, NEG_INF)
                wmax = jnp.max(w_state, axis=1, keepdims=True)
                w_row = jnp.exp(w_state - wmax)
                per_chunk.append((b_end, wmax, _dot(vt_aug * w_row, kc)))
            out.append(dict(qt=qt, b_row=b_row, dmax=dmax, intra=_dot(vt_aug, s_t), per_chunk=per_chunk))
        return out

    n_pairs = tb // pair
    sc = {0: stage_scores(0)}
    if n_pairs > 1:
        sc[1] = stage_scores(1)
    free = {0: stage_free(0, sc.pop(0))}
    for p in range(n_pairs):
        rs = rows(p)
        if p + 2 < n_pairs:
            sc[p + 2] = stage_scores(p + 2)
        if p + 1 < n_pairs:
            free[p + 1] = stage_free(p + 1, sc.pop(p + 1))
        cur = free.pop(p)
        for h in range(heads):
            cs = slice(h * d, (h + 1) * d)
            f = cur[h]
            num, m_tok = None, None
            for c in range(2):
                b_end, wmax, upd = f["per_chunk"][c]
                state, m_prev = st_scr[h], m_scr[h]
                inter = _dot(state, f["qt"])
                inter_log = f["b_row"] + m_prev
                m_t = jnp.maximum(inter_log, f["dmax"])
                num_c = jnp.exp(inter_log - m_t) * inter + jnp.exp(f["dmax"] - m_t) * f["intra"]
                num = num_c if c == 0 else jnp.where(in_chunk[0], num, num_c)
                m_tok = m_t if c == 0 else jnp.where(in_chunk[0], m_tok, m_t)
                m_new = jnp.maximum(b_end + m_prev, wmax)
                st_scr[h] = jnp.exp(b_end + m_prev - m_new) * state + jnp.exp(wmax - m_new) * upd
                m_scr[h] = m_new
            qn = num[d:d + 1]
            hcell = num[0:d] / jnp.maximum(jnp.abs(qn), jnp.exp(-m_tok))
            hn = hcell * lax.rsqrt(jnp.mean(hcell * hcell, axis=0, keepdims=True) + EPS)
            y = hn.T * ng_ref[:, cs] * _sigmoid(o_ref[rs, cs])
            y_ref[rs, cs] = y.astype(y_ref.dtype)


def mlstm(z, zs, gt, conv_w, norm_gain, *, heads, tb, col0, zs_block):
    t = z.shape[0]
    w = heads * HEAD_DIM
    cb = col0 // w
    return pl.pallas_call(
        functools.partial(_mlstm_kernel, heads=heads, tb=tb),
        grid=(t // tb,),
        in_specs=[pl.BlockSpec((tb, w), functools.partial(lambda i, c: (i, c), c=cb + c)) for c in range(4)] + [
            pl.BlockSpec((tb, 128), lambda i: (i, zs_block)),
            pl.BlockSpec((8, tb), lambda i: (0, i)),
            pl.BlockSpec((ML_CONV, 2 * w), lambda i: (0, 0)),
            pl.BlockSpec((1, w), lambda i: (0, 0)),
        ],
        out_specs=pl.BlockSpec((tb, w), lambda i: (i, 0)),
        out_shape=jax.ShapeDtypeStruct((t, w), MXU_DTYPE),
        scratch_shapes=[
            pltpu.VMEM((tb + ML_TAIL, 2 * w), F32),
            pltpu.VMEM((heads, ML_AUG, HEAD_DIM), F32),
            pltpu.VMEM((heads, 1, 1), F32),
        ],
        compiler_params=_params("arbitrary"),
        name="mlstm",
    )(z, z, z, z, zs, gt, conv_w, norm_gain.reshape(1, w))


NSA_GROUP = 4
CMP_LEN = 32
CMP_STRIDE = 16
SEL_BLOCK = 64
N_SELECT = 16
WINDOW = 512
FORCE_BONUS = 1e4
SEL_LANES = 128
SEL_LAG_MAX_GAP = 64.0
SEL_BOUND_MARGIN = 1.02


def _rope(x, cos_t, sin_t):
    return x * cos_t + pltpu.roll(x, HEAD_DIM // 2, axis=1) * sin_t


def _nsa_prep_kernel(q_ref, ks_ref, vs_ref, kw_ref, vw_ref, cos_ref, sin_ref, et_ref, qg_ref, kg_ref,
                     qh_ref, ksa_ref, vso_ref, kwr_ref, vwo_ref, qn_ref, kn_ref, *, heads, kv_heads):
    d = HEAD_DIM
    cos_t, sin_t = cos_ref[...], sin_ref[...]
    for h in range(heads):
        cs = slice(h * d, (h + 1) * d)
        qt = (_rope(_rms(q_ref[:, cs], qg_ref[...]), cos_t, sin_t) * (d ** -0.5 * LOG2_E)).T
        qh_ref[cs, :] = qt.astype(qh_ref.dtype)
        qn_ref[h:h + 1, :] = jnp.sqrt(jnp.sum(qt * qt, axis=0, keepdims=True))
    for g in range(kv_heads):
        cs = slice(g * d, (g + 1) * d)
        ks = _rope(_rms(ks_ref[:, cs], kg_ref[1:2]), cos_t, sin_t)
        ksa_ref[g, :, 0:d] = ks.astype(ksa_ref.dtype)
        k_norm2 = jnp.max(jnp.sum(ks * ks, axis=-1, keepdims=True), axis=0, keepdims=True)
        kn_ref[g, 0] = jnp.broadcast_to(jnp.sqrt(k_norm2), kn_ref.shape[2:])
        ksa_ref[g, :, d:2 * d] = et_ref[...]
        kwr_ref[g] = _rope(_rms(kw_ref[:, cs], kg_ref[2:3]), cos_t, sin_t).astype(kwr_ref.dtype)
        vso_ref[g] = vs_ref[:, cs].T.astype(vso_ref.dtype)
        vwo_ref[g] = vw_ref[:, cs].T.astype(vwo_ref.dtype)


def nsa_prep(z, cos_t, sin_t, et, q_gain, k_gains, *, heads, kv_heads, tq, col_q):
    t = z.shape[0]
    d = HEAD_DIM
    kvw = kv_heads * d
    qb = col_q // (heads * d)
    kb = (col_q + heads * d) // kvw

    def zcol(width, c):
        return pl.BlockSpec((tq, width), functools.partial(lambda i, c: (i, c), c=c))

    k_out = pl.BlockSpec((kv_heads, tq, d), lambda i: (0, i, 0))
    k_shape = jax.ShapeDtypeStruct((kv_heads, t, d), MXU_DTYPE)
    vt_out = pl.BlockSpec((kv_heads, d, tq), lambda i: (0, 0, i))
    vt_shape = jax.ShapeDtypeStruct((kv_heads, d, t), MXU_DTYPE)
    return pl.pallas_call(
        functools.partial(_nsa_prep_kernel, heads=heads, kv_heads=kv_heads),
        grid=(t // tq,),
        in_specs=[zcol(heads * d, qb), zcol(kvw, kb + 2), zcol(kvw, kb + 3), zcol(kvw, kb + 4), zcol(kvw, kb + 5),
                  pl.BlockSpec((tq, d), lambda i: (i, 0)), pl.BlockSpec((tq, d), lambda i: (i, 0)),
                  pl.BlockSpec((tq, SEL_LANES), lambda i: (i, 0)),
                  pl.BlockSpec((1, d), lambda i: (0, 0)), pl.BlockSpec((3, d), lambda i: (0, 0))],
        out_specs=[pl.BlockSpec((heads * d, tq), lambda i: (0, i)),
                   pl.BlockSpec((kv_heads, tq, 2 * d), lambda i: (0, i, 0)), vt_out, k_out, vt_out,
                   pl.BlockSpec((heads, tq), lambda i: (0, i)),
                   pl.BlockSpec((kv_heads, 1, 8, d), lambda i: (0, i, 0, 0))],
        out_shape=[jax.ShapeDtypeStruct((heads * d, t), MXU_DTYPE),
                   jax.ShapeDtypeStruct((kv_heads, t, 2 * d), MXU_DTYPE), vt_shape, k_shape, vt_shape,
                   jax.ShapeDtypeStruct((heads, t), F32),
                   jax.ShapeDtypeStruct((kv_heads, t // tq, 8, d), F32)],
        compiler_params=_params("parallel"),
        name="nsa_prep",
    )(z, z, z, z, z, cos_t, sin_t, et, q_gain.reshape(1, d), k_gains)


def _nsa_compress_kernel(zk_ref, zv_ref, pe_ref, w_ref, kg_ref, cos_ref, sin_ref, kc_ref, vc_ref, *, ncp):
    d = HEAD_DIM
    half = CMP_LEN // 2
    row = lax.broadcasted_iota(jnp.int32, (ncp, d), 0)

    def compress(z_ref, which):
        lo = jnp.zeros((ncp, d), F32)
        hi = jnp.zeros((ncp, d), F32)
        for l in range(half):
            xl = z_ref[pl.ds(l, ncp, stride=CMP_STRIDE), :]
            lo = lo + _dot(xl + pe_ref[which, l:l + 1, :], w_ref[which, l])
            hi = hi + _dot(xl + pe_ref[which, half + l:half + l + 1, :], w_ref[which, half + l])
        out = lo + pltpu.roll(hi, ncp - 1, axis=0)
        return jnp.where(row < ncp - 1, out, 0.0)

    kc_ref[0] = _rope(_rms(compress(zk_ref, 0), kg_ref[0:1]), cos_ref[...], sin_ref[...]).astype(kc_ref.dtype)
    vc_ref[0] = compress(zv_ref, 1).T.astype(vc_ref.dtype)


def nsa_compress(z, cmp_pos, cmp_w, k_gains, cos_c, sin_c, *, kv_heads, col_kc):
    t = z.shape[0]
    d = HEAD_DIM
    ncp = t // CMP_STRIDE
    kb = col_kc // d
    out_spec = pl.BlockSpec((1, ncp, d), lambda g: (g, 0, 0))
    out_shape = jax.ShapeDtypeStruct((kv_heads, ncp, d), MXU_DTYPE)
    return pl.pallas_call(
        functools.partial(_nsa_compress_kernel, ncp=ncp),
        grid=(kv_heads,),
        in_specs=[pl.BlockSpec((t, d), lambda g: (0, kb + g)),
                  pl.BlockSpec((t, d), lambda g: (0, kb + kv_heads + g)),
                  pl.BlockSpec((2, CMP_LEN, d), lambda g: (0, 0, 0)),
                  pl.BlockSpec((2, CMP_LEN, d, d), lambda g: (0, 0, 0, 0)),
                  pl.BlockSpec((3, d), lambda g: (0, 0)),
                  pl.BlockSpec((ncp, d), lambda g: (0, 0)), pl.BlockSpec((ncp, d), lambda g: (0, 0))],
        out_specs=[out_spec, pl.BlockSpec((1, d, ncp), lambda g: (g, 0, 0))],
        out_shape=[out_shape, jax.ShapeDtypeStruct((kv_heads, d, ncp), MXU_DTYPE)],
        compiler_params=_params("parallel"),
        name="nsa_compress",
    )(z, z, cmp_pos, cmp_w, k_gains, cos_c, sin_c)


def _nsa_cmp_kernel(qt_ref, kc_ref, vct_ref, ovt_ref, ocmp_ref, mb_ref, *, tq, grp, ncp):
    d = HEAD_DIM
    qi = pl.program_id(1)
    cmp_end = lax.broadcasted_iota(jnp.int32, (ncp, tq), 0) * CMP_STRIDE + (CMP_LEN - 1)
    vis = cmp_end <= qi * tq + lax.broadcasted_iota(jnp.int32, (ncp, tq), 1)
    psum = None

    def scores(j):
        return jnp.dot(kc_ref[0], qt_ref[j * d:(j + 1) * d, :], preferred_element_type=F32)

    s_next = scores(0)
    for j in range(grp):
        s = s_next
        if j + 1 < grp:
            s_next = scores(j + 1)
        sm = jnp.where(vis, s, NEG_INF)
        p = jnp.where(vis, jnp.exp2(sm - jnp.max(sm, axis=0, keepdims=True)), 0.0)
        l = jnp.sum(p, axis=0, keepdims=True)
        p = p * (1.0 / jnp.where(l > 0.0, l, 1.0))
        ocmp_ref[j * d:(j + 1) * d, :] = _dot(vct_ref[0], p)
        psum = p if psum is None else psum + p
    hi = psum.astype(MXU_DTYPE)
    lo = (psum - hi.astype(F32)).astype(MXU_DTYPE)
    imp = (jnp.dot(ovt_ref[...], hi, preferred_element_type=F32)
           + jnp.dot(ovt_ref[...], lo, preferred_element_type=F32))
    blk = lax.broadcasted_iota(jnp.int32, (SEL_LANES, tq), 0)
    cur = (qi * tq + lax.broadcasted_iota(jnp.int32, (SEL_LANES, tq), 1)) // SEL_BLOCK
    forced = (blk == 0) | (blk == cur) | (blk == cur - 1)
    score = jnp.where(blk <= cur, imp + jnp.where(forced, FORCE_BONUS, 0.0), NEG_INF)
    blk_f = blk.astype(F32)
    bias = jnp.full((SEL_LANES, tq), NEG_INF, F32)
    for _ in range(N_SELECT):
        mx = jnp.max(score, axis=0, keepdims=True)
        first = jnp.min(jnp.where(score == mx, blk_f, float(SEL_LANES)), axis=0, keepdims=True)
        pick = blk_f == first
        bias = jnp.where(pick, jnp.where(mx > 0.5 * NEG_INF, 0.0, NEG_INF), bias)
        score = jnp.where(pick, -jnp.inf, score)
    mb_ref[0] = bias.astype(mb_ref.dtype)


def nsa_cmp_select(qt, kc, vct, ovt, *, kv_heads, tq):
    t = qt.shape[1]
    d = HEAD_DIM
    ncp = kc.shape[1]
    gw = NSA_GROUP * d
    return pl.pallas_call(
        functools.partial(_nsa_cmp_kernel, tq=tq, grp=NSA_GROUP, ncp=ncp),
        grid=(kv_heads, t // tq),
        in_specs=[pl.BlockSpec((gw, tq), lambda g, i: (g, i)),
                  pl.BlockSpec((1, ncp, d), lambda g, i: (g, 0, 0)),
                  pl.BlockSpec((1, d, ncp), lambda g, i: (g, 0, 0)),
                  pl.BlockSpec((SEL_LANES, ncp), lambda g, i: (0, 0))],
        out_specs=[pl.BlockSpec((gw, tq), lambda g, i: (g, i)),
                   pl.BlockSpec((1, SEL_LANES, tq), lambda g, i: (g, 0, i))],
        out_shape=[jax.ShapeDtypeStruct((kv_heads * gw, t), F32),
                   jax.ShapeDtypeStruct((kv_heads, SEL_LANES, t), MXU_DTYPE)],
        compiler_params=_params("parallel", "parallel"),
        name="nsa_cmp_select",
    )(qt, kc, vct, ovt)


def _nsa_sel_kernel(qi_tab, kj_tab, qt_ref, mb_ref, ka_ref, vt_ref, qn_ref, kn_ref, o_ref,
                    qa_scr, m_scr, l_scr, acc_scr,
                    *, tq, tk, kv_heads, grp, hpu):
    d = HEAD_DIM
    step = pl.program_id(0)
    qi, kj = qi_tab[step], kj_tab[step]
    last = (qi * tq + tq - 1) // tk
    heads = kv_heads * grp
    units = heads // hpu
    uw = hpu * tq

    def kv_of(u):
        return u * hpu // grp

    @pl.when(kj == 0)
    def _():
        for j in range(heads):
            u, ls = j // hpu, slice((j % hpu) * tq, (j % hpu + 1) * tq)
            qa_scr[u, 0:d, ls] = qt_ref[j * d:(j + 1) * d, :]
            qa_scr[u, d:2 * d, ls] = mb_ref[j // grp]
        m_scr[...] = jnp.full_like(m_scr, NEG_INF)
        l_scr[...] = jnp.zeros_like(l_scr)
        acc_scr[...] = jnp.zeros_like(acc_scr)

    def masked_scores(diagonal):
        if diagonal:
            causal = (kj * tk + lax.broadcasted_iota(jnp.int32, (tk, uw), 0)
                      <= qi * tq + lax.broadcasted_iota(jnp.int32, (tk, uw), 1) % tq)

        def scores(u):
            return jnp.dot(ka_ref[kv_of(u)], qa_scr[u], preferred_element_type=F32)

        s_next = scores(0)
        for u in range(units):
            s = s_next
            if u + 1 < units:
                s_next = scores(u + 1)
            yield u, (jnp.where(causal, s, NEG_INF) if diagonal else s)

    def update(diagonal):
        for u, s in masked_scores(diagonal):
            m_prev = m_scr[u]
            m_new = jnp.maximum(m_prev, jnp.max(s, axis=0, keepdims=True))
            alpha = jnp.exp2(m_prev - m_new)
            p = jnp.exp2(s - m_new)
            l_scr[u] = alpha * l_scr[u] + jnp.sum(p, axis=0, keepdims=True)
            acc_scr[u] = alpha * acc_scr[u] + _dot(vt_ref[kv_of(u)], p)
            m_scr[u] = m_new

    def update_lagged(diagonal):
        gap = None
        for u in range(units):
            heads_u = range(u * hpu, (u + 1) * hpu)
            q_norm = jnp.concatenate([qn_ref[j:j + 1, :] for j in heads_u], axis=1)
            bound = q_norm * (kn_ref[kv_of(u), 0, 0:1, 0:1] * SEL_BOUND_MARGIN)
            gap = bound - m_scr[u] if gap is None else jnp.maximum(gap, bound - m_scr[u])
        safe = jnp.max(gap) <= SEL_LAG_MAX_GAP

        @pl.when(safe)
        def _():
            for u, s in masked_scores(diagonal):
                m_prev = m_scr[u]
                p = jnp.exp2(s - m_prev)
                m_new = jnp.maximum(m_prev, jnp.max(s, axis=0, keepdims=True))
                alpha = jnp.exp2(m_prev - m_new)
                l_scr[u] = (l_scr[u] + jnp.sum(p, axis=0, keepdims=True)) * alpha
                acc_scr[u] = (acc_scr[u] + _dot(vt_ref[kv_of(u)], p)) * alpha
                m_scr[u] = m_new

        @pl.when(jnp.logical_not(safe))
        def _():
            update(diagonal)

    def finalize():
        for j in range(heads):
            u, ls = j // hpu, slice((j % hpu) * tq, (j % hpu + 1) * tq)
            o_ref[j * d:(j + 1) * d, :] = acc_scr[u, :, ls] * (1.0 / l_scr[u, :, ls])

    @pl.when((kj == 0) & (kj < last))
    def _():
        update(False)

    @pl.when((kj == 0) & (kj == last))
    def _():
        update(True)
        finalize()

    @pl.when((kj > 0) & (kj < last))
    def _():
        update_lagged(False)

    @pl.when((kj > 0) & (kj == last))
    def _():
        update_lagged(True)
        finalize()


def nsa_sel_attn(qt, mb, ks_aug, vst, q_norm, k_norm_max, *, kv_heads, tq, tk, hpu):
    t = qt.shape[1]
    assert k_norm_max.shape[1] * tk == t
    d = HEAD_DIM
    gw = NSA_GROUP * d
    pairs = [(qi, kj) for qi in range(t // tq) for kj in range((qi * tq + tq - 1) // tk + 1)]
    qi_tab = jnp.asarray(np.array([p[0] for p in pairs], np.int32))
    kj_tab = jnp.asarray(np.array([p[1] for p in pairs], np.int32))
    units, uw = kv_heads * NSA_GROUP // hpu, hpu * tq
    grid_spec = pltpu.PrefetchScalarGridSpec(
        num_scalar_prefetch=2,
        grid=(len(pairs),),
        in_specs=[pl.BlockSpec((kv_heads * gw, tq), lambda s, qt, kt: (0, qt[s])),
                  pl.BlockSpec((kv_heads, SEL_LANES, tq), lambda s, qt, kt: (0, 0, qt[s])),
                  pl.BlockSpec((kv_heads, tk, 2 * d), lambda s, qt, kt: (0, kt[s], 0)),
                  pl.BlockSpec((kv_heads, d, tk), lambda s, qt, kt: (0, 0, kt[s])),
                  pl.BlockSpec((kv_heads * NSA_GROUP, tq), lambda s, qt, kt: (0, qt[s])),
                  pl.BlockSpec((kv_heads, 1, 8, d), lambda s, qt, kt: (0, kt[s], 0, 0))],
        out_specs=pl.BlockSpec((kv_heads * gw, tq), lambda s, qt, kt: (0, qt[s])),
        scratch_shapes=[pltpu.VMEM((units, 2 * d, uw), MXU_DTYPE), pltpu.VMEM((units, 1, uw), F32),
                        pltpu.VMEM((units, 1, uw), F32), pltpu.VMEM((units, d, uw), F32)],
    )
    return pl.pallas_call(
        functools.partial(_nsa_sel_kernel, tq=tq, tk=tk, kv_heads=kv_heads, grp=NSA_GROUP, hpu=hpu),
        grid_spec=grid_spec,
        out_shape=jax.ShapeDtypeStruct((kv_heads * gw, t), F32),
        compiler_params=_params("arbitrary"),
        name="nsa_sel_attn",
    )(qi_tab, kj_tab, qt, mb, ks_aug, vst, q_norm, k_norm_max)


def _nsa_win_kernel(qt_ref, k0_ref, k1_ref, k2_ref, v0_ref, v1_ref, v2_ref, ocmp_ref, osel_ref, zg_ref, y_ref,
                    *, tq, grp):
    d = HEAD_DIM
    qi = pl.program_id(1)
    k = jnp.concatenate([k0_ref[0], k1_ref[0], k2_ref[0]], axis=0)
    vt = jnp.concatenate([v0_ref[0], v1_ref[0], v2_ref[0]], axis=1)
    kpos = (qi - 2) * tq + lax.broadcasted_iota(jnp.int32, (3 * tq, tq), 0)
    rel = qi * tq + lax.broadcasted_iota(jnp.int32, (3 * tq, tq), 1) - kpos
    ok = (rel >= 0) & (rel < WINDOW) & (kpos >= 0)
    gates_t = _sigmoid(zg_ref[...]).T

    def scores(j):
        return jnp.dot(k, qt_ref[j * d:(j + 1) * d, :], preferred_element_type=F32)

    s_next = scores(0)
    for j in range(grp):
        rs = slice(j * d, (j + 1) * d)
        s = s_next
        if j + 1 < grp:
            s_next = scores(j + 1)
        sm = jnp.where(ok, s, NEG_INF)
        p = jnp.where(ok, jnp.exp2(sm - jnp.max(sm, axis=0, keepdims=True)), 0.0)
        o = _dot(vt, p) * (1.0 / jnp.sum(p, axis=0, keepdims=True))
        y = (gates_t[3 * j:3 * j + 1] * ocmp_ref[rs, :] + gates_t[3 * j + 1:3 * j + 2] * osel_ref[rs, :]
             + gates_t[3 * j + 2:3 * j + 3] * o)
        y_ref[:, rs] = y.T.astype(y_ref.dtype)


def nsa_win_combine(qt, kw, vwt, o_cmp, o_sel, zs, *, kv_heads, tq, zs_block0):
    t = qt.shape[1]
    d = HEAD_DIM
    gw = NSA_GROUP * d
    k_specs = [pl.BlockSpec((1, tq, d), functools.partial(lambda g, i, b: (g, jnp.maximum(i - b, 0), 0), b=b))
               for b in (2, 1, 0)]
    v_specs = [pl.BlockSpec((1, d, tq), functools.partial(lambda g, i, b: (g, 0, jnp.maximum(i - b, 0)), b=b))
               for b in (2, 1, 0)]
    tspec = pl.BlockSpec((gw, tq), lambda g, i: (g, i))
    return pl.pallas_call(
        functools.partial(_nsa_win_kernel, tq=tq, grp=NSA_GROUP),
        grid=(kv_heads, t // tq),
        in_specs=[tspec] + k_specs + v_specs + [tspec, tspec,
                                                pl.BlockSpec((tq, 128), lambda g, i: (i, zs_block0 + g))],
        out_specs=pl.BlockSpec((tq, gw), lambda g, i: (i, g)),
        out_shape=jax.ShapeDtypeStruct((t, kv_heads * gw), MXU_DTYPE),
        compiler_params=_params("parallel", "parallel"),
        name="nsa_win_combine",
    )(qt, kw, kw, kw, vwt, vwt, vwt, o_cmp, o_sel, zs)


def _split_in_proj(w_in, b_in, sizes):
    starts = np.concatenate([[0], np.cumsum(sizes)])
    i_gate, i_mli, i_mlf = 11, 16, 17
    g0, g1 = starts[i_gate], starts[i_gate + 1]
    half = (g1 - g0) // 2

    def pad(a, width):
        return jnp.pad(a, [(0, 0)] * (a.ndim - 1) + [(0, width - a.shape[-1])])

    def part_b(a):
        return jnp.concatenate([a[..., g1:starts[i_mli]], pad(a[..., g0:g0 + half], 128),
                                pad(a[..., g0 + half:g1], 128),
                                pad(a[..., starts[i_mli]:starts[i_mlf + 1]], 128)], axis=-1)

    return int(g0), part_b(w_in), part_b(b_in)


def kernel(x, mem, norm_mix, w_in, b_in, hgrn_lb_logits, hgrn_norm, nsa_q_norm, nsa_k_norm, nsa_cmp_pos, nsa_cmp_w, mlstm_conv, mlstm_norm, w_out, norm_xattn, norm_mem, xa_wq, xa_wk, xa_wv, xa_wo, xa_q_norm, xa_k_norm, norm_mlp, mlp_w1, mlp_w2):
    _, t, d_model = x.shape
    depth = w_in.shape[0]
    d = HEAD_DIM
    hg_heads = ml_heads = d_model // (4 * d)
    nsa_heads = d_model // (2 * d)
    kv_heads = nsa_heads // NSA_GROUP
    hw, nw, kvw = hg_heads * d, nsa_heads * d, kv_heads * d
    sizes = (hw,) * 4 + (nw,) + (kvw,) * 6 + (3 * nsa_heads,) + (hw,) * 4 + (ml_heads, ml_heads)
    col_nsa_q = 4 * hw
    col_nsa_kc = col_nsa_q + nw
    xa_heads = 4
    bf = MXU_DTYPE

    half = d // 2
    inv_freq = ROPE_THETA ** (-jnp.arange(half, dtype=F32) / half)

    def rope_tables(pos):
        ang = pos[:, None] * inv_freq[None, :]
        cos, sin = jnp.cos(ang), jnp.sin(ang)
        return jnp.concatenate([cos, cos], axis=-1), jnp.concatenate([-sin, sin], axis=-1)

    cos_t, sin_t = rope_tables(jnp.arange(t, dtype=F32))
    ncp = t // CMP_STRIDE
    n_cmp, n_sel = ncp - 1, t // SEL_BLOCK
    cos_c, sin_c = rope_tables(jnp.arange(ncp, dtype=F32) * CMP_STRIDE + (CMP_LEN - 1))
    et = jnp.asarray(np.arange(t)[:, None] // SEL_BLOCK == np.arange(SEL_LANES)[None, :], bf)
    c_start = np.arange(ncp)[:, None] * CMP_STRIDE
    s_start = np.arange(SEL_LANES)[None, :] * SEL_BLOCK
    overlap = ((c_start < s_start + SEL_BLOCK) & (c_start + CMP_LEN > s_start)
               & (np.arange(ncp)[:, None] < n_cmp) & (np.arange(SEL_LANES)[None, :] < n_sel))
    ovt = jnp.asarray(overlap.T, bf)

    n_a, w_b, b_b = _split_in_proj(w_in, b_in, sizes)
    w_a, w_b = w_in[:, :, :n_a].astype(bf), w_b.astype(bf)
    w_out_c, wq_c, wo_c = (a.astype(bf) for a in (w_out, xa_wq, xa_wo))
    w1_c, w2_c, cmp_w_c = mlp_w1.astype(bf), mlp_w2.astype(bf), nsa_cmp_w.astype(bf)

    h = x.reshape(t, d_model)
    mem2 = mem.reshape(mem.shape[1], d_model)
    hn = rms_cast(h, norm_mix[0], tm=512)
    for layer in range(depth):
        z = matmul_cols(hn, w_a, b_in, layer=layer, n_tiles=3, tm=1024, tn=n_a // 3)
        zb = matmul_cols(hn, w_b, b_b, layer=layer, n_tiles=1, tm=512, tn=w_b.shape[2])
        gate_block = 4 * hw // 128
        gt = zb[:, (gate_block + 2) * 128:(gate_block + 2) * 128 + 8].T

        y_hg = hgrn2(z, hgrn_lb_logits, hgrn_norm[layer], layer=layer, heads=hg_heads, tb=512)
        y_ml = mlstm(zb, zb, gt, mlstm_conv[layer], mlstm_norm[layer], heads=ml_heads, tb=512, col0=0,
                     zs_block=gate_block + 2)

        qt, ks_aug, vst, kw, vwt, q_nrm, k_nrm = nsa_prep(z, cos_t, sin_t, et, nsa_q_norm[layer], nsa_k_norm[layer],
                                                          heads=nsa_heads, kv_heads=kv_heads, tq=512, col_q=col_nsa_q)
        kc, vct = nsa_compress(z, nsa_cmp_pos[layer], cmp_w_c[layer], nsa_k_norm[layer],
                               cos_c, sin_c, kv_heads=kv_heads, col_kc=col_nsa_kc)
        o_cmp, mb = nsa_cmp_select(qt, kc, vct, ovt, kv_heads=kv_heads, tq=256)
        o_sel = nsa_sel_attn(qt, mb, ks_aug, vst, q_nrm, k_nrm, kv_heads=kv_heads, tq=512, tk=512, hpu=1)
        y_ns = nsa_win_combine(qt, kw, vwt, o_cmp, o_sel, zb, kv_heads=kv_heads, tq=256, zs_block0=gate_block)

        kb = hw
        terms = [(y_hg, 0, 0)] + [(y_ns, c, 1 + c) for c in range(nw // kb)] + [(y_ml, 0, 1 + nw // kb)]
        h, hn = matmul_residual(terms, w_out_c, h, norm_xattn[layer], layer=layer, kb=kb, tm=512)

        k_mem, v_mem = xa_kv(mem2, norm_mem[layer], xa_wk, xa_wv, xa_k_norm[layer], layer=layer, heads=xa_heads)
        q_x = matmul_cols(hn, wq_c, None, layer=layer, n_tiles=1, tm=1024, tn=d_model)
        o_x = xa_attn(q_x, xa_q_norm[layer], k_mem, v_mem, heads=xa_heads, tm=512)
        h = matmul_residual([(o_x, c, c) for c in range(d_model // kb)], wo_c, h, None, layer=layer, kb=kb, tm=512)

        if layer + 1 < depth:
            h, hn = mlp(h, norm_mlp[layer], w1_c, w2_c, norm_mix[layer + 1], layer=layer, tm=512, tf=1024)
        else:
            h = mlp(h, norm_mlp[layer], w1_c, w2_c, None, layer=layer, tm=512, tf=1024)
    return h.reshape(x.shape)
```

```python
import functools

import jax
import jax.numpy as jnp
import numpy as np
from jax import lax
from jax.experimental import pallas as pl
from jax.experimental.pallas import tpu as pltpu

F32 = jnp.float32
MXU_DTYPE = jnp.bfloat16

EPS = 1e-6
NEG_INF = -1e30
LOG2_E = 1.4426950408889634
HEAD_DIM = 128
ROPE_THETA = 10000.0

V7X_VMEM_LIMIT_BYTES = 56 * 1024 * 1024


def _params(*semantics):
    return pltpu.CompilerParams(dimension_semantics=semantics, vmem_limit_bytes=V7X_VMEM_LIMIT_BYTES)


def _rms(x, gain):
    return x * lax.rsqrt(jnp.mean(x * x, axis=-1, keepdims=True) + EPS) * gain


def _dot(a, b):
    return jnp.dot(a.astype(MXU_DTYPE), b.astype(MXU_DTYPE), preferred_element_type=F32)


def _dot_nt(a, b):
    return lax.dot_general(a.astype(MXU_DTYPE), b.astype(MXU_DTYPE), (((1,), (1,)), ((), ())),
                           preferred_element_type=F32)


def _dot_tn(a, b):
    return lax.dot_general(a.astype(MXU_DTYPE), b.astype(MXU_DTYPE), (((0,), (0,)), ((), ())),
                           preferred_element_type=F32)


def _rms_cast_kernel(x_ref, g_ref, o_ref):
    o_ref[...] = _rms(x_ref[...], g_ref[...]).astype(o_ref.dtype)


def rms_cast(x, gain, *, tm):
    m, d = x.shape
    return pl.pallas_call(
        _rms_cast_kernel,
        grid=(m // tm,),
        in_specs=[pl.BlockSpec((tm, d), lambda i: (i, 0)), pl.BlockSpec((1, d), lambda i: (0, 0))],
        out_specs=pl.BlockSpec((tm, d), lambda i: (i, 0)),
        out_shape=jax.ShapeDtypeStruct((m, d), MXU_DTYPE),
        compiler_params=_params("parallel"),
        name="rms_cast",
    )(x, gain.reshape(1, d))


def _matmul_bias_kernel(h_ref, w_ref, b_ref, o_ref):
    o_ref[...] = jnp.dot(h_ref[...], w_ref[...], preferred_element_type=F32) + b_ref[...]


def _matmul_kernel(h_ref, w_ref, o_ref):
    o_ref[...] = jnp.dot(h_ref[...], w_ref[...], preferred_element_type=F32)


def matmul_cols(h, w, bias, *, layer, n_tiles, tm, tn):
    m, d = h.shape
    in_specs = [pl.BlockSpec((tm, d), lambda j, i: (i, 0)),
                pl.BlockSpec((None, d, tn), lambda j, i: (layer, 0, j))]
    args = [h, w]
    if bias is not None:
        in_specs.append(pl.BlockSpec((None, 1, tn), lambda j, i: (layer, 0, j)))
        args.append(bias.reshape(bias.shape[0], 1, bias.shape[1]))
    return pl.pallas_call(
        _matmul_kernel if bias is None else _matmul_bias_kernel,
        grid=(n_tiles, m // tm),
        in_specs=in_specs,
        out_specs=pl.BlockSpec((tm, tn), lambda j, i: (i, j)),
        out_shape=jax.ShapeDtypeStruct((m, n_tiles * tn), F32),
        compiler_params=_params("parallel", "parallel"),
        name="matmul_cols",
    )(*args)


def _matmul_res_kernel(*refs, n_terms, with_norm):
    ys, ws = refs[:n_terms], refs[n_terms:2 * n_terms]
    res_ref = refs[2 * n_terms]
    acc = res_ref[...]
    for y_ref, w_ref in zip(ys, ws):
        acc = acc + jnp.dot(y_ref[...].astype(MXU_DTYPE), w_ref[...], preferred_element_type=F32)
    if with_norm:
        g_ref, o_ref, hn_ref = refs[2 * n_terms + 1:]
        hn_ref[...] = _rms(acc, g_ref[...]).astype(hn_ref.dtype)
    else:
        o_ref = refs[2 * n_terms + 1]
    o_ref[...] = acc


def matmul_residual(terms, w, res, next_gain, *, layer, kb, tm):
    m, n = res.shape
    y_specs = [pl.BlockSpec((tm, kb), functools.partial(lambda i, c: (i, c), c=cb)) for _, cb, _ in terms]
    w_specs = [pl.BlockSpec((None, kb, n), functools.partial(lambda i, r: (layer, r, 0), r=rb)) for _, _, rb in terms]
    row_spec = pl.BlockSpec((tm, n), lambda i: (i, 0))
    with_norm = next_gain is not None
    return pl.pallas_call(
        functools.partial(_matmul_res_kernel, n_terms=len(terms), with_norm=with_norm),
        grid=(m // tm,),
        in_specs=y_specs + w_specs + [row_spec] + ([pl.BlockSpec((1, n), lambda i: (0, 0))] if with_norm else []),
        out_specs=[row_spec, row_spec] if with_norm else row_spec,
        out_shape=([jax.ShapeDtypeStruct((m, n), F32), jax.ShapeDtypeStruct((m, n), MXU_DTYPE)] if with_norm
                   else jax.ShapeDtypeStruct((m, n), F32)),
        compiler_params=_params("parallel"),
        name="matmul_residual",
    )(*[t[0] for t in terms], *([w] * len(terms)), res, *([next_gain.reshape(1, n)] if with_norm else []))


def _mlp_kernel(*refs, with_norm):
    if with_norm:
        x_ref, g_ref, w1_ref, w2_ref, gn_ref, o_ref, hn_ref, h_scr = refs
    else:
        x_ref, g_ref, w1_ref, w2_ref, o_ref, h_scr = refs

    @pl.when(pl.program_id(1) == 0)
    def _():
        x = x_ref[...]
        h_scr[...] = _rms(x, g_ref[...]).astype(h_scr.dtype)
        o_ref[...] = x

    u = jnp.dot(h_scr[...], w1_ref[...], preferred_element_type=F32)
    a = jnp.square(jnp.maximum(u, 0.0)).astype(MXU_DTYPE)
    o_ref[...] += jnp.dot(a, w2_ref[...], preferred_element_type=F32)

    if with_norm:
        @pl.when(pl.program_id(1) == pl.num_programs(1) - 1)
        def _():
            hn_ref[...] = _rms(o_ref[...], gn_ref[...]).astype(hn_ref.dtype)


def mlp(x, gain, w1, w2, next_gain, *, layer, tm, tf):
    m, d = x.shape
    ff = w1.shape[2]
    row_spec = pl.BlockSpec((tm, d), lambda i, f: (i, 0))
    gain_spec = pl.BlockSpec((1, d), lambda i, f: (0, 0))
    with_norm = next_gain is not None
    return pl.pallas_call(
        functools.partial(_mlp_kernel, with_norm=with_norm),
        grid=(m // tm, ff // tf),
        in_specs=[row_spec, gain_spec,
                  pl.BlockSpec((None, d, tf), lambda i, f: (layer, 0, f)),
                  pl.BlockSpec((None, tf, d), lambda i, f: (layer, f, 0))] + ([gain_spec] if with_norm else []),
        out_specs=[row_spec, row_spec] if with_norm else row_spec,
        out_shape=([jax.ShapeDtypeStruct((m, d), F32), jax.ShapeDtypeStruct((m, d), MXU_DTYPE)] if with_norm
                   else jax.ShapeDtypeStruct((m, d), F32)),
        scratch_shapes=[pltpu.VMEM((tm, d), MXU_DTYPE)],
        compiler_params=_params("parallel", "arbitrary"),
        name="mlp",
    )(x, gain.reshape(1, d), w1, w2, *([next_gain.reshape(1, d)] if with_norm else []))


def _xa_kv_kernel(mem_ref, g_ref, wk_ref, wv_ref, kg_ref, k_ref, v_ref):
    mn = _rms(mem_ref[...], g_ref[...])
    k_ref[...] = _rms(_dot(mn, wk_ref[...]), kg_ref[...]).astype(k_ref.dtype)
    v_ref[...] = _dot(mn, wv_ref[...]).astype(v_ref.dtype)


def xa_kv(mem, gain, wk, wv, k_gain, *, layer, heads):
    ml, d = mem.shape
    hd = d // heads
    return pl.pallas_call(
        _xa_kv_kernel,
        grid=(heads,),
        in_specs=[
            pl.BlockSpec((ml, d), lambda h: (0, 0)),
            pl.BlockSpec((1, d), lambda h: (0, 0)),
            pl.BlockSpec((None, d, hd), lambda h: (layer, 0, h)),
            pl.BlockSpec((None, d, hd), lambda h: (layer, 0, h)),
            pl.BlockSpec((1, hd), lambda h: (0, 0)),
        ],
        out_specs=[pl.BlockSpec((ml, hd), lambda h: (0, h))] * 2,
        out_shape=[jax.ShapeDtypeStruct((ml, d), MXU_DTYPE)] * 2,
        compiler_params=_params("parallel"),
        name="xa_kv",
    )(mem, gain.reshape(1, d), wk, wv, k_gain.reshape(1, hd))


def _xa_attn_kernel(q_ref, qg_ref, k_ref, v_ref, o_ref, *, heads, hd):
    scale = hd ** -0.5
    for h in range(heads):
        cs = slice(h * hd, (h + 1) * hd)
        q = _rms(q_ref[:, cs], qg_ref[...]) * scale
        s = _dot_nt(q, k_ref[:, cs])
        p = jnp.exp(s - jnp.max(s, axis=-1, keepdims=True))
        p = p / jnp.sum(p, axis=-1, keepdims=True)
        o_ref[:, cs] = _dot(p, v_ref[:, cs]).astype(o_ref.dtype)


def xa_attn(q, q_gain, k, v, *, heads, tm):
    m, d = q.shape
    ml = k.shape[0]
    hd = d // heads
    return pl.pallas_call(
        functools.partial(_xa_attn_kernel, heads=heads, hd=hd),
        grid=(m // tm,),
        in_specs=[
            pl.BlockSpec((tm, d), lambda i: (i, 0)),
            pl.BlockSpec((1, hd), lambda i: (0, 0)),
            pl.BlockSpec((ml, d), lambda i: (0, 0)),
            pl.BlockSpec((ml, d), lambda i: (0, 0)),
        ],
        out_specs=pl.BlockSpec((tm, d), lambda i: (i, 0)),
        out_shape=jax.ShapeDtypeStruct((m, d), MXU_DTYPE),
        compiler_params=_params("parallel"),
        name="xa_attn",
    )(q, q_gain.reshape(1, hd), k, v)


def _seg_cumsum_rows(x, seg):
    pos = lax.broadcasted_iota(jnp.int32, x.shape, 0) % seg
    k = 1
    while k < seg:
        x = x + jnp.where(pos >= k, pltpu.roll(x, k, axis=0), 0.0)
        k *= 2
    return x


def _seg_cumsum_lanes(x, seg):
    pos = lax.broadcasted_iota(jnp.int32, x.shape, 1) % seg
    k = 1
    while k < seg:
        x = x + jnp.where(pos >= k, pltpu.roll(x, k, axis=1), 0.0)
        k *= 2
    return x


def _sigmoid(x):
    return 1.0 / (1.0 + jnp.exp(-x))


def _log_sigmoid(x):
    return jnp.minimum(x, 0.0) - jnp.log(1.0 + jnp.exp(-jnp.abs(x)))


HG_SUB = 16
HG_MIN_FORGET = 1e-20


def _hgrn2_kernel(q_ref, f_ref, i_ref, g_ref, lbl_ref, ng_ref, y_ref, st_ref, *, layer, heads, tb):
    dk = HEAD_DIM

    @pl.when(pl.program_id(0) == 0)
    def _():
        st_ref[...] = jnp.zeros_like(st_ref)

    logits = lbl_ref[...]
    e = jnp.exp(logits - jnp.max(logits, axis=0, keepdims=True))
    probs = e / jnp.sum(e, axis=0, keepdims=True)
    lb_all = jnp.zeros_like(probs[0:1])
    for l in range(1, layer + 1):
        lb_all = lb_all + probs[l:l + 1]
    half = HG_SUB // 2
    row_half = lax.broadcasted_iota(jnp.int32, (half, dk), 0)

    n_sub = tb // HG_SUB

    def gates(c):
        r = pl.ds(pl.multiple_of(c * HG_SUB, HG_SUB), HG_SUB)
        pre = []
        for h in range(heads):
            cs = slice(h * dk, (h + 1) * dk)
            lb = lb_all[:, cs]
            q, zf, v = q_ref[r, cs], f_ref[r, cs], i_ref[r, cs]
            forget = lb + (1.0 - lb) * _sigmoid(zf)
            log_f = jnp.log(jnp.maximum(forget, HG_MIN_FORGET))
            key = (1.0 - lb) * _sigmoid(-zf)
            b = _seg_cumsum_rows(log_f, HG_SUB) * LOG2_E
            pre.append((q, v, key, b))
        return pre

    def body(c, pre):
        r = pl.ds(pl.multiple_of(c * HG_SUB, HG_SUB), HG_SUB)
        pre_next = gates(jnp.minimum(c + 1, n_sub - 1))
        pre = [(q, v, key, b, b[HG_SUB - 1:HG_SUB]) for q, v, key, b in pre]
        inter = [_dot_nt(q * jnp.exp2(b), st_ref[h]) for h, (q, v, key, b, b_end) in enumerate(pre)]
        upd = [_dot_tn(v, key * jnp.exp2(b_end - b)) for q, v, key, b, b_end in pre]
        for h in range(heads):
            cs = slice(h * dk, (h + 1) * dk)
            q, v, key, b, b_end = pre[h]
            q_lo, q_hi, b_lo, b_hi = q[0:half], q[half:], b[0:half], b[half:]
            o_lo = jnp.zeros((half, dk), F32)
            o_hi = jnp.zeros((half, dk), F32)
            for s in range(HG_SUB):
                bs, ks, vs = b[s:s + 1], key[s:s + 1], v[s:s + 1]
                if s < half:
                    decay = jnp.exp2(jnp.where(row_half >= s, b_lo - bs, NEG_INF))
                    o_lo = o_lo + jnp.sum(q_lo * decay * ks, axis=-1, keepdims=True) * vs
                    decay = jnp.exp2(b_hi - bs)
                else:
                    decay = jnp.exp2(jnp.where(row_half >= s - half, b_hi - bs, NEG_INF))
                o_hi = o_hi + jnp.sum(q_hi * decay * ks, axis=-1, keepdims=True) * vs
            o = jnp.concatenate([o_lo, o_hi], axis=0) + inter[h]
            st_ref[h] = jnp.exp2(b_end) * st_ref[h] + upd[h]
            g = g_ref[r, cs]
            y = _rms(o, ng_ref[:, cs]) * (g * _sigmoid(g))
            y_ref[r, cs] = y.astype(y_ref.dtype)
        return pre_next

    lax.fori_loop(0, n_sub, body, gates(jnp.int32(0)))


def hgrn2(z, lb_logits, norm_gain, *, layer, heads, tb):
    t = z.shape[0]
    w = heads * HEAD_DIM
    depth = lb_logits.shape[0]
    return pl.pallas_call(
        functools.partial(_hgrn2_kernel, layer=layer, heads=heads, tb=tb),
        grid=(t // tb,),
        in_specs=[pl.BlockSpec((tb, w), functools.partial(lambda i, c: (i, c), c=c)) for c in range(4)] + [
            pl.BlockSpec((depth, w), lambda i: (0, 0)),
            pl.BlockSpec((1, w), lambda i: (0, 0)),
        ],
        out_specs=pl.BlockSpec((tb, w), lambda i: (i, 0)),
        out_shape=jax.ShapeDtypeStruct((t, w), MXU_DTYPE),
        scratch_shapes=[pltpu.VMEM((heads, HEAD_DIM, HEAD_DIM), F32)],
        compiler_params=_params("arbitrary"),
        name="hgrn2",
    )(z, z, z, z, lb_logits, norm_gain.reshape(1, w))


ML_CHUNK = 64
ML_CONV = 4
ML_TAIL = 8
ML_AUG = HEAD_DIM + 16
ZS_ML_I = 0
ZS_ML_F = 4


def _mlstm_kernel(q_ref, k_ref, v_ref, o_ref, zs_ref, gt_ref, cw_ref, ng_ref, y_ref,
                  conv_scr, st_scr, m_scr, *, heads, tb):
    d = HEAD_DIM
    w = heads * d

    @pl.when(pl.program_id(0) == 0)
    def _():
        conv_scr[0:ML_TAIL, :] = jnp.zeros((ML_TAIL, 2 * w), F32)
        st_scr[...] = jnp.zeros_like(st_scr)
        m_scr[...] = jnp.zeros_like(m_scr)

    conv_scr[ML_TAIL:ML_TAIL + tb, 0:w] = q_ref[...]
    conv_scr[ML_TAIL:ML_TAIL + tb, w:2 * w] = k_ref[...]
    acc = jnp.zeros((tb, 2 * w), F32)
    for j in range(ML_CONV):
        off = ML_TAIL - (ML_CONV - 1) + j
        acc = acc + conv_scr[off:off + tb, :] * cw_ref[j:j + 1, :]
    conv_scr[0:ML_TAIL, :] = conv_scr[tb:tb + ML_TAIL, :]
    qk = acc * _sigmoid(acc)

    pair = 2 * ML_CHUNK
    zs = zs_ref[...]
    b_col_all = _seg_cumsum_rows(_log_sigmoid(zs), ML_CHUNK)
    gt = gt_ref[...]
    b_row_all = _seg_cumsum_lanes(_log_sigmoid(gt), ML_CHUNK)
    s_idx = lax.broadcasted_iota(jnp.int32, (pair, pair), 0)
    t_idx = lax.broadcasted_iota(jnp.int32, (pair, pair), 1)
    visible = (s_idx // ML_CHUNK == t_idx // ML_CHUNK) & (s_idx <= t_idx)
    lane = lax.broadcasted_iota(jnp.int32, (1, pair), 1)
    in_chunk = [lane < ML_CHUNK, lane >= ML_CHUNK]
    ones_rows = (lax.broadcasted_iota(jnp.int32, (ML_AUG - d, pair), 0) == 0).astype(F32)

    def rows(p):
        return slice(p * pair, (p + 1) * pair)

    def stage_scores(p):
        out = []
        for h in range(heads):
            qt = (qk[rows(p), h * d:(h + 1) * d] * (d ** -0.5)).T
            out.append((qt, _dot(qk[rows(p), w + h * d:w + (h + 1) * d], qt)))
        return out

    def stage_free(p, sc):
        rs = rows(p)
        out = []
        for h in range(heads):
            qt, s_raw = sc[h]
            b_row = b_row_all[heads + h:heads + h + 1, rs]
            li_row = gt[h:h + 1, rs]
            u_col = b_col_all[rs, ZS_ML_F + h:ZS_ML_F + h + 1] - zs[rs, ZS_ML_I + h:ZS_ML_I + h + 1]
            d_log = jnp.where(visible, b_row - u_col, NEG_INF)
            dmax = jnp.max(d_log, axis=0, keepdims=True)
            s_t = s_raw * jnp.exp(d_log - dmax)
            vt_aug = jnp.concatenate([v_ref[rs, h * d:(h + 1) * d].T, ones_rows], axis=0)
            kc = qk[rs, w + h * d:w + (h + 1) * d]
            per_chunk = []
            for c in range(2):
                b_end = b_row[:, (c + 1) * ML_CHUNK - 1:(c + 1) * ML_CHUNK]
                w_state = jnp.where(in_chunk[c], b_end - b_row + li_row, NEG_INF)
                wmax = jnp.max(w_state, axis=1, keepdims=True)
                w_row = jnp.exp(w_state - wmax)
                per_chunk.append((b_end, wmax, _dot(vt_aug * w_row, kc)))
            out.append(dict(qt=qt, b_row=b_row, dmax=dmax, intra=_dot(vt_aug, s_t), per_chunk=per_chunk))
        return out

    n_pairs = tb // pair
    sc = {0: stage_scores(0)}
    if n_pairs > 1:
        sc[1] = stage_scores(1)
    free = {0: stage_free(0, sc.pop(0))}
    for p in range(n_pairs):
        rs = rows(p)
        if p + 2 < n_pairs:
            sc[p + 2] = stage_scores(p + 2)
        if p + 1 < n_pairs:
            free[p + 1] = stage_free(p + 1, sc.pop(p + 1))
        cur = free.pop(p)
        for h in range(heads):
            cs = slice(h * d, (h + 1) * d)
            f = cur[h]
            num, m_tok = None, None
            for c in range(2):
                b_end, wmax, upd = f["per_chunk"][c]
                state, m_prev = st_scr[h], m_scr[h]
                inter = _dot(state, f["qt"])
                inter_log = f["b_row"] + m_prev
                m_t = jnp.maximum(inter_log, f["dmax"])
                num_c = jnp.exp(inter_log - m_t) * inter + jnp.exp(f["dmax"] - m_t) * f["intra"]
                num = num_c if c == 0 else jnp.where(in_chunk[0], num, num_c)
                m_tok = m_t if c == 0 else jnp.where(in_chunk[0], m_tok, m_t)
                m_new = jnp.maximum(b_end + m_prev, wmax)
                st_scr[h] = jnp.exp(b_end + m_prev - m_new) * state + jnp.exp(wmax - m_new) * upd
                m_scr[h] = m_new
            qn = num[d:d + 1]
            hcell = num[0:d] / jnp.maximum(jnp.abs(qn), jnp.exp(-m_tok))
            hn = hcell * lax.rsqrt(jnp.mean(hcell * hcell, axis=0, keepdims=True) + EPS)
            y = hn.T * ng_ref[:, cs] * _sigmoid(o_ref[rs, cs])
            y_ref[rs, cs] = y.astype(y_ref.dtype)


def mlstm(z, zs, gt, conv_w, norm_gain, *, heads, tb, col0, zs_block):
    t = z.shape[0]
    w = heads * HEAD_DIM
    cb = col0 // w
    return pl.pallas_call(
        functools.partial(_mlstm_kernel, heads=heads, tb=tb),
        grid=(t // tb,),
        in_specs=[pl.BlockSpec((tb, w), functools.partial(lambda i, c: (i, c), c=cb + c)) for c in range(4)] + [
            pl.BlockSpec((tb, 128), lambda i: (i, zs_block)),
            pl.BlockSpec((8, tb), lambda i: (0, i)),
            pl.BlockSpec((ML_CONV, 2 * w), lambda i: (0, 0)),
            pl.BlockSpec((1, w), lambda i: (0, 0)),
        ],
        out_specs=pl.BlockSpec((tb, w), lambda i: (i, 0)),
        out_shape=jax.ShapeDtypeStruct((t, w), MXU_DTYPE),
        scratch_shapes=[
            pltpu.VMEM((tb + ML_TAIL, 2 * w), F32),
            pltpu.VMEM((heads, ML_AUG, HEAD_DIM), F32),
            pltpu.VMEM((heads, 1, 1), F32),
        ],
        compiler_params=_params("arbitrary"),
        name="mlstm",
    )(z, z, z, z, zs, gt, conv_w, norm_gain.reshape(1, w))


NSA_GROUP = 4
CMP_LEN = 32
CMP_STRIDE = 16
SEL_BLOCK = 64
N_SELECT = 16
WINDOW = 512
FORCE_BONUS = 1e4
SEL_LANES = 128
SEL_LAG_MAX_GAP = 64.0
SEL_BOUND_MARGIN = 1.02


def _rope(x, cos_t, sin_t):
    return x * cos_t + pltpu.roll(x, HEAD_DIM // 2, axis=1) * sin_t


def _nsa_prep_kernel(q_ref, ks_ref, vs_ref, kw_ref, vw_ref, cos_ref, sin_ref, et_ref, qg_ref, kg_ref,
                     qh_ref, ksa_ref, vso_ref, kwr_ref, vwo_ref, qn_ref, kn_ref, *, heads, kv_heads):
    d = HEAD_DIM
    cos_t, sin_t = cos_ref[...], sin_ref[...]
    for h in range(heads):
        cs = slice(h * d, (h + 1) * d)
        qt = (_rope(_rms(q_ref[:, cs], qg_ref[...]), cos_t, sin_t) * (d ** -0.5 * LOG2_E)).T
        qh_ref[cs, :] = qt.astype(qh_ref.dtype)
        qn_ref[h:h + 1, :] = jnp.sqrt(jnp.sum(qt * qt, axis=0, keepdims=True))
    for g in range(kv_heads):
        cs = slice(g * d, (g + 1) * d)
        ks = _rope(_rms(ks_ref[:, cs], kg_ref[1:2]), cos_t, sin_t)
        ksa_ref[g, :, 0:d] = ks.astype(ksa_ref.dtype)
        k_norm2 = jnp.max(jnp.sum(ks * ks, axis=-1, keepdims=True), axis=0, keepdims=True)
        kn_ref[g, 0] = jnp.broadcast_to(jnp.sqrt(k_norm2), kn_ref.shape[2:])
        ksa_ref[g, :, d:2 * d] = et_ref[...]
        kwr_ref[g] = _rope(_rms(kw_ref[:, cs], kg_ref[2:3]), cos_t, sin_t).astype(kwr_ref.dtype)
        vso_ref[g] = vs_ref[:, cs].T.astype(vso_ref.dtype)
        vwo_ref[g] = vw_ref[:, cs].T.astype(vwo_ref.dtype)


def nsa_prep(z, cos_t, sin_t, et, q_gain, k_gains, *, heads, kv_heads, tq, col_q):
    t = z.shape[0]
    d = HEAD_DIM
    kvw = kv_heads * d
    qb = col_q // (heads * d)
    kb = (col_q + heads * d) // kvw

    def zcol(width, c):
        return pl.BlockSpec((tq, width), functools.partial(lambda i, c: (i, c), c=c))

    k_out = pl.BlockSpec((kv_heads, tq, d), lambda i: (0, i, 0))
    k_shape = jax.ShapeDtypeStruct((kv_heads, t, d), MXU_DTYPE)
    vt_out = pl.BlockSpec((kv_heads, d, tq), lambda i: (0, 0, i))
    vt_shape = jax.ShapeDtypeStruct((kv_heads, d, t), MXU_DTYPE)
    return pl.pallas_call(
        functools.partial(_nsa_prep_kernel, heads=heads, kv_heads=kv_heads),
        grid=(t // tq,),
        in_specs=[zcol(heads * d, qb), zcol(kvw, kb + 2), zcol(kvw, kb + 3), zcol(kvw, kb + 4), zcol(kvw, kb + 5),
                  pl.BlockSpec((tq, d), lambda i: (i, 0)), pl.BlockSpec((tq, d), lambda i: (i, 0)),
                  pl.BlockSpec((tq, SEL_LANES), lambda i: (i, 0)),
                  pl.BlockSpec((1, d), lambda i: (0, 0)), pl.BlockSpec((3, d), lambda i: (0, 0))],
        out_specs=[pl.BlockSpec((heads * d, tq), lambda i: (0, i)),
                   pl.BlockSpec((kv_heads, tq, 2 * d), lambda i: (0, i, 0)), vt_out, k_out, vt_out,
                   pl.BlockSpec((heads, tq), lambda i: (0, i)),
                   pl.BlockSpec((kv_heads, 1, 8, d), lambda i: (0, i, 0, 0))],
        out_shape=[jax.ShapeDtypeStruct((heads * d, t), MXU_DTYPE),
                   jax.ShapeDtypeStruct((kv_heads, t, 2 * d), MXU_DTYPE), vt_shape, k_shape, vt_shape,
                   jax.ShapeDtypeStruct((heads, t), F32),
                   jax.ShapeDtypeStruct((kv_heads, t // tq, 8, d), F32)],
        compiler_params=_params("parallel"),
        name="nsa_prep",
    )(z, z, z, z, z, cos_t, sin_t, et, q_gain.reshape(1, d), k_gains)


def _nsa_compress_kernel(zk_ref, zv_ref, pe_ref, w_ref, kg_ref, cos_ref, sin_ref, kc_ref, vc_ref, *, ncp):
    d = HEAD_DIM
    half = CMP_LEN // 2
    row = lax.broadcasted_iota(jnp.int32, (ncp, d), 0)

    def compress(z_ref, which):
        lo = jnp.zeros((ncp, d), F32)
        hi = jnp.zeros((ncp, d), F32)
        for l in range(half):
            xl = z_ref[pl.ds(l, ncp, stride=CMP_STRIDE), :]
            lo = lo + _dot(xl + pe_ref[which, l:l + 1, :], w_ref[which, l])
            hi = hi + _dot(xl + pe_ref[which, half + l:half + l + 1, :], w_ref[which, half + l])
        out = lo + pltpu.roll(hi, ncp - 1, axis=0)
        return jnp.where(row < ncp - 1, out, 0.0)

    kc_ref[0] = _rope(_rms(compress(zk_ref, 0), kg_ref[0:1]), cos_ref[...], sin_ref[...]).astype(kc_ref.dtype)
    vc_ref[0] = compress(zv_ref, 1).T.astype(vc_ref.dtype)


def nsa_compress(z, cmp_pos, cmp_w, k_gains, cos_c, sin_c, *, kv_heads, col_kc):
    t = z.shape[0]
    d = HEAD_DIM
    ncp = t // CMP_STRIDE
    kb = col_kc // d
    out_spec = pl.BlockSpec((1, ncp, d), lambda g: (g, 0, 0))
    out_shape = jax.ShapeDtypeStruct((kv_heads, ncp, d), MXU_DTYPE)
    return pl.pallas_call(
        functools.partial(_nsa_compress_kernel, ncp=ncp),
        grid=(kv_heads,),
        in_specs=[pl.BlockSpec((t, d), lambda g: (0, kb + g)),
                  pl.BlockSpec((t, d), lambda g: (0, kb + kv_heads + g)),
                  pl.BlockSpec((2, CMP_LEN, d), lambda g: (0, 0, 0)),
                  pl.BlockSpec((2, CMP_LEN, d, d), lambda g: (0, 0, 0, 0)),
                  pl.BlockSpec((3, d), lambda g: (0, 0)),
                  pl.BlockSpec((ncp, d), lambda g: (0, 0)), pl.BlockSpec((ncp, d), lambda g: (0, 0))],
        out_specs=[out_spec, pl.BlockSpec((1, d, ncp), lambda g: (g, 0, 0))],
        out_shape=[out_shape, jax.ShapeDtypeStruct((kv_heads, d, ncp), MXU_DTYPE)],
        compiler_params=_params("parallel"),
        name="nsa_compress",
    )(z, z, cmp_pos, cmp_w, k_gains, cos_c, sin_c)


def _nsa_cmp_kernel(qt_ref, kc_ref, vct_ref, ovt_ref, ocmp_ref, mb_ref, *, tq, grp, ncp):
    d = HEAD_DIM
    qi = pl.program_id(1)
    cmp_end = lax.broadcasted_iota(jnp.int32, (ncp, tq), 0) * CMP_STRIDE + (CMP_LEN - 1)
    vis = cmp_end <= qi * tq + lax.broadcasted_iota(jnp.int32, (ncp, tq), 1)
    psum = None

    def scores(j):
        return jnp.dot(kc_ref[0], qt_ref[j * d:(j + 1) * d, :], preferred_element_type=F32)

    s_next = scores(0)
    for j in range(grp):
        s = s_next
        if j + 1 < grp:
            s_next = scores(j + 1)
        sm = jnp.where(vis, s, NEG_INF)
        p = jnp.where(vis, jnp.exp2(sm - jnp.max(sm, axis=0, keepdims=True)), 0.0)
        l = jnp.sum(p, axis=0, keepdims=True)
        p = p * (1.0 / jnp.where(l > 0.0, l, 1.0))
        ocmp_ref[j * d:(j + 1) * d, :] = _dot(vct_ref[0], p)
        psum = p if psum is None else psum + p
    hi = psum.astype(MXU_DTYPE)
    lo = (psum - hi.astype(F32)).astype(MXU_DTYPE)
    imp = (jnp.dot(ovt_ref[...], hi, preferred_element_type=F32)
           + jnp.dot(ovt_ref[...], lo, preferred_element_type=F32))
    blk = lax.broadcasted_iota(jnp.int32, (SEL_LANES, tq), 0)
    cur = (qi * tq + lax.broadcasted_iota(jnp.int32, (SEL_LANES, tq), 1)) // SEL_BLOCK
    forced = (blk == 0) | (blk == cur) | (blk == cur - 1)
    score = jnp.where(blk <= cur, imp + jnp.where(forced, FORCE_BONUS, 0.0), NEG_INF)
    blk_f = blk.astype(F32)
    bias = jnp.full((SEL_LANES, tq), NEG_INF, F32)
    for _ in range(N_SELECT):
        mx = jnp.max(score, axis=0, keepdims=True)
        first = jnp.min(jnp.where(score == mx, blk_f, float(SEL_LANES)), axis=0, keepdims=True)
        pick = blk_f == first
        bias = jnp.where(pick, jnp.where(mx > 0.5 * NEG_INF, 0.0, NEG_INF), bias)
        score = jnp.where(pick, -jnp.inf, score)
    mb_ref[0] = bias.astype(mb_ref.dtype)


def nsa_cmp_select(qt, kc, vct, ovt, *, kv_heads, tq):
    t = qt.shape[1]
    d = HEAD_DIM
    ncp = kc.shape[1]
    gw = NSA_GROUP * d
    return pl.pallas_call(
        functools.partial(_nsa_cmp_kernel, tq=tq, grp=NSA_GROUP, ncp=ncp),
        grid=(kv_heads, t // tq),
        in_specs=[pl.BlockSpec((gw, tq), lambda g, i: (g, i)),
                  pl.BlockSpec((1, ncp, d), lambda g, i: (g, 0, 0)),
                  pl.BlockSpec((1, d, ncp), lambda g, i: (g, 0, 0)),
                  pl.BlockSpec((SEL_LANES, ncp), lambda g, i: (0, 0))],
        out_specs=[pl.BlockSpec((gw, tq), lambda g, i: (g, i)),
                   pl.BlockSpec((1, SEL_LANES, tq), lambda g, i: (g, 0, i))],
        out_shape=[jax.ShapeDtypeStruct((kv_heads * gw, t), F32),
                   jax.ShapeDtypeStruct((kv_heads, SEL_LANES, t), MXU_DTYPE)],
        compiler_params=_params("parallel", "parallel"),
        name="nsa_cmp_select",
    )(qt, kc, vct, ovt)


def _nsa_sel_kernel(qi_tab, kj_tab, qt_ref, mb_ref, ka_ref, vt_ref, qn_ref, kn_ref, o_ref,
                    qa_scr, m_scr, l_scr, acc_scr,
                    *, tq, tk, kv_heads, grp, hpu):
    d = HEAD_DIM
    step = pl.program_id(0)
    qi, kj = qi_tab[step], kj_tab[step]
    last = (qi * tq + tq - 1) // tk
    heads = kv_heads * grp
    units = heads // hpu
    uw = hpu * tq

    def kv_of(u):
        return u * hpu // grp

    @pl.when(kj == 0)
    def _():
        for j in range(heads):
            u, ls = j // hpu, slice((j % hpu) * tq, (j % hpu + 1) * tq)
            qa_scr[u, 0:d, ls] = qt_ref[j * d:(j + 1) * d, :]
            qa_scr[u, d:2 * d, ls] = mb_ref[j // grp]
        m_scr[...] = jnp.full_like(m_scr, NEG_INF)
        l_scr[...] = jnp.zeros_like(l_scr)
        acc_scr[...] = jnp.zeros_like(acc_scr)

    def masked_scores(diagonal):
        if diagonal:
            causal = (kj * tk + lax.broadcasted_iota(jnp.int32, (tk, uw), 0)
                      <= qi * tq + lax.broadcasted_iota(jnp.int32, (tk, uw), 1) % tq)

        def scores(u):
            return jnp.dot(ka_ref[kv_of(u)], qa_scr[u], preferred_element_type=F32)

        s_next = scores(0)
        for u in range(units):
            s = s_next
            if u + 1 < units:
                s_next = scores(u + 1)
            yield u, (jnp.where(causal, s, NEG_INF) if diagonal else s)

    def update(diagonal):
        for u, s in masked_scores(diagonal):
            m_prev = m_scr[u]
            m_new = jnp.maximum(m_prev, jnp.max(s, axis=0, keepdims=True))
            alpha = jnp.exp2(m_prev - m_new)
            p = jnp.exp2(s - m_new)
            l_scr[u] = alpha * l_scr[u] + jnp.sum(p, axis=0, keepdims=True)
            acc_scr[u] = alpha * acc_scr[u] + _dot(vt_ref[kv_of(u)], p)
            m_scr[u] = m_new

    def update_lagged(diagonal):
        gap = None
        for u in range(units):
            heads_u = range(u * hpu, (u + 1) * hpu)
            q_norm = jnp.concatenate([qn_ref[j:j + 1, :] for j in heads_u], axis=1)
            bound = q_norm * (kn_ref[kv_of(u), 0, 0:1, 0:1] * SEL_BOUND_MARGIN)
            gap = bound - m_scr[u] if gap is None else jnp.maximum(gap, bound - m_scr[u])
        safe = jnp.max(gap) <= SEL_LAG_MAX_GAP

        @pl.when(safe)
        def _():
            for u, s in masked_scores(diagonal):
                m_prev = m_scr[u]
                p = jnp.exp2(s - m_prev)
                m_new = jnp.maximum(m_prev, jnp.max(s, axis=0, keepdims=True))
                alpha = jnp.exp2(m_prev - m_new)
                l_scr[u] = (l_scr[u] + jnp.sum(p, axis=0, keepdims=True)) * alpha
                acc_scr[u] = (acc_scr[u] + _dot(vt_ref[kv_of(u)], p)) * alpha
                m_scr[u] = m_new

        @pl.when(jnp.logical_not(safe))
        def _():
            update(diagonal)

    def finalize():
        for j in range(heads):
            u, ls = j // hpu, slice((j % hpu) * tq, (j % hpu + 1) * tq)
            o_ref[j * d:(j + 1) * d, :] = acc_scr[u, :, ls] * (1.0 / l_scr[u, :, ls])

    @pl.when((kj == 0) & (kj < last))
    def _():
        update(False)

    @pl.when((kj == 0) & (kj == last))
    def _():
        update(True)
        finalize()

    @pl.when((kj > 0) & (kj < last))
    def _():
        update_lagged(False)

    @pl.when((kj > 0) & (kj == last))
    def _():
        update_lagged(True)
        finalize()


def nsa_sel_attn(qt, mb, ks_aug, vst, q_norm, k_norm_max, *, kv_heads, tq, tk, hpu):
    t = qt.shape[1]
    assert k_norm_max.shape[1] * tk == t
    d = HEAD_DIM
    gw = NSA_GROUP * d
    pairs = [(qi, kj) for qi in range(t // tq) for kj in range((qi * tq + tq - 1) // tk + 1)]
    qi_tab = jnp.asarray(np.array([p[0] for p in pairs], np.int32))
    kj_tab = jnp.asarray(np.array([p[1] for p in pairs], np.int32))
    units, uw = kv_heads * NSA_GROUP // hpu, hpu * tq
    grid_spec = pltpu.PrefetchScalarGridSpec(
        num_scalar_prefetch=2,
        grid=(len(pairs),),
        in_specs=[pl.BlockSpec((kv_heads * gw, tq), lambda s, qt, kt: (0, qt[s])),
                  pl.BlockSpec((kv_heads, SEL_LANES, tq), lambda s, qt, kt: (0, 0, qt[s])),
                  pl.BlockSpec((kv_heads, tk, 2 * d), lambda s, qt, kt: (0, kt[s], 0)),
                  pl.BlockSpec((kv_heads, d, tk), lambda s, qt, kt: (0, 0, kt[s])),
                  pl.BlockSpec((kv_heads * NSA_GROUP, tq), lambda s, qt, kt: (0, qt[s])),
                  pl.BlockSpec((kv_heads, 1, 8, d), lambda s, qt, kt: (0, kt[s], 0, 0))],
        out_specs=pl.BlockSpec((kv_heads * gw, tq), lambda s, qt, kt: (0, qt[s])),
        scratch_shapes=[pltpu.VMEM((units, 2 * d, uw), MXU_DTYPE), pltpu.VMEM((units, 1, uw), F32),
                        pltpu.VMEM((units, 1, uw), F32), pltpu.VMEM((units, d, uw), F32)],
    )
    return pl.pallas_call(
        functools.partial(_nsa_sel_kernel, tq=tq, tk=tk, kv_heads=kv_heads, grp=NSA_GROUP, hpu=hpu),
        grid_spec=grid_spec,
        out_shape=jax.ShapeDtypeStruct((kv_heads * gw, t), F32),
        compiler_params=_params("arbitrary"),
        name="nsa_sel_attn",
    )(qi_tab, kj_tab, qt, mb, ks_aug, vst, q_norm, k_norm_max)


def _nsa_win_kernel(qt_ref, k0_ref, k1_ref, k2_ref, v0_ref, v1_ref, v2_ref, ocmp_ref, osel_ref, zg_ref, y_ref,
                    *, tq, grp):
    d = HEAD_DIM
    qi = pl.program_id(1)
    k = jnp.concatenate([k0_ref[0], k1_ref[0], k2_ref[0]], axis=0)
    vt = jnp.concatenate([v0_ref[0], v1_ref[0], v2_ref[0]], axis=1)
    kpos = (qi - 2) * tq + lax.broadcasted_iota(jnp.int32, (3 * tq, tq), 0)
    rel = qi * tq + lax.broadcasted_iota(jnp.int32, (3 * tq, tq), 1) - kpos
    ok = (rel >= 0) & (rel < WINDOW) & (kpos >= 0)
    gates_t = _sigmoid(zg_ref[...]).T

    def scores(j):
        return jnp.dot(k, qt_ref[j * d:(j + 1) * d, :], preferred_element_type=F32)

    s_next = scores(0)
    for j in range(grp):
        rs = slice(j * d, (j + 1) * d)
        s = s_next
        if j + 1 < grp:
            s_next = scores(j + 1)
        sm = jnp.where(ok, s, NEG_INF)
        p = jnp.where(ok, jnp.exp2(sm - jnp.max(sm, axis=0, keepdims=True)), 0.0)
        o = _dot(vt, p) * (1.0 / jnp.sum(p, axis=0, keepdims=True))
        y = (gates_t[3 * j:3 * j + 1] * ocmp_ref[rs, :] + gates_t[3 * j + 1:3 * j + 2] * osel_ref[rs, :]
             + gates_t[3 * j + 2:3 * j + 3] * o)
        y_ref[:, rs] = y.T.astype(y_ref.dtype)


def nsa_win_combine(qt, kw, vwt, o_cmp, o_sel, zs, *, kv_heads, tq, zs_block0):
    t = qt.shape[1]
    d = HEAD_DIM
    gw = NSA_GROUP * d
    k_specs = [pl.BlockSpec((1, tq, d), functools.partial(lambda g, i, b: (g, jnp.maximum(i - b, 0), 0), b=b))
               for b in (2, 1, 0)]
    v_specs = [pl.BlockSpec((1, d, tq), functools.partial(lambda g, i, b: (g, 0, jnp.maximum(i - b, 0)), b=b))
               for b in (2, 1, 0)]
    tspec = pl.BlockSpec((gw, tq), lambda g, i: (g, i))
    return pl.pallas_call(
        functools.partial(_nsa_win_kernel, tq=tq, grp=NSA_GROUP),
        grid=(kv_heads, t // tq),
        in_specs=[tspec] + k_specs + v_specs + [tspec, tspec,
                                                pl.BlockSpec((tq, 128), lambda g, i: (i, zs_block0 + g))],
        out_specs=pl.BlockSpec((tq, gw), lambda g, i: (i, g)),
        out_shape=jax.ShapeDtypeStruct((t, kv_heads * gw), MXU_DTYPE),
        compiler_params=_params("parallel", "parallel"),
        name="nsa_win_combine",
    )(qt, kw, kw, kw, vwt, vwt, vwt, o_cmp, o_sel, zs)


def _split_in_proj(w_in, b_in, sizes):
    starts = np.concatenate([[0], np.cumsum(sizes)])
    i_gate, i_mli, i_mlf = 11, 16, 17
    g0, g1 = starts[i_gate], starts[i_gate + 1]
    half = (g1 - g0) // 2

    def pad(a, width):
        return jnp.pad(a, [(0, 0)] * (a.ndim - 1) + [(0, width - a.shape[-1])])

    def part_b(a):
        return jnp.concatenate([a[..., g1:starts[i_mli]], pad(a[..., g0:g0 + half], 128),
                                pad(a[..., g0 + half:g1], 128),
                                pad(a[..., starts[i_mli]:starts[i_mlf + 1]], 128)], axis=-1)

    return int(g0), part_b(w_in), part_b(b_in)


def kernel(x, mem, norm_mix, w_in, b_in, hgrn_lb_logits, hgrn_norm, nsa_q_norm, nsa_k_norm, nsa_cmp_pos, nsa_cmp_w, mlstm_conv, mlstm_norm, w_out, norm_xattn, norm_mem, xa_wq, xa_wk, xa_wv, xa_wo, xa_q_norm, xa_k_norm, norm_mlp, mlp_w1, mlp_w2):
    _, t, d_model = x.shape
    depth = w_in.shape[0]
    d = HEAD_DIM
    hg_heads = ml_heads = d_model // (4 * d)
    nsa_heads = d_model // (2 * d)
    kv_heads = nsa_heads // NSA_GROUP
    hw, nw, kvw = hg_heads * d, nsa_heads * d, kv_heads * d
    sizes = (hw,) * 4 + (nw,) + (kvw,) * 6 + (3 * nsa_heads,) + (hw,) * 4 + (ml_heads, ml_heads)
    col_nsa_q = 4 * hw
    col_nsa_kc = col_nsa_q + nw
    xa_heads = 4
    bf = MXU_DTYPE

    half = d // 2
    inv_freq = ROPE_THETA ** (-jnp.arange(half, dtype=F32) / half)

    def rope_tables(pos):
        ang = pos[:, None] * inv_freq[None, :]
        cos, sin = jnp.cos(ang), jnp.sin(ang)
        return jnp.concatenate([cos, cos], axis=-1), jnp.concatenate([-sin, sin], axis=-1)

    cos_t, sin_t = rope_tables(jnp.arange(t, dtype=F32))
    ncp = t // CMP_STRIDE
    n_cmp, n_sel = ncp - 1, t // SEL_BLOCK
    cos_c, sin_c = rope_tables(jnp.arange(ncp, dtype=F32) * CMP_STRIDE + (CMP_LEN - 1))
    et = jnp.asarray(np.arange(t)[:, None] // SEL_BLOCK == np.arange(SEL_LANES)[None, :], bf)
    c_start = np.arange(ncp)[:, None] * CMP_STRIDE
    s_start = np.arange(SEL_LANES)[None, :] * SEL_BLOCK
    overlap = ((c_start < s_start + SEL_BLOCK) & (c_start + CMP_LEN > s_start)
               & (np.arange(ncp)[:, None] < n_cmp) & (np.arange(SEL_LANES)[None, :] < n_sel))
    ovt = jnp.asarray(overlap.T, bf)

    n_a, w_b, b_b = _split_in_proj(w_in, b_in, sizes)
    w_a, w_b = w_in[:, :, :n_a].astype(bf), w_b.astype(bf)
    w_out_c, wq_c, wo_c = (a.astype(bf) for a in (w_out, xa_wq, xa_wo))
    w1_c, w2_c, cmp_w_c = mlp_w1.astype(bf), mlp_w2.astype(bf), nsa_cmp_w.astype(bf)

    h = x.reshape(t, d_model)
    mem2 = mem.reshape(mem.shape[1], d_model)
    hn = rms_cast(h, norm_mix[0], tm=512)
    for layer in range(depth):
        z = matmul_cols(hn, w_a, b_in, layer=layer, n_tiles=3, tm=1024, tn=n_a // 3)
        zb = matmul_cols(hn, w_b, b_b, layer=layer, n_tiles=1, tm=512, tn=w_b.shape[2])
        gate_block = 4 * hw // 128
        gt = zb[:, (gate_block + 2) * 128:(gate_block + 2) * 128 + 8].T

        y_hg = hgrn2(z, hgrn_lb_logits, hgrn_norm[layer], layer=layer, heads=hg_heads, tb=512)
        y_ml = mlstm(zb, zb, gt, mlstm_conv[layer], mlstm_norm[layer], heads=ml_heads, tb=512, col0=0,
                     zs_block=gate_block + 2)

        qt, ks_aug, vst, kw, vwt, q_nrm, k_nrm = nsa_prep(z, cos_t, sin_t, et, nsa_q_norm[layer], nsa_k_norm[layer],
                                                          heads=nsa_heads, kv_heads=kv_heads, tq=512, col_q=col_nsa_q)
        kc, vct = nsa_compress(z, nsa_cmp_pos[layer], cmp_w_c[layer], nsa_k_norm[layer],
                               cos_c, sin_c, kv_heads=kv_heads, col_kc=col_nsa_kc)
        o_cmp, mb = nsa_cmp_select(qt, kc, vct, ovt, kv_heads=kv_heads, tq=512)
        o_sel = nsa_sel_attn(qt, mb, ks_aug, vst, q_nrm, k_nrm, kv_heads=kv_heads, tq=512, tk=512, hpu=1)
        y_ns = nsa_win_combine(qt, kw, vwt, o_cmp, o_sel, zb, kv_heads=kv_heads, tq=256, zs_block0=gate_block)

        kb = hw
        terms = [(y_hg, 0, 0)] + [(y_ns, c, 1 + c) for c in range(nw // kb)] + [(y_ml, 0, 1 + nw // kb)]
        h, hn = matmul_residual(terms, w_out_c, h, norm_xattn[layer], layer=layer, kb=kb, tm=512)

        k_mem, v_mem = xa_kv(mem2, norm_mem[layer], xa_wk, xa_wv, xa_k_norm[layer], layer=layer, heads=xa_heads)
        q_x = matmul_cols(hn, wq_c, None, layer=layer, n_tiles=1, tm=1024, tn=d_model)
        o_x = xa_attn(q_x, xa_q_norm[layer], k_mem, v_mem, heads=xa_heads, tm=512)
        h = matmul_residual([(o_x, c, c) for c in range(d_model // kb)], wo_c, h, None, layer=layer, kb=kb, tm=512)

        if layer + 1 < depth:
            h, hn = mlp(h, norm_mlp[layer], w1_c, w2_c, norm_mix[layer + 1], layer=layer, tm=512, tf=1024)
        else:
            h = mlp(h, norm_mlp[layer], w1_c, w2_c, None, layer=layer, tm=512, tf=1024)
    return h.reshape(x.shape)
```

```python
import functools

import jax
import jax.numpy as jnp
import numpy as np
from jax import lax
from jax.experimental import pallas as pl
from jax.experimental.pallas import tpu as pltpu

F32 = jnp.float32
MXU_DTYPE = jnp.bfloat16

EPS = 1e-6
NEG_INF = -1e30
LOG2_E = 1.4426950408889634
HEAD_DIM = 128
ROPE_THETA = 10000.0

V7X_VMEM_LIMIT_BYTES = 56 * 1024 * 1024


def _params(*semantics):
    return pltpu.CompilerParams(dimension_semantics=semantics, vmem_limit_bytes=V7X_VMEM_LIMIT_BYTES)


def _rms(x, gain):
    return x * lax.rsqrt(jnp.mean(x * x, axis=-1, keepdims=True) + EPS) * gain


def _dot(a, b):
    return jnp.dot(a.astype(MXU_DTYPE), b.astype(MXU_DTYPE), preferred_element_type=F32)


def _dot_nt(a, b):
    return lax.dot_general(a.astype(MXU_DTYPE), b.astype(MXU_DTYPE), (((1,), (1,)), ((), ())),
                           preferred_element_type=F32)


def _dot_tn(a, b):
    return lax.dot_general(a.astype(MXU_DTYPE), b.astype(MXU_DTYPE), (((0,), (0,)), ((), ())),
                           preferred_element_type=F32)


def _rms_cast_kernel(x_ref, g_ref, o_ref):
    o_ref[...] = _rms(x_ref[...], g_ref[...]).astype(o_ref.dtype)


def rms_cast(x, gain, *, tm):
    m, d = x.shape
    return pl.pallas_call(
        _rms_cast_kernel,
        grid=(m // tm,),
        in_specs=[pl.BlockSpec((tm, d), lambda i: (i, 0)), pl.BlockSpec((1, d), lambda i: (0, 0))],
        out_specs=pl.BlockSpec((tm, d), lambda i: (i, 0)),
        out_shape=jax.ShapeDtypeStruct((m, d), MXU_DTYPE),
        compiler_params=_params("parallel"),
        name="rms_cast",
    )(x, gain.reshape(1, d))


def _matmul_bias_kernel(h_ref, w_ref, b_ref, o_ref):
    o_ref[...] = jnp.dot(h_ref[...], w_ref[...], preferred_element_type=F32) + b_ref[...]


def _matmul_kernel(h_ref, w_ref, o_ref):
    o_ref[...] = jnp.dot(h_ref[...], w_ref[...], preferred_element_type=F32)


def matmul_cols(h, w, bias, *, layer, n_tiles, tm, tn):
    m, d = h.shape
    in_specs = [pl.BlockSpec((tm, d), lambda j, i: (i, 0)),
                pl.BlockSpec((None, d, tn), lambda j, i: (layer, 0, j))]
    args = [h, w]
    if bias is not None:
        in_specs.append(pl.BlockSpec((None, 1, tn), lambda j, i: (layer, 0, j)))
        args.append(bias.reshape(bias.shape[0], 1, bias.shape[1]))
    return pl.pallas_call(
        _matmul_kernel if bias is None else _matmul_bias_kernel,
        grid=(n_tiles, m // tm),
        in_specs=in_specs,
        out_specs=pl.BlockSpec((tm, tn), lambda j, i: (i, j)),
        out_shape=jax.ShapeDtypeStruct((m, n_tiles * tn), F32),
        compiler_params=_params("parallel", "parallel"),
        name="matmul_cols",
    )(*args)


def _matmul_res_kernel(*refs, n_terms, with_norm):
    ys, ws = refs[:n_terms], refs[n_terms:2 * n_terms]
    res_ref = refs[2 * n_terms]
    acc = res_ref[...]
    for y_ref, w_ref in zip(ys, ws):
        acc = acc + jnp.dot(y_ref[...].astype(MXU_DTYPE), w_ref[...], preferred_element_type=F32)
    if with_norm:
        g_ref, o_ref, hn_ref = refs[2 * n_terms + 1:]
        hn_ref[...] = _rms(acc, g_ref[...]).astype(hn_ref.dtype)
    else:
        o_ref = refs[2 * n_terms + 1]
    o_ref[...] = acc


def matmul_residual(terms, w, res, next_gain, *, layer, kb, tm):
    m, n = res.shape
    y_specs = [pl.BlockSpec((tm, kb), functools.partial(lambda i, c: (i, c), c=cb)) for _, cb, _ in terms]
    w_specs = [pl.BlockSpec((None, kb, n), functools.partial(lambda i, r: (layer, r, 0), r=rb)) for _, _, rb in terms]
    row_spec = pl.BlockSpec((tm, n), lambda i: (i, 0))
    with_norm = next_gain is not None
    return pl.pallas_call(
        functools.partial(_matmul_res_kernel, n_terms=len(terms), with_norm=with_norm),
        grid=(m // tm,),
        in_specs=y_specs + w_specs + [row_spec] + ([pl.BlockSpec((1, n), lambda i: (0, 0))] if with_norm else []),
        out_specs=[row_spec, row_spec] if with_norm else row_spec,
        out_shape=([jax.ShapeDtypeStruct((m, n), F32), jax.ShapeDtypeStruct((m, n), MXU_DTYPE)] if with_norm
                   else jax.ShapeDtypeStruct((m, n), F32)),
        compiler_params=_params("parallel"),
        name="matmul_residual",
    )(*[t[0] for t in terms], *([w] * len(terms)), res, *([next_gain.reshape(1, n)] if with_norm else []))


def _mlp_kernel(*refs, with_norm):
    if with_norm:
        x_ref, g_ref, w1_ref, w2_ref, gn_ref, o_ref, hn_ref, h_scr = refs
    else:
        x_ref, g_ref, w1_ref, w2_ref, o_ref, h_scr = refs

    @pl.when(pl.program_id(1) == 0)
    def _():
        x = x_ref[...]
        h_scr[...] = _rms(x, g_ref[...]).astype(h_scr.dtype)
        o_ref[...] = x

    u = jnp.dot(h_scr[...], w1_ref[...], preferred_element_type=F32)
    a = jnp.square(jnp.maximum(u, 0.0)).astype(MXU_DTYPE)
    o_ref[...] += jnp.dot(a, w2_ref[...], preferred_element_type=F32)

    if with_norm:
        @pl.when(pl.program_id(1) == pl.num_programs(1) - 1)
        def _():
            hn_ref[...] = _rms(o_ref[...], gn_ref[...]).astype(hn_ref.dtype)


def mlp(x, gain, w1, w2, next_gain, *, layer, tm, tf):
    m, d = x.shape
    ff = w1.shape[2]
    row_spec = pl.BlockSpec((tm, d), lambda i, f: (i, 0))
    gain_spec = pl.BlockSpec((1, d), lambda i, f: (0, 0))
    with_norm = next_gain is not None
    return pl.pallas_call(
        functools.partial(_mlp_kernel, with_norm=with_norm),
        grid=(m // tm, ff // tf),
        in_specs=[row_spec, gain_spec,
                  pl.BlockSpec((None, d, tf), lambda i, f: (layer, 0, f)),
                  pl.BlockSpec((None, tf, d), lambda i, f: (layer, f, 0))] + ([gain_spec] if with_norm else []),
        out_specs=[row_spec, row_spec] if with_norm else row_spec,
        out_shape=([jax.ShapeDtypeStruct((m, d), F32), jax.ShapeDtypeStruct((m, d), MXU_DTYPE)] if with_norm
                   else jax.ShapeDtypeStruct((m, d), F32)),
        scratch_shapes=[pltpu.VMEM((tm, d), MXU_DTYPE)],
        compiler_params=_params("parallel", "arbitrary"),
        name="mlp",
    )(x, gain.reshape(1, d), w1, w2, *([next_gain.reshape(1, d)] if with_norm else []))


def _xa_kv_kernel(mem_ref, g_ref, wk_ref, wv_ref, kg_ref, k_ref, v_ref):
    mn = _rms(mem_ref[...], g_ref[...])
    k_ref[...] = _rms(_dot(mn, wk_ref[...]), kg_ref[...]).astype(k_ref.dtype)
    v_ref[...] = _dot(mn, wv_ref[...]).astype(v_ref.dtype)


def xa_kv(mem, gain, wk, wv, k_gain, *, layer, heads):
    ml, d = mem.shape
    hd = d // heads
    return pl.pallas_call(
        _xa_kv_kernel,
        grid=(heads,),
        in_specs=[
            pl.BlockSpec((ml, d), lambda h: (0, 0)),
            pl.BlockSpec((1, d), lambda h: (0, 0)),
            pl.BlockSpec((None, d, hd), lambda h: (layer, 0, h)),
            pl.BlockSpec((None, d, hd), lambda h: (layer, 0, h)),
            pl.BlockSpec((1, hd), lambda h: (0, 0)),
        ],
        out_specs=[pl.BlockSpec((ml, hd), lambda h: (0, h))] * 2,
        out_shape=[jax.ShapeDtypeStruct((ml, d), MXU_DTYPE)] * 2,
        compiler_params=_params("parallel"),
        name="xa_kv",
    )(mem, gain.reshape(1, d), wk, wv, k_gain.reshape(1, hd))


def _xa_attn_kernel(q_ref, qg_ref, k_ref, v_ref, o_ref, *, heads, hd):
    scale = hd ** -0.5
    for h in range(heads):
        cs = slice(h * hd, (h + 1) * hd)
        q = _rms(q_ref[:, cs], qg_ref[...]) * scale
        s = _dot_nt(q, k_ref[:, cs])
        p = jnp.exp(s - jnp.max(s, axis=-1, keepdims=True))
        p = p / jnp.sum(p, axis=-1, keepdims=True)
        o_ref[:, cs] = _dot(p, v_ref[:, cs]).astype(o_ref.dtype)


def xa_attn(q, q_gain, k, v, *, heads, tm):
    m, d = q.shape
    ml = k.shape[0]
    hd = d // heads
    return pl.pallas_call(
        functools.partial(_xa_attn_kernel, heads=heads, hd=hd),
        grid=(m // tm,),
        in_specs=[
            pl.BlockSpec((tm, d), lambda i: (i, 0)),
            pl.BlockSpec((1, hd), lambda i: (0, 0)),
            pl.BlockSpec((ml, d), lambda i: (0, 0)),
            pl.BlockSpec((ml, d), lambda i: (0, 0)),
        ],
        out_specs=pl.BlockSpec((tm, d), lambda i: (i, 0)),
        out_shape=jax.ShapeDtypeStruct((m, d), MXU_DTYPE),
        compiler_params=_params("parallel"),
        name="xa_attn",
    )(q, q_gain.reshape(1, hd), k, v)


def _seg_cumsum_rows(x, seg):
    pos = lax.broadcasted_iota(jnp.int32, x.shape, 0) % seg
    k = 1
    while k < seg:
        x = x + jnp.where(pos >= k, pltpu.roll(x, k, axis=0), 0.0)
        k *= 2
    return x


def _seg_cumsum_lanes(x, seg):
    pos = lax.broadcasted_iota(jnp.int32, x.shape, 1) % seg
    k = 1
    while k < seg:
        x = x + jnp.where(pos >= k, pltpu.roll(x, k, axis=1), 0.0)
        k *= 2
    return x


def _sigmoid(x):
    return 1.0 / (1.0 + jnp.exp(-x))


def _log_sigmoid(x):
    return jnp.minimum(x, 0.0) - jnp.log(1.0 + jnp.exp(-jnp.abs(x)))


HG_SUB = 16
HG_MIN_FORGET = 1e-20


def _hgrn2_kernel(q_ref, f_ref, i_ref, g_ref, lbl_ref, ng_ref, y_ref, st_ref, *, layer, heads, tb):
    dk = HEAD_DIM

    @pl.when(pl.program_id(0) == 0)
    def _():
        st_ref[...] = jnp.zeros_like(st_ref)

    logits = lbl_ref[...]
    e = jnp.exp(logits - jnp.max(logits, axis=0, keepdims=True))
    probs = e / jnp.sum(e, axis=0, keepdims=True)
    lb_all = jnp.zeros_like(probs[0:1])
    for l in range(1, layer + 1):
        lb_all = lb_all + probs[l:l + 1]
    half = HG_SUB // 2
    row_half = lax.broadcasted_iota(jnp.int32, (half, dk), 0)

    n_sub = tb // HG_SUB

    def gates(c):
        r = pl.ds(pl.multiple_of(c * HG_SUB, HG_SUB), HG_SUB)
        pre = []
        for h in range(heads):
            cs = slice(h * dk, (h + 1) * dk)
            lb = lb_all[:, cs]
            q, zf, v = q_ref[r, cs], f_ref[r, cs], i_ref[r, cs]
            forget = lb + (1.0 - lb) * _sigmoid(zf)
            log_f = jnp.log(jnp.maximum(forget, HG_MIN_FORGET))
            key = (1.0 - lb) * _sigmoid(-zf)
            b = _seg_cumsum_rows(log_f, HG_SUB) * LOG2_E
            pre.append((q, v, key, b))
        return pre

    def body(c, pre):
        r = pl.ds(pl.multiple_of(c * HG_SUB, HG_SUB), HG_SUB)
        pre_next = gates(jnp.minimum(c + 1, n_sub - 1))
        pre = [(q, v, key, b, b[HG_SUB - 1:HG_SUB]) for q, v, key, b in pre]
        inter = [_dot_nt(q * jnp.exp2(b), st_ref[h]) for h, (q, v, key, b, b_end) in enumerate(pre)]
        upd = [_dot_tn(v, key * jnp.exp2(b_end - b)) for q, v, key, b, b_end in pre]
        for h in range(heads):
            cs = slice(h * dk, (h + 1) * dk)
            q, v, key, b, b_end = pre[h]
            q_lo, q_hi, b_lo, b_hi = q[0:half], q[half:], b[0:half], b[half:]
            o_lo = jnp.zeros((half, dk), F32)
            o_hi = jnp.zeros((half, dk), F32)
            for s in range(HG_SUB):
                bs, ks, vs = b[s:s + 1], key[s:s + 1], v[s:s + 1]
                if s < half:
                    decay = jnp.exp2(jnp.where(row_half >= s, b_lo - bs, NEG_INF))
                    o_lo = o_lo + jnp.sum(q_lo * decay * ks, axis=-1, keepdims=True) * vs
                    decay = jnp.exp2(b_hi - bs)
                else:
                    decay = jnp.exp2(jnp.where(row_half >= s - half, b_hi - bs, NEG_INF))
                o_hi = o_hi + jnp.sum(q_hi * decay * ks, axis=-1, keepdims=True) * vs
            o = jnp.concatenate([o_lo, o_hi], axis=0) + inter[h]
            st_ref[h] = jnp.exp2(b_end) * st_ref[h] + upd[h]
            g = g_ref[r, cs]
            y = _rms(o, ng_ref[:, cs]) * (g * _sigmoid(g))
            y_ref[r, cs] = y.astype(y_ref.dtype)
        return pre_next

    lax.fori_loop(0, n_sub, body, gates(jnp.int32(0)))


def hgrn2(z, lb_logits, norm_gain, *, layer, heads, tb):
    t = z.shape[0]
    w = heads * HEAD_DIM
    depth = lb_logits.shape[0]
    return pl.pallas_call(
        functools.partial(_hgrn2_kernel, layer=layer, heads=heads, tb=tb),
        grid=(t // tb,),
        in_specs=[pl.BlockSpec((tb, w), functools.partial(lambda i, c: (i, c), c=c)) for c in range(4)] + [
            pl.BlockSpec((depth, w), lambda i: (0, 0)),
            pl.BlockSpec((1, w), lambda i: (0, 0)),
        ],
        out_specs=pl.BlockSpec((tb, w), lambda i: (i, 0)),
        out_shape=jax.ShapeDtypeStruct((t, w), MXU_DTYPE),
        scratch_shapes=[pltpu.VMEM((heads, HEAD_DIM, HEAD_DIM), F32)],
        compiler_params=_params("arbitrary"),
        name="hgrn2",
    )(z, z, z, z, lb_logits, norm_gain.reshape(1, w))


ML_CHUNK = 64
ML_CONV = 4
ML_TAIL = 8
ML_AUG = HEAD_DIM + 16
ZS_ML_I = 0
ZS_ML_F = 4


def _mlstm_kernel(q_ref, k_ref, v_ref, o_ref, zs_ref, gt_ref, cw_ref, ng_ref, y_ref,
                  conv_scr, st_scr, m_scr, *, heads, tb):
    d = HEAD_DIM
    w = heads * d

    @pl.when(pl.program_id(0) == 0)
    def _():
        conv_scr[0:ML_TAIL, :] = jnp.zeros((ML_TAIL, 2 * w), F32)
        st_scr[...] = jnp.zeros_like(st_scr)
        m_scr[...] = jnp.zeros_like(m_scr)

    conv_scr[ML_TAIL:ML_TAIL + tb, 0:w] = q_ref[...]
    conv_scr[ML_TAIL:ML_TAIL + tb, w:2 * w] = k_ref[...]
    acc = jnp.zeros((tb, 2 * w), F32)
    for j in range(ML_CONV):
        off = ML_TAIL - (ML_CONV - 1) + j
        acc = acc + conv_scr[off:off + tb, :] * cw_ref[j:j + 1, :]
    conv_scr[0:ML_TAIL, :] = conv_scr[tb:tb + ML_TAIL, :]
    qk = acc * _sigmoid(acc)

    pair = 2 * ML_CHUNK
    zs = zs_ref[...]
    b_col_all = _seg_cumsum_rows(_log_sigmoid(zs), ML_CHUNK)
    gt = gt_ref[...]
    b_row_all = _seg_cumsum_lanes(_log_sigmoid(gt), ML_CHUNK)
    s_idx = lax.broadcasted_iota(jnp.int32, (pair, pair), 0)
    t_idx = lax.broadcasted_iota(jnp.int32, (pair, pair), 1)
    visible = (s_idx // ML_CHUNK == t_idx // ML_CHUNK) & (s_idx <= t_idx)
    lane = lax.broadcasted_iota(jnp.int32, (1, pair), 1)
    in_chunk = [lane < ML_CHUNK, lane >= ML_CHUNK]
    ones_rows = (lax.broadcasted_iota(jnp.int32, (ML_AUG - d, pair), 0) == 0).astype(F32)

    def rows(p):
        return slice(p * pair, (p + 1) * pair)

    def stage_scores(p):
        out = []
        for h in range(heads):
            qt = (qk[rows(p), h * d:(h + 1) * d] * (d ** -0.5)).T
            out.append((qt, _dot(qk[rows(p), w + h * d:w + (h + 1) * d], qt)))
        return out

    def stage_free(p, sc):
        rs = rows(p)
        out = []
        for h in range(heads):
            qt, s_raw = sc[h]
            b_row = b_row_all[heads + h:heads + h + 1, rs]
            li_row = gt[h:h + 1, rs]
            u_col = b_col_all[rs, ZS_ML_F + h:ZS_ML_F + h + 1] - zs[rs, ZS_ML_I + h:ZS_ML_I + h + 1]
            d_log = jnp.where(visible, b_row - u_col, NEG_INF)
            dmax = jnp.max(d_log, axis=0, keepdims=True)
            s_t = s_raw * jnp.exp(d_log - dmax)
            vt_aug = jnp.concatenate([v_ref[rs, h * d:(h + 1) * d].T, ones_rows], axis=0)
            kc = qk[rs, w + h * d:w + (h + 1) * d]
            per_chunk = []
            for c in range(2):
                b_end = b_row[:, (c + 1) * ML_CHUNK - 1:(c + 1) * ML_CHUNK]
                w_state = jnp.where(in_chunk[c], b_end - b_row + li_row, NEG_INF)
                wmax = jnp.max(w_state, axis=1, keepdims=True)
                w_row = jnp.exp(w_state - wmax)
                per_chunk.append((b_end, wmax, _dot(vt_aug * w_row, kc)))
            out.append(dict(qt=qt, b_row=b_row, dmax=dmax, intra=_dot(vt_aug, s_t), per_chunk=per_chunk))
        return out

    n_pairs = tb // pair
    sc = {0: stage_scores(0)}
    if n_pairs > 1:
        sc[1] = stage_scores(1)
    free = {0: stage_free(0, sc.pop(0))}
    for p in range(n_pairs):
        rs = rows(p)
        if p + 2 < n_pairs:
            sc[p + 2] = stage_scores(p + 2)
        if p + 1 < n_pairs:
            free[p + 1] = stage_free(p + 1, sc.pop(p + 1))
        cur = free.pop(p)
        for h in range(heads):
            cs = slice(h * d, (h + 1) * d)
            f = cur[h]
            num, m_tok = None, None
            for c in range(2):
                b_end, wmax, upd = f["per_chunk"][c]
                state, m_prev = st_scr[h], m_scr[h]
                inter = _dot(state, f["qt"])
                inter_log = f["b_row"] + m_prev
                m_t = jnp.maximum(inter_log, f["dmax"])
                num_c = jnp.exp(inter_log - m_t) * inter + jnp.exp(f["dmax"] - m_t) * f["intra"]
                num = num_c if c == 0 else jnp.where(in_chunk[0], num, num_c)
                m_tok = m_t if c == 0 else jnp.where(in_chunk[0], m_tok, m_t)
                m_new = jnp.maximum(b_end + m_prev, wmax)
                st_scr[h] = jnp.exp(b_end + m_prev - m_new) * state + jnp.exp(wmax - m_new) * upd
                m_scr[h] = m_new
            qn = num[d:d + 1]
            hcell = num[0:d] / jnp.maximum(jnp.abs(qn), jnp.exp(-m_tok))
            hn = hcell * lax.rsqrt(jnp.mean(hcell * hcell, axis=0, keepdims=True) + EPS)
            y = hn.T * ng_ref[:, cs] * _sigmoid(o_ref[rs, cs])
            y_ref[rs, cs] = y.astype(y_ref.dtype)


def mlstm(z, zs, gt, conv_w, norm_gain, *, heads, tb, col0, zs_block):
    t = z.shape[0]
    w = heads * HEAD_DIM
    cb = col0 // w
    return pl.pallas_call(
        functools.partial(_mlstm_kernel, heads=heads, tb=tb),
        grid=(t // tb,),
        in_specs=[pl.BlockSpec((tb, w), functools.partial(lambda i, c: (i, c), c=cb + c)) for c in range(4)] + [
            pl.BlockSpec((tb, 128), lambda i: (i, zs_block)),
            pl.BlockSpec((8, tb), lambda i: (0, i)),
            pl.BlockSpec((ML_CONV, 2 * w), lambda i: (0, 0)),
            pl.BlockSpec((1, w), lambda i: (0, 0)),
        ],
        out_specs=pl.BlockSpec((tb, w), lambda i: (i, 0)),
        out_shape=jax.ShapeDtypeStruct((t, w), MXU_DTYPE),
        scratch_shapes=[
            pltpu.VMEM((tb + ML_TAIL, 2 * w), F32),
            pltpu.VMEM((heads, ML_AUG, HEAD_DIM), F32),
            pltpu.VMEM((heads, 1, 1), F32),
        ],
        compiler_params=_params("arbitrary"),
        name="mlstm",
    )(z, z, z, z, zs, gt, conv_w, norm_gain.reshape(1, w))


NSA_GROUP = 4
CMP_LEN = 32
CMP_STRIDE = 16
SEL_BLOCK = 64
N_SELECT = 16
WINDOW = 512
FORCE_BONUS = 1e4
SEL_LANES = 128
SEL_LAG_MAX_GAP = 64.0
SEL_BOUND_MARGIN = 1.02


def _rope(x, cos_t, sin_t):
    return x * cos_t + pltpu.roll(x, HEAD_DIM // 2, axis=1) * sin_t


def _nsa_prep_kernel(q_ref, ks_ref, vs_ref, kw_ref, vw_ref, cos_ref, sin_ref, et_ref, qg_ref, kg_ref,
                     qh_ref, ksa_ref, vso_ref, kwr_ref, vwo_ref, qn_ref, kn_ref, *, heads, kv_heads):
    d = HEAD_DIM
    cos_t, sin_t = cos_ref[...], sin_ref[...]
    for h in range(heads):
        cs = slice(h * d, (h + 1) * d)
        qt = (_rope(_rms(q_ref[:, cs], qg_ref[...]), cos_t, sin_t) * (d ** -0.5 * LOG2_E)).T
        qh_ref[cs, :] = qt.astype(qh_ref.dtype)
        qn_ref[h:h + 1, :] = jnp.sqrt(jnp.sum(qt * qt, axis=0, keepdims=True))
    for g in range(kv_heads):
        cs = slice(g * d, (g + 1) * d)
        ks = _rope(_rms(ks_ref[:, cs], kg_ref[1:2]), cos_t, sin_t)
        ksa_ref[g, :, 0:d] = ks.astype(ksa_ref.dtype)
        k_norm2 = jnp.max(jnp.sum(ks * ks, axis=-1, keepdims=True), axis=0, keepdims=True)
        kn_ref[g, 0] = jnp.broadcast_to(jnp.sqrt(k_norm2), kn_ref.shape[2:])
        ksa_ref[g, :, d:2 * d] = et_ref[...]
        kwr_ref[g] = _rope(_rms(kw_ref[:, cs], kg_ref[2:3]), cos_t, sin_t).astype(kwr_ref.dtype)
        vso_ref[g] = vs_ref[:, cs].T.astype(vso_ref.dtype)
        vwo_ref[g] = vw_ref[:, cs].T.astype(vwo_ref.dtype)


def nsa_prep(z, cos_t, sin_t, et, q_gain, k_gains, *, heads, kv_heads, tq, col_q):
    t = z.shape[0]
    d = HEAD_DIM
    kvw = kv_heads * d
    qb = col_q // (heads * d)
    kb = (col_q + heads * d) // kvw

    def zcol(width, c):
        return pl.BlockSpec((tq, width), functools.partial(lambda i, c: (i, c), c=c))

    k_out = pl.BlockSpec((kv_heads, tq, d), lambda i: (0, i, 0))
    k_shape = jax.ShapeDtypeStruct((kv_heads, t, d), MXU_DTYPE)
    vt_out = pl.BlockSpec((kv_heads, d, tq), lambda i: (0, 0, i))
    vt_shape = jax.ShapeDtypeStruct((kv_heads, d, t), MXU_DTYPE)
    return pl.pallas_call(
        functools.partial(_nsa_prep_kernel, heads=heads, kv_heads=kv_heads),
        grid=(t // tq,),
        in_specs=[zcol(heads * d, qb), zcol(kvw, kb + 2), zcol(kvw, kb + 3), zcol(kvw, kb + 4), zcol(kvw, kb + 5),
                  pl.BlockSpec((tq, d), lambda i: (i, 0)), pl.BlockSpec((tq, d), lambda i: (i, 0)),
                  pl.BlockSpec((tq, SEL_LANES), lambda i: (i, 0)),
                  pl.BlockSpec((1, d), lambda i: (0, 0)), pl.BlockSpec((3, d), lambda i: (0, 0))],
        out_specs=[pl.BlockSpec((heads * d, tq), lambda i: (0, i)),
                   pl.BlockSpec((kv_heads, tq, 2 * d), lambda i: (0, i, 0)), vt_out, k_out, vt_out,
                   pl.BlockSpec((heads, tq), lambda i: (0, i)),
                   pl.BlockSpec((kv_heads, 1, 8, d), lambda i: (0, i, 0, 0))],
        out_shape=[jax.ShapeDtypeStruct((heads * d, t), MXU_DTYPE),
                   jax.ShapeDtypeStruct((kv_heads, t, 2 * d), MXU_DTYPE), vt_shape, k_shape, vt_shape,
                   jax.ShapeDtypeStruct((heads, t), F32),
                   jax.ShapeDtypeStruct((kv_heads, t // tq, 8, d), F32)],
        compiler_params=_params("parallel"),
        name="nsa_prep",
    )(z, z, z, z, z, cos_t, sin_t, et, q_gain.reshape(1, d), k_gains)


def _nsa_compress_kernel(zk_ref, zv_ref, pe_ref, w_ref, kg_ref, cos_ref, sin_ref, kc_ref, vc_ref, *, ncp):
    d = HEAD_DIM
    half = CMP_LEN // 2
    row = lax.broadcasted_iota(jnp.int32, (ncp, d), 0)

    def compress(z_ref, which):
        lo = jnp.zeros((ncp, d), F32)
        hi = jnp.zeros((ncp, d), F32)
        for l in range(half):
            xl = z_ref[pl.ds(l, ncp, stride=CMP_STRIDE), :]
            lo = lo + _dot(xl + pe_ref[which, l:l + 1, :], w_ref[which, l])
            hi = hi + _dot(xl + pe_ref[which, half + l:half + l + 1, :], w_ref[which, half + l])
        out = lo + pltpu.roll(hi, ncp - 1, axis=0)
        return jnp.where(row < ncp - 1, out, 0.0)

    kc_ref[0] = _rope(_rms(compress(zk_ref, 0), kg_ref[0:1]), cos_ref[...], sin_ref[...]).astype(kc_ref.dtype)
    vc_ref[0] = compress(zv_ref, 1).T.astype(vc_ref.dtype)


def nsa_compress(z, cmp_pos, cmp_w, k_gains, cos_c, sin_c, *, kv_heads, col_kc):
    t = z.shape[0]
    d = HEAD_DIM
    ncp = t // CMP_STRIDE
    kb = col_kc // d
    out_spec = pl.BlockSpec((1, ncp, d), lambda g: (g, 0, 0))
    out_shape = jax.ShapeDtypeStruct((kv_heads, ncp, d), MXU_DTYPE)
    return pl.pallas_call(
        functools.partial(_nsa_compress_kernel, ncp=ncp),
        grid=(kv_heads,),
        in_specs=[pl.BlockSpec((t, d), lambda g: (0, kb + g)),
                  pl.BlockSpec((t, d), lambda g: (0, kb + kv_heads + g)),
                  pl.BlockSpec((2, CMP_LEN, d), lambda g: (0, 0, 0)),
                  pl.BlockSpec((2, CMP_LEN, d, d), lambda g: (0, 0, 0, 0)),
                  pl.BlockSpec((3, d), lambda g: (0, 0)),
                  pl.BlockSpec((ncp, d), lambda g: (0, 0)), pl.BlockSpec((ncp, d), lambda g: (0, 0))],
        out_specs=[out_spec, pl.BlockSpec((1, d, ncp), lambda g: (g, 0, 0))],
        out_shape=[out_shape, jax.ShapeDtypeStruct((kv_heads, d, ncp), MXU_DTYPE)],
        compiler_params=_params("parallel"),
        name="nsa_compress",
    )(z, z, cmp_pos, cmp_w, k_gains, cos_c, sin_c)


def _nsa_cmp_kernel(qt_ref, kc_ref, vct_ref, ovt_ref, ocmp_ref, mb_ref, *, tq, grp, ncp):
    d = HEAD_DIM
    qi = pl.program_id(1)
    cmp_end = lax.broadcasted_iota(jnp.int32, (ncp, tq), 0) * CMP_STRIDE + (CMP_LEN - 1)
    vis = cmp_end <= qi * tq + lax.broadcasted_iota(jnp.int32, (ncp, tq), 1)
    psum = None

    def scores(j):
        return jnp.dot(kc_ref[0], qt_ref[j * d:(j + 1) * d, :], preferred_element_type=F32)

    s_next = scores(0)
    for j in range(grp):
        s = s_next
        if j + 1 < grp:
            s_next = scores(j + 1)
        sm = jnp.where(vis, s, NEG_INF)
        p = jnp.where(vis, jnp.exp2(sm - jnp.max(sm, axis=0, keepdims=True)), 0.0)
        l = jnp.sum(p, axis=0, keepdims=True)
        p = p * (1.0 / jnp.where(l > 0.0, l, 1.0))
        ocmp_ref[j * d:(j + 1) * d, :] = _dot(vct_ref[0], p)
        psum = p if psum is None else psum + p
    hi = psum.astype(MXU_DTYPE)
    lo = (psum - hi.astype(F32)).astype(MXU_DTYPE)
    imp = (jnp.dot(ovt_ref[...], hi, preferred_element_type=F32)
           + jnp.dot(ovt_ref[...], lo, preferred_element_type=F32))
    blk = lax.broadcasted_iota(jnp.int32, (SEL_LANES, tq), 0)
    cur = (qi * tq + lax.broadcasted_iota(jnp.int32, (SEL_LANES, tq), 1)) // SEL_BLOCK
    forced = (blk == 0) | (blk == cur) | (blk == cur - 1)
    score = jnp.where(blk <= cur, imp + jnp.where(forced, FORCE_BONUS, 0.0), NEG_INF)
    blk_f = blk.astype(F32)
    bias = jnp.full((SEL_LANES, tq), NEG_INF, F32)
    for _ in range(N_SELECT):
        mx = jnp.max(score, axis=0, keepdims=True)
        first = jnp.min(jnp.where(score == mx, blk_f, float(SEL_LANES)), axis=0, keepdims=True)
        pick = blk_f == first
        bias = jnp.where(pick, jnp.where(mx > 0.5 * NEG_INF, 0.0, NEG_INF), bias)
        score = jnp.where(pick, -jnp.inf, score)
    mb_ref[0] = bias.astype(mb_ref.dtype)


def nsa_cmp_select(qt, kc, vct, ovt, *, kv_heads, tq):
    t = qt.shape[1]
    d = HEAD_DIM
    ncp = kc.shape[1]
    gw = NSA_GROUP * d
    return pl.pallas_call(
        functools.partial(_nsa_cmp_kernel, tq=tq, grp=NSA_GROUP, ncp=ncp),
        grid=(kv_heads, t // tq),
        in_specs=[pl.BlockSpec((gw, tq), lambda g, i: (g, i)),
                  pl.BlockSpec((1, ncp, d), lambda g, i: (g, 0, 0)),
                  pl.BlockSpec((1, d, ncp), lambda g, i: (g, 0, 0)),
                  pl.BlockSpec((SEL_LANES, ncp), lambda g, i: (0, 0))],
        out_specs=[pl.BlockSpec((gw, tq), lambda g, i: (g, i)),
                   pl.BlockSpec((1, SEL_LANES, tq), lambda g, i: (g, 0, i))],
        out_shape=[jax.ShapeDtypeStruct((kv_heads * gw, t), F32),
                   jax.ShapeDtypeStruct((kv_heads, SEL_LANES, t), MXU_DTYPE)],
        compiler_params=_params("parallel", "parallel"),
        name="nsa_cmp_select",
    )(qt, kc, vct, ovt)


def _nsa_sel_kernel(qi_tab, kj_tab, qt_ref, mb_ref, ka_ref, vt_ref, qn_ref, kn_ref, o_ref,
                    qa_scr, m_scr, l_scr, acc_scr,
                    *, tq, tk, kv_heads, grp, hpu):
    d = HEAD_DIM
    step = pl.program_id(0)
    qi, kj = qi_tab[step], kj_tab[step]
    last = (qi * tq + tq - 1) // tk
    heads = kv_heads * grp
    units = heads // hpu
    uw = hpu * tq

    def kv_of(u):
        return u * hpu // grp

    @pl.when(kj == 0)
    def _():
        for j in range(heads):
            u, ls = j // hpu, slice((j % hpu) * tq, (j % hpu + 1) * tq)
            qa_scr[u, 0:d, ls] = qt_ref[j * d:(j + 1) * d, :]
            qa_scr[u, d:2 * d, ls] = mb_ref[j // grp]
        m_scr[...] = jnp.full_like(m_scr, NEG_INF)
        l_scr[...] = jnp.zeros_like(l_scr)
        acc_scr[...] = jnp.zeros_like(acc_scr)

    def masked_scores(diagonal):
        if diagonal:
            causal = (kj * tk + lax.broadcasted_iota(jnp.int32, (tk, uw), 0)
                      <= qi * tq + lax.broadcasted_iota(jnp.int32, (tk, uw), 1) % tq)

        def scores(u):
            return jnp.dot(ka_ref[kv_of(u)], qa_scr[u], preferred_element_type=F32)

        s_next = scores(0)
        for u in range(units):
            s = s_next
            if u + 1 < units:
                s_next = scores(u + 1)
            yield u, (jnp.where(causal, s, NEG_INF) if diagonal else s)

    def update(diagonal):
        for u, s in masked_scores(diagonal):
            m_prev = m_scr[u]
            m_new = jnp.maximum(m_prev, jnp.max(s, axis=0, keepdims=True))
            alpha = jnp.exp2(m_prev - m_new)
            p = jnp.exp2(s - m_new)
            l_scr[u] = alpha * l_scr[u] + jnp.sum(p, axis=0, keepdims=True)
            acc_scr[u] = alpha * acc_scr[u] + _dot(vt_ref[kv_of(u)], p)
            m_scr[u] = m_new

    def update_lagged(diagonal):
        gap = None
        for u in range(units):
            heads_u = range(u * hpu, (u + 1) * hpu)
            q_norm = jnp.concatenate([qn_ref[j:j + 1, :] for j in heads_u], axis=1)
            bound = q_norm * (kn_ref[kv_of(u), 0, 0:1, 0:1] * SEL_BOUND_MARGIN)
            gap = bound - m_scr[u] if gap is None else jnp.maximum(gap, bound - m_scr[u])
        safe = jnp.max(gap) <= SEL_LAG_MAX_GAP

        @pl.when(safe)
        def _():
            for u, s in masked_scores(diagonal):
                m_prev = m_scr[u]
                p = jnp.exp2(s - m_prev)
                m_new = jnp.maximum(m_prev, jnp.max(s, axis=0, keepdims=True))
                alpha = jnp.exp2(m_prev - m_new)
                l_scr[u] = (l_scr[u] + jnp.sum(p, axis=0, keepdims=True)) * alpha
                acc_scr[u] = (acc_scr[u] + _dot(vt_ref[kv_of(u)], p)) * alpha
                m_scr[u] = m_new

        @pl.when(jnp.logical_not(safe))
        def _():
            update(diagonal)

    def finalize():
        for j in range(heads):
            u, ls = j // hpu, slice((j % hpu) * tq, (j % hpu + 1) * tq)
            o_ref[j * d:(j + 1) * d, :] = acc_scr[u, :, ls] * (1.0 / l_scr[u, :, ls])

    @pl.when((kj == 0) & (kj < last))
    def _():
        update(False)

    @pl.when((kj == 0) & (kj == last))
    def _():
        update(True)
        finalize()

    @pl.when((kj > 0) & (kj < last))
    def _():
        update_lagged(False)

    @pl.when((kj > 0) & (kj == last))
    def _():
        update_lagged(True)
        finalize()


def nsa_sel_attn(qt, mb, ks_aug, vst, q_norm, k_norm_max, *, kv_heads, tq, tk, hpu):
    t = qt.shape[1]
    assert k_norm_max.shape[1] * tk == t
    d = HEAD_DIM
    gw = NSA_GROUP * d
    pairs = [(qi, kj) for qi in range(t // tq) for kj in range((qi * tq + tq - 1) // tk + 1)]
    qi_tab = jnp.asarray(np.array([p[0] for p in pairs], np.int32))
    kj_tab = jnp.asarray(np.array([p[1] for p in pairs], np.int32))
    units, uw = kv_heads * NSA_GROUP // hpu, hpu * tq
    grid_spec = pltpu.PrefetchScalarGridSpec(
        num_scalar_prefetch=2,
        grid=(len(pairs),),
        in_specs=[pl.BlockSpec((kv_heads * gw, tq), lambda s, qt, kt: (0, qt[s])),
                  pl.BlockSpec((kv_heads, SEL_LANES, tq), lambda s, qt, kt: (0, 0, qt[s])),
                  pl.BlockSpec((kv_heads, tk, 2 * d), lambda s, qt, kt: (0, kt[s], 0)),
                  pl.BlockSpec((kv_heads, d, tk), lambda s, qt, kt: (0, 0, kt[s])),
                  pl.BlockSpec((kv_heads * NSA_GROUP, tq), lambda s, qt, kt: (0, qt[s])),
                  pl.BlockSpec((kv_heads, 1, 8, d), lambda s, qt, kt: (0, kt[s], 0, 0))],
        out_specs=pl.BlockSpec((kv_heads * gw, tq), lambda s, qt, kt: (0, qt[s])),
        scratch_shapes=[pltpu.VMEM((units, 2 * d, uw), MXU_DTYPE), pltpu.VMEM((units, 1, uw), F32),
                        pltpu.VMEM((units, 1, uw), F32), pltpu.VMEM((units, d, uw), F32)],
    )
    return pl.pallas_call(
        functools.partial(_nsa_sel_kernel, tq=tq, tk=tk, kv_heads=kv_heads, grp=NSA_GROUP, hpu=hpu),
        grid_spec=grid_spec,
        out_shape=jax.ShapeDtypeStruct((kv_heads * gw, t), F32),
        compiler_params=_params("arbitrary"),
        name="nsa_sel_attn",
    )(qi_tab, kj_tab, qt, mb, ks_aug, vst, q_norm, k_norm_max)


def _nsa_win_kernel(qt_ref, k0_ref, k1_ref, k2_ref, v0_ref, v1_ref, v2_ref, ocmp_ref, osel_ref, zg_ref, y_ref,
                    *, tq, grp):
    d = HEAD_DIM
    qi = pl.program_id(1)
    k = jnp.concatenate([k0_ref[0], k1_ref[0], k2_ref[0]], axis=0)
    vt = jnp.concatenate([v0_ref[0], v1_ref[0], v2_ref[0]], axis=1)
    kpos = (qi - 2) * tq + lax.broadcasted_iota(jnp.int32, (3 * tq, tq), 0)
    rel = qi * tq + lax.broadcasted_iota(jnp.int32, (3 * tq, tq), 1) - kpos
    ok = (rel >= 0) & (rel < WINDOW) & (kpos >= 0)
    gates_t = _sigmoid(zg_ref[...]).T

    def scores(j):
        return jnp.dot(k, qt_ref[j * d:(j + 1) * d, :], preferred_element_type=F32)

    s_next = scores(0)
    for j in range(grp):
        rs = slice(j * d, (j + 1) * d)
        s = s_next
        if j + 1 < grp:
            s_next = scores(j + 1)
        sm = jnp.where(ok, s, NEG_INF)
        p = jnp.where(ok, jnp.exp2(sm - jnp.max(sm, axis=0, keepdims=True)), 0.0)
        o = _dot(vt, p) * (1.0 / jnp.sum(p, axis=0, keepdims=True))
        y = (gates_t[3 * j:3 * j + 1] * ocmp_ref[rs, :] + gates_t[3 * j + 1:3 * j + 2] * osel_ref[rs, :]
             + gates_t[3 * j + 2:3 * j + 3] * o)
        y_ref[:, rs] = y.T.astype(y_ref.dtype)


def nsa_win_combine(qt, kw, vwt, o_cmp, o_sel, zs, *, kv_heads, tq, zs_block0):
    t = qt.shape[1]
    d = HEAD_DIM
    gw = NSA_GROUP * d
    k_specs = [pl.BlockSpec((1, tq, d), functools.partial(lambda g, i, b: (g, jnp.maximum(i - b, 0), 0), b=b))
               for b in (2, 1, 0)]
    v_specs = [pl.BlockSpec((1, d, tq), functools.partial(lambda g, i, b: (g, 0, jnp.maximum(i - b, 0)), b=b))
               for b in (2, 1, 0)]
    tspec = pl.BlockSpec((gw, tq), lambda g, i: (g, i))
    return pl.pallas_call(
        functools.partial(_nsa_win_kernel, tq=tq, grp=NSA_GROUP),
        grid=(kv_heads, t // tq),
        in_specs=[tspec] + k_specs + v_specs + [tspec, tspec,
                                                pl.BlockSpec((tq, 128), lambda g, i: (i, zs_block0 + g))],
        out_specs=pl.BlockSpec((tq, gw), lambda g, i: (i, g)),
        out_shape=jax.ShapeDtypeStruct((t, kv_heads * gw), MXU_DTYPE),
        compiler_params=_params("parallel", "parallel"),
        name="nsa_win_combine",
    )(qt, kw, kw, kw, vwt, vwt, vwt, o_cmp, o_sel, zs)


def _split_in_proj(w_in, b_in, sizes):
    starts = np.concatenate([[0], np.cumsum(sizes)])
    i_gate, i_mli, i_mlf = 11, 16, 17
    g0, g1 = starts[i_gate], starts[i_gate + 1]
    half = (g1 - g0) // 2

    def pad(a, width):
        return jnp.pad(a, [(0, 0)] * (a.ndim - 1) + [(0, width - a.shape[-1])])

    def part_b(a):
        return jnp.concatenate([a[..., g1:starts[i_mli]], pad(a[..., g0:g0 + half], 128),
                                pad(a[..., g0 + half:g1], 128),
                                pad(a[..., starts[i_mli]:starts[i_mlf + 1]], 128)], axis=-1)

    return int(g0), part_b(w_in), part_b(b_in)


def kernel(x, mem, norm_mix, w_in, b_in, hgrn_lb_logits, hgrn_norm, nsa_q_norm, nsa_k_norm, nsa_cmp_pos, nsa_cmp_w, mlstm_conv, mlstm_norm, w_out, norm_xattn, norm_mem, xa_wq, xa_wk, xa_wv, xa_wo, xa_q_norm, xa_k_norm, norm_mlp, mlp_w1, mlp_w2):
    batch, t, d_model = x.shape
    depth = w_in.shape[0]
    d = HEAD_DIM
    assert batch == 1 and t % 1024 == 0 and t // SEL_BLOCK <= SEL_LANES and d_model % (4 * d) == 0
    hg_heads = ml_heads = d_model // (4 * d)
    nsa_heads = d_model // (2 * d)
    kv_heads = nsa_heads // NSA_GROUP
    hw, nw, kvw = hg_heads * d, nsa_heads * d, kv_heads * d
    sizes = (hw,) * 4 + (nw,) + (kvw,) * 6 + (3 * nsa_heads,) + (hw,) * 4 + (ml_heads, ml_heads)
    col_nsa_q = 4 * hw
    col_nsa_kc = col_nsa_q + nw
    xa_heads = 4
    bf = MXU_DTYPE

    half = d // 2
    inv_freq = ROPE_THETA ** (-jnp.arange(half, dtype=F32) / half)

    def rope_tables(pos):
        ang = pos[:, None] * inv_freq[None, :]
        cos, sin = jnp.cos(ang), jnp.sin(ang)
        return jnp.concatenate([cos, cos], axis=-1), jnp.concatenate([-sin, sin], axis=-1)

    cos_t, sin_t = rope_tables(jnp.arange(t, dtype=F32))
    ncp = t // CMP_STRIDE
    n_cmp, n_sel = ncp - 1, t // SEL_BLOCK
    cos_c, sin_c = rope_tables(jnp.arange(ncp, dtype=F32) * CMP_STRIDE + (CMP_LEN - 1))
    et = jnp.asarray(np.arange(t)[:, None] // SEL_BLOCK == np.arange(SEL_LANES)[None, :], bf)
    c_start = np.arange(ncp)[:, None] * CMP_STRIDE
    s_start = np.arange(SEL_LANES)[None, :] * SEL_BLOCK
    overlap = ((c_start < s_start + SEL_BLOCK) & (c_start + CMP_LEN > s_start)
               & (np.arange(ncp)[:, None] < n_cmp) & (np.arange(SEL_LANES)[None, :] < n_sel))
    ovt = jnp.asarray(overlap.T, bf)

    n_a, w_b, b_b = _split_in_proj(w_in, b_in, sizes)
    w_a, w_b = w_in[:, :, :n_a].astype(bf), w_b.astype(bf)
    w_out_c, wq_c, wo_c = (a.astype(bf) for a in (w_out, xa_wq, xa_wo))
    w1_c, w2_c, cmp_w_c = mlp_w1.astype(bf), mlp_w2.astype(bf), nsa_cmp_w.astype(bf)

    h = x.reshape(t, d_model)
    mem2 = mem.reshape(mem.shape[1], d_model)
    hn = rms_cast(h, norm_mix[0], tm=512)
    for layer in range(depth):
        z = matmul_cols(hn, w_a, b_in, layer=layer, n_tiles=3, tm=1024, tn=n_a // 3)
        zb = matmul_cols(hn, w_b, b_b, layer=layer, n_tiles=1, tm=512, tn=w_b.shape[2])
        gate_block = 4 * hw // 128
        gt = zb[:, (gate_block + 2) * 128:(gate_block + 2) * 128 + 8].T

        y_hg = hgrn2(z, hgrn_lb_logits, hgrn_norm[layer], layer=layer, heads=hg_heads, tb=512)
        y_ml = mlstm(zb, zb, gt, mlstm_conv[layer], mlstm_norm[layer], heads=ml_heads, tb=512, col0=0,
                     zs_block=gate_block + 2)

        qt, ks_aug, vst, kw, vwt, q_nrm, k_nrm = nsa_prep(z, cos_t, sin_t, et, nsa_q_norm[layer], nsa_k_norm[layer],
                                                          heads=nsa_heads, kv_heads=kv_heads, tq=512, col_q=col_nsa_q)
        kc, vct = nsa_compress(z, nsa_cmp_pos[layer], cmp_w_c[layer], nsa_k_norm[layer],
                               cos_c, sin_c, kv_heads=kv_heads, col_kc=col_nsa_kc)
        o_cmp, mb = nsa_cmp_select(qt, kc, vct, ovt, kv_heads=kv_heads, tq=512)
        o_sel = nsa_sel_attn(qt, mb, ks_aug, vst, q_nrm, k_nrm, kv_heads=kv_heads, tq=512, tk=512, hpu=1)
        y_ns = nsa_win_combine(qt, kw, vwt, o_cmp, o_sel, zb, kv_heads=kv_heads, tq=256, zs_block0=gate_block)

        kb = hw
        terms = [(y_hg, 0, 0)] + [(y_ns, c, 1 + c) for c in range(nw // kb)] + [(y_ml, 0, 1 + nw // kb)]
        h, hn = matmul_residual(terms, w_out_c, h, norm_xattn[layer], layer=layer, kb=kb, tm=512)

        k_mem, v_mem = xa_kv(mem2, norm_mem[layer], xa_wk, xa_wv, xa_k_norm[layer], layer=layer, heads=xa_heads)
        q_x = matmul_cols(hn, wq_c, None, layer=layer, n_tiles=1, tm=1024, tn=d_model)
        o_x = xa_attn(q_x, xa_q_norm[layer], k_mem, v_mem, heads=xa_heads, tm=512)
        h = matmul_residual([(o_x, c, c) for c in range(d_model // kb)], wo_c, h, None, layer=layer, kb=kb, tm=512)

        if layer + 1 < depth:
            h, hn = mlp(h, norm_mlp[layer], w1_c, w2_c, norm_mix[layer + 1], layer=layer, tm=512, tf=1024)
        else:
            h = mlp(h, norm_mlp[layer], w1_c, w2_c, None, layer=layer, tm=512, tf=1024)
    return h.reshape(x.shape)
```
